```python
import math, functools
import jax, jax.numpy as jnp
from jax import lax
import numpy as np

D_MODEL = 2048
BATCH = 8
SEQ = 8192
DEPTH = 4

GRID_W = 64
CTX_LEN = 256
HEAD_DIM = 128
RET_HEADS = 4
DN_HEADS = 4
ATT_HEADS = 8
ATT_KV_HEADS = 2
RET_W = RET_HEADS * HEAD_DIM
DN_W = DN_HEADS * HEAD_DIM
ATT_W = ATT_HEADS * HEAD_DIM
ATT_KV_W = ATT_KV_HEADS * HEAD_DIM
MIX_W = RET_W + DN_W + ATT_W
RET_CHUNK = 128
DN_CHUNK = 64
DN_CONV_K = 5
Q_BLOCK = 128
ROPE_THETA = 10000.0
D_FF = ((8 * D_MODEL + 3 * 256 - 1) // (3 * 256)) * 256
DEEPNORM_ALPHA = (2 * DEPTH) ** 0.25
DEEPNORM_BETA = (8 * DEPTH) ** -0.25
EPS = 1e-6
SPLIT_SIZES = (RET_W, RET_W, RET_W, RET_W, 3 * DN_W, DN_W, 2 * DN_HEADS, 2 * DN_HEADS, ATT_W, ATT_KV_W, ATT_KV_W)
PROJ_W = sum(SPLIT_SIZES)
SPLIT_POINTS = tuple(np.cumsum(SPLIT_SIZES)[:-1].tolist())

kernel_name = "hybrid_ret_gdn_gqa_diffusion_block"


def layer_norm(x, w, b):
    xf = x.astype(jnp.float32)
    mu = jnp.mean(xf, -1, keepdims=True)
    var = jnp.mean(jnp.square(xf - mu), -1, keepdims=True)
    return (xf - mu) * lax.rsqrt(var + EPS) * w + b


def rms_norm(x, w=None):
    xf = x.astype(jnp.float32)
    y = xf * lax.rsqrt(jnp.mean(xf * xf, -1, keepdims=True) + EPS)
    if w is not None:
        y = y * w
    return y.astype(x.dtype)


def l2_normalize(x):
    return x * lax.rsqrt(jnp.sum(x * x, -1, keepdims=True) + EPS)


def split_heads(a, n_heads):
    return a.reshape(a.shape[:-1] + (n_heads, HEAD_DIM))


def modulate(h, shift, scale):
    return h * (1.0 + scale) + shift


def post_norm(x, y, w, b):
    return layer_norm(DEEPNORM_ALPHA * x + y, w, b).astype(x.dtype)


def axial_rope(n_tokens):
    rows = n_tokens // GRID_W
    row = jnp.repeat(jnp.arange(rows, dtype=jnp.float32), GRID_W)
    col = jnp.tile(jnp.arange(GRID_W, dtype=jnp.float32), rows)
    n_freq = HEAD_DIM // 4
    inv = ROPE_THETA ** (-jnp.arange(n_freq, dtype=jnp.float32) / n_freq)
    ang = jnp.concatenate([row[:, None] * inv, col[:, None] * inv], -1)
    return jnp.cos(ang), jnp.sin(ang)


def apply_rope(x, cos, sin):
    xf = x.astype(jnp.float32)
    x1, x2 = jnp.split(xf, 2, -1)
    c = cos[None, :, None, :]
    s = sin[None, :, None, :]
    return jnp.concatenate([x1 * c - x2 * s, x1 * s + x2 * c], -1).astype(x.dtype)


def bidirectional(scan_f, scan_b, ctx_f, lat_f, ctx_b, lat_b, s0):
    flip = lambda seq: tuple(jnp.flip(a, axis=1) for a in seq)
    o_cf, s_cf = scan_f(*ctx_f, s0)
    o_lf, _ = scan_f(*lat_f, s_cf)
    o_cb, s_cb = scan_b(*flip(ctx_b), s0)
    o_lb, _ = scan_b(*flip(lat_b), s_cb)
    return o_cf + jnp.flip(o_cb, 1), o_lf + jnp.flip(o_lb, 1)


def retention_scan(q, k, v, s0, log_gamma):
    b, l, h, d = q.shape
    n = l // RET_CHUNK
    qc = q.reshape(b, n, RET_CHUNK, h, d)
    kc = k.reshape(b, n, RET_CHUNK, h, d)
    vc = v.reshape(b, n, RET_CHUNK, h, d)
    pos = jnp.arange(RET_CHUNK, dtype=jnp.float32)
    rel = pos[:, None] - pos[None, :]
    decay = jnp.where(rel >= 0, jnp.exp(jnp.maximum(rel, 0.0)[None] * log_gamma[:, None, None]), 0.0)
    intra = jnp.einsum('bnihd,bnjhd->bnhij', qc, kc) * decay
    o_intra = jnp.einsum('bnhij,bnjhd->bnihd', intra, vc)
    q_decay = jnp.exp((pos + 1.0)[:, None] * log_gamma[None, :])
    k_decay = jnp.exp((RET_CHUNK - 1.0 - pos)[:, None] * log_gamma[None, :])
    chunk_kv = jnp.einsum('bnjhd,jh,bnjhe->nbhde', kc, k_decay, vc)
    chunk_decay = jnp.exp(RET_CHUNK * log_gamma)[None, :, None, None]

    def step(s, u):
        return s * chunk_decay + u, s

    s_fin, s_prev = lax.scan(step, s0, chunk_kv)
    o_inter = jnp.einsum('bnihd,ih,nbhde->bnihe', qc, q_decay, s_prev)
    return (o_intra + o_inter).reshape(b, l, h, d), s_fin


def to_chunks(a, c):
    b, l = a.shape[:2]
    return jnp.swapaxes(a.reshape((b, l // c, c) + a.shape[2:]), 2, 3)


def gated_delta_scan(q, k, v, g, beta, s0):
    b, l, h, _ = q.shape
    c = DN_CHUNK
    qc, kc, vc = to_chunks(q, c), to_chunks(k, c), to_chunks(v, c)
    gc, bc = to_chunks(g, c), to_chunks(beta, c)
    g_cum = jnp.cumsum(gc, -1)
    tri = jnp.tril(jnp.ones((c, c), bool))
    strict = jnp.tril(jnp.ones((c, c), bool), -1)
    diff = g_cum[..., :, None] - g_cum[..., None, :]
    decay = jnp.where(tri, jnp.exp(jnp.where(tri, diff, 0.0)), 0.0)
    k_beta = kc * bc[..., None]
    v_beta = vc * bc[..., None]
    a = jnp.where(strict, jnp.einsum('bnhid,bnhjd->bnhij', k_beta, kc) * decay, 0.0)
    eye = jnp.eye(c, dtype=a.dtype)
    t = lax.linalg.triangular_solve(eye + a, jnp.broadcast_to(eye, a.shape), left_side=True,
                                    lower=True, unit_diagonal=True)
    w_val = jnp.einsum('bnhij,bnhjd->bnhid', t, v_beta)
    k_cum = jnp.einsum('bnhij,bnhjd->bnhid', t, k_beta * jnp.exp(g_cum)[..., None])
    qk = jnp.einsum('bnhid,bnhjd->bnhij', qc, kc) * decay
    q_g = qc * jnp.exp(g_cum)[..., None]
    k_g = kc * jnp.exp(g_cum[..., -1:] - g_cum)[..., None]
    g_last = jnp.exp(g_cum[..., -1])
    xs = tuple(jnp.moveaxis(z, 1, 0) for z in (w_val, k_cum, qk, q_g, k_g, g_last))

    def step(s, inp):
        w_i, kc_i, qk_i, qg_i, kg_i, gl_i = inp
        v_new = w_i - jnp.einsum('bhcd,bhde->bhce', kc_i, s)
        o = jnp.einsum('bhcd,bhde->bhce', qg_i, s) + jnp.einsum('bhij,bhje->bhie', qk_i, v_new)
        s = s * gl_i[..., None, None] + jnp.einsum('bhcd,bhce->bhde', kg_i, v_new)
        return s, o

    s_fin, o = lax.scan(step, s0, xs)
    return o.transpose(1, 0, 3, 2, 4).reshape(b, l, h, -1), s_fin


def short_conv(x, w):
    pad = DN_CONV_K // 2
    return lax.conv_general_dilated(x, w[:, None, :], window_strides=(1,), padding=[(pad, pad)],
                                    dimension_numbers=('NWC', 'WIO', 'NWC'),
                                    feature_group_count=x.shape[-1])


def block_attention(q, k, v):
    b, lq, h, d = q.shape
    kvh = k.shape[2]
    qb = q.reshape(b, lq // Q_BLOCK, Q_BLOCK, kvh, h // kvh, d).swapaxes(0, 1)

    def one_block(qi):
        s = jnp.einsum('bqhgd,bkhd->bhgqk', qi, k).astype(jnp.float32) * d ** -0.5
        p = jax.nn.softmax(s, axis=-1).astype(v.dtype)
        return jnp.einsum('bhgqk,bkhd->bqhgd', p, v)

    o = lax.map(one_block, qb)
    return o.swapaxes(0, 1).reshape(b, lq, h * d)


def retention_group(pc, pl, decay_logit, cos, sin):
    log_gamma = jax.nn.log_sigmoid(decay_logit.astype(jnp.float32))

    def qkv(p, rotate):
        q, k, v = (split_heads(a, RET_HEADS).astype(jnp.float32) for a in p[:3])
        if rotate:
            q, k = apply_rope(q, cos, sin), apply_rope(k, cos, sin)
        return q, k * HEAD_DIM ** -0.5, v

    ctx_seq = qkv(pc, False)
    lat_seq = qkv(pl, True)
    s0 = jnp.zeros((pc[0].shape[0], RET_HEADS, HEAD_DIM, HEAD_DIM), jnp.float32)
    o_c, o_l = bidirectional(functools.partial(retention_scan, log_gamma=log_gamma[0]),
                             functools.partial(retention_scan, log_gamma=log_gamma[1]),
                             ctx_seq, lat_seq, ctx_seq, lat_seq, s0)

    def out(o, g):
        y = rms_norm(o) * jax.nn.silu(split_heads(g, RET_HEADS).astype(jnp.float32))
        return y.reshape(y.shape[:2] + (RET_W,)).astype(g.dtype)

    return out(o_c, pc[3]), out(o_l, pl[3])


def deltanet_group(pc, pl, conv_w, a_log, dt_bias, norm_w):
    neg_a = -jnp.exp(a_log.astype(jnp.float32))
    dt_b = dt_bias.astype(jnp.float32)

    def prep(p):
        qkv, _, a, bb = p
        qkv = jax.nn.silu(short_conv(qkv, conv_w.astype(qkv.dtype)))
        q, k, v = (split_heads(t, DN_HEADS).astype(jnp.float32) for t in jnp.split(qkv, 3, -1))
        q = l2_normalize(q) * HEAD_DIM ** -0.5
        k = l2_normalize(k)
        bsz, n = a.shape[:2]
        g = neg_a * jax.nn.softplus(a.astype(jnp.float32).reshape(bsz, n, 2, DN_HEADS) + dt_b)
        beta = jax.nn.sigmoid(bb.astype(jnp.float32).reshape(bsz, n, 2, DN_HEADS))
        return (q, k, v, g[:, :, 0], beta[:, :, 0]), (q, k, v, g[:, :, 1], beta[:, :, 1])

    cf, cb = prep(pc)
    lf, lb = prep(pl)
    s0 = jnp.zeros((pc[0].shape[0], DN_HEADS, HEAD_DIM, HEAD_DIM), jnp.float32)
    o_c, o_l = bidirectional(gated_delta_scan, gated_delta_scan, cf, lf, cb, lb, s0)

    def out(o, z):
        y = rms_norm(o, norm_w) * jax.nn.silu(split_heads(z, DN_HEADS).astype(jnp.float32))
        return y.reshape(y.shape[:2] + (DN_W,)).astype(z.dtype)

    return out(o_c, pc[1]), out(o_l, pl[1])


def attention_group(pc, pl, qn_w, kn_w, cos, sin, keep_ctx):
    def qkv(p):
        q = rms_norm(split_heads(p[0], ATT_HEADS), qn_w)
        k = rms_norm(split_heads(p[1], ATT_KV_HEADS), kn_w)
        return q, k, split_heads(p[2], ATT_KV_HEADS)

    qc, kc, vc = qkv(pc)
    ql, kl, vl = qkv(pl)
    ql, kl = apply_rope(ql, cos, sin), apply_rope(kl, cos, sin)
    y_l = block_attention(ql, jnp.concatenate([kc, kl], 1), jnp.concatenate([vc, vl], 1))
    y_c = block_attention(qc, kc, vc) if keep_ctx else None
    return y_c, y_l


def hybrid_mixer(h_ctx, h_lat, w_in, ret_decay_logit, dn_conv_w, dn_a_log, dn_dt_bias, dn_norm_w,
                 att_qn_w, att_kn_w, cos, sin, keep_ctx):
    pc = jnp.split(h_ctx @ w_in, SPLIT_POINTS, axis=-1)
    pl = jnp.split(h_lat @ w_in, SPLIT_POINTS, axis=-1)
    rc, rl = retention_group(pc[0:4], pl[0:4], ret_decay_logit, cos, sin)
    dc, dl = deltanet_group(pc[4:8], pl[4:8], dn_conv_w, dn_a_log, dn_dt_bias, dn_norm_w)
    ac, al = attention_group(pc[8:11], pl[8:11], att_qn_w, att_kn_w, cos, sin, keep_ctx)
    y_lat = jnp.concatenate([rl, dl, al], -1)
    y_ctx = jnp.concatenate([rc, dc, ac], -1) if keep_ctx else None
    return y_ctx, y_lat


def swiglu(h, w_in, w_out):
    gate, up = jnp.split(h @ w_in, 2, -1)
    return (jax.nn.silu(gate) * up) @ w_out


def _fwd_setup_inputs(seed: int = 0) -> dict:
    key = jax.random.key(seed)
    ks = jax.random.split(key, 24)
    f32 = jnp.float32

    def nrm(k, shape, scale):
        return jax.random.normal(k, shape, f32) * scale

    base_logit = jnp.log(2.0 ** (5.0 + jnp.arange(RET_HEADS, dtype=f32)) - 1.0)
    dt = jnp.exp(jax.random.uniform(ks[10], (DEPTH, 2, DN_HEADS), f32, math.log(1e-3), math.log(1e-1)))
    return {
        "x": nrm(ks[0], (BATCH, SEQ, D_MODEL), 1.0),
        "c": nrm(ks[1], (BATCH, D_MODEL), 1.0),
        "ctx": nrm(ks[2], (BATCH, CTX_LEN, D_MODEL), 1.0),
        "c_ctx": nrm(ks[3], (D_MODEL,), 1.0),
        "w_ada": nrm(ks[4], (DEPTH, D_MODEL, 6 * D_MODEL), 0.5 * D_MODEL ** -0.5),
        "b_ada": nrm(ks[5], (DEPTH, 6 * D_MODEL), 0.02),
        "w_in": nrm(ks[6], (DEPTH, D_MODEL, PROJ_W), D_MODEL ** -0.5),
        "ret_decay_logit": base_logit + nrm(ks[7], (DEPTH, 2, RET_HEADS), 0.1),
        "dn_conv_w": nrm(ks[8], (DEPTH, DN_CONV_K, 3 * DN_W), DN_CONV_K ** -0.5),
        "dn_a_log": jnp.log(jax.random.uniform(ks[9], (DEPTH, 2, DN_HEADS), f32, 1.0, 16.0)),
        "dn_dt_bias": dt + jnp.log(-jnp.expm1(-dt)),
        "dn_norm_w": 1.0 + nrm(ks[11], (DEPTH, HEAD_DIM), 0.02),
        "att_qn_w": 1.0 + nrm(ks[12], (DEPTH, HEAD_DIM), 0.02),
        "att_kn_w": 1.0 + nrm(ks[13], (DEPTH, HEAD_DIM), 0.02),
        "w_o": nrm(ks[14], (DEPTH, MIX_W, D_MODEL), MIX_W ** -0.5 * DEEPNORM_BETA),
        "ln1_w": 1.0 + nrm(ks[15], (DEPTH, D_MODEL), 0.02),
        "ln1_b": nrm(ks[16], (DEPTH, D_MODEL), 0.02),
        "w_ffn_in": nrm(ks[17], (DEPTH, D_MODEL, 2 * D_FF), D_MODEL ** -0.5),
        "w_ffn_out": nrm(ks[18], (DEPTH, D_FF, D_MODEL), D_FF ** -0.5 * DEEPNORM_BETA),
        "ln2_w": 1.0 + nrm(ks[19], (DEPTH, D_MODEL), 0.02),
        "ln2_b": nrm(ks[20], (DEPTH, D_MODEL), 0.02),
    }


def _fwd_reference(x, c, ctx, c_ctx, w_ada, b_ada, w_in, ret_decay_logit, dn_conv_w, dn_a_log, dn_dt_bias,
              dn_norm_w, att_qn_w, att_kn_w, w_o, ln1_w, ln1_b, w_ffn_in, w_ffn_out, ln2_w, ln2_b):
    cos, sin = axial_rope(x.shape[1])
    cond_lat = jax.nn.silu(c)
    cond_ctx = jax.nn.silu(c_ctx)
    for i in range(DEPTH):
        keep_ctx = i < DEPTH - 1
        m_l = jnp.split((cond_lat @ w_ada[i] + b_ada[i])[:, None, :], 6, -1)
        m_c = jnp.split((cond_ctx @ w_ada[i] + b_ada[i])[None, None, :], 6, -1)
        y_c, y_l = hybrid_mixer(modulate(ctx, m_c[0], m_c[1]), modulate(x, m_l[0], m_l[1]), w_in[i],
                                ret_decay_logit[i], dn_conv_w[i], dn_a_log[i], dn_dt_bias[i], dn_norm_w[i],
                                att_qn_w[i], att_kn_w[i], cos, sin, keep_ctx)
        x = post_norm(x, m_l[2] * (y_l @ w_o[i]), ln1_w[i], ln1_b[i])
        x = post_norm(x, m_l[5] * swiglu(modulate(x, m_l[3], m_l[4]), w_ffn_in[i], w_ffn_out[i]),
                      ln2_w[i], ln2_b[i])
        if keep_ctx:
            ctx = post_norm(ctx, m_c[2] * (y_c @ w_o[i]), ln1_w[i], ln1_b[i])
            ctx = post_norm(ctx, m_c[5] * swiglu(modulate(ctx, m_c[3], m_c[4]), w_ffn_in[i], w_ffn_out[i]),
                            ln2_w[i], ln2_b[i])
    return x


import jax as _jax
import jax.numpy as _jnp

TWIN_FORMAT = 'train_step'
FWD_PARAMS = ['x', 'c', 'ctx', 'c_ctx', 'w_ada', 'b_ada', 'w_in', 'ret_decay_logit', 'dn_conv_w', 'dn_a_log', 'dn_dt_bias', 'dn_norm_w', 'att_qn_w', 'att_kn_w', 'w_o', 'ln1_w', 'ln1_b', 'w_ffn_in', 'w_ffn_out', 'ln2_w', 'ln2_b']
TWIN_WEIGHTS = ['c_ctx', 'w_ada', 'b_ada', 'w_in', 'ret_decay_logit', 'dn_conv_w', 'dn_a_log', 'dn_dt_bias', 'dn_norm_w', 'att_qn_w', 'att_kn_w', 'w_o', 'ln1_w', 'ln1_b', 'w_ffn_in', 'w_ffn_out', 'ln2_w', 'ln2_b']
TWIN_DIFF_INPUT = 'x'
TWIN_INPUTS = ['x', 'c', 'ctx', 'c_ctx', 'w_ada', 'b_ada', 'w_in', 'ret_decay_logit', 'dn_conv_w', 'dn_a_log', 'dn_dt_bias', 'dn_norm_w', 'att_qn_w', 'att_kn_w', 'w_o', 'ln1_w', 'ln1_b', 'w_ffn_in', 'w_ffn_out', 'ln2_w', 'ln2_b', 'loss_target', 'm_c_ctx', 'm_w_ada', 'm_b_ada', 'm_w_in', 'm_ret_decay_logit', 'm_dn_conv_w', 'm_dn_a_log', 'm_dn_dt_bias', 'm_dn_norm_w', 'm_att_qn_w', 'm_att_kn_w', 'm_w_o', 'm_ln1_w', 'm_ln1_b', 'm_w_ffn_in', 'm_w_ffn_out', 'm_ln2_w', 'm_ln2_b', 'v_c_ctx', 'v_w_ada', 'v_b_ada', 'v_w_in', 'v_ret_decay_logit', 'v_dn_conv_w', 'v_dn_a_log', 'v_dn_dt_bias', 'v_dn_norm_w', 'v_att_qn_w', 'v_att_kn_w', 'v_w_o', 'v_ln1_w', 'v_ln1_b', 'v_w_ffn_in', 'v_w_ffn_out', 'v_ln2_w', 'v_ln2_b']
TWIN_OUTPUTS = ['loss', 'grad_x', 'grad_c_ctx', 'grad_w_ada', 'grad_b_ada', 'grad_w_in', 'grad_ret_decay_logit', 'grad_dn_conv_w', 'grad_dn_a_log', 'grad_dn_dt_bias', 'grad_dn_norm_w', 'grad_att_qn_w', 'grad_att_kn_w', 'grad_w_o', 'grad_ln1_w', 'grad_ln1_b', 'grad_w_ffn_in', 'grad_w_ffn_out', 'grad_ln2_w', 'grad_ln2_b', 'delta_c_ctx', 'delta_w_ada', 'delta_b_ada', 'delta_w_in', 'delta_ret_decay_logit', 'delta_dn_conv_w', 'delta_dn_a_log', 'delta_dn_dt_bias', 'delta_dn_norm_w', 'delta_att_qn_w', 'delta_att_kn_w', 'delta_w_o', 'delta_ln1_w', 'delta_ln1_b', 'delta_w_ffn_in', 'delta_w_ffn_out', 'delta_ln2_w', 'delta_ln2_b', 'new_m_c_ctx', 'new_m_w_ada', 'new_m_b_ada', 'new_m_w_in', 'new_m_ret_decay_logit', 'new_m_dn_conv_w', 'new_m_dn_a_log', 'new_m_dn_dt_bias', 'new_m_dn_norm_w', 'new_m_att_qn_w', 'new_m_att_kn_w', 'new_m_w_o', 'new_m_ln1_w', 'new_m_ln1_b', 'new_m_w_ffn_in', 'new_m_w_ffn_out', 'new_m_ln2_w', 'new_m_ln2_b', 'new_v_c_ctx', 'new_v_w_ada', 'new_v_b_ada', 'new_v_w_in', 'new_v_ret_decay_logit', 'new_v_dn_conv_w', 'new_v_dn_a_log', 'new_v_dn_dt_bias', 'new_v_dn_norm_w', 'new_v_att_qn_w', 'new_v_att_kn_w', 'new_v_w_o', 'new_v_ln1_w', 'new_v_ln1_b', 'new_v_w_ffn_in', 'new_v_w_ffn_out', 'new_v_ln2_w', 'new_v_ln2_b']
TWIN_LEAF_KINDS = {'loss': 'loss', 'grad_x': 'grad_x', 'grad_c_ctx': 'grad_w', 'grad_w_ada': 'grad_w', 'grad_b_ada': 'grad_w', 'grad_w_in': 'grad_w', 'grad_ret_decay_logit': 'grad_w', 'grad_dn_conv_w': 'grad_w', 'grad_dn_a_log': 'grad_w', 'grad_dn_dt_bias': 'grad_w', 'grad_dn_norm_w': 'grad_w', 'grad_att_qn_w': 'grad_w', 'grad_att_kn_w': 'grad_w', 'grad_w_o': 'grad_w', 'grad_ln1_w': 'grad_w', 'grad_ln1_b': 'grad_w', 'grad_w_ffn_in': 'grad_w', 'grad_w_ffn_out': 'grad_w', 'grad_ln2_w': 'grad_w', 'grad_ln2_b': 'grad_w', 'delta_c_ctx': 'delta_w', 'delta_w_ada': 'delta_w', 'delta_b_ada': 'delta_w', 'delta_w_in': 'delta_w', 'delta_ret_decay_logit': 'delta_w', 'delta_dn_conv_w': 'delta_w', 'delta_dn_a_log': 'delta_w', 'delta_dn_dt_bias': 'delta_w', 'delta_dn_norm_w': 'delta_w', 'delta_att_qn_w': 'delta_w', 'delta_att_kn_w': 'delta_w', 'delta_w_o': 'delta_w', 'delta_ln1_w': 'delta_w', 'delta_ln1_b': 'delta_w', 'delta_w_ffn_in': 'delta_w', 'delta_w_ffn_out': 'delta_w', 'delta_ln2_w': 'delta_w', 'delta_ln2_b': 'delta_w', 'new_m_c_ctx': 'new_m', 'new_m_w_ada': 'new_m', 'new_m_b_ada': 'new_m', 'new_m_w_in': 'new_m', 'new_m_ret_decay_logit': 'new_m', 'new_m_dn_conv_w': 'new_m', 'new_m_dn_a_log': 'new_m', 'new_m_dn_dt_bias': 'new_m', 'new_m_dn_norm_w': 'new_m', 'new_m_att_qn_w': 'new_m', 'new_m_att_kn_w': 'new_m', 'new_m_w_o': 'new_m', 'new_m_ln1_w': 'new_m', 'new_m_ln1_b': 'new_m', 'new_m_w_ffn_in': 'new_m', 'new_m_w_ffn_out': 'new_m', 'new_m_ln2_w': 'new_m', 'new_m_ln2_b': 'new_m', 'new_v_c_ctx': 'new_v', 'new_v_w_ada': 'new_v', 'new_v_b_ada': 'new_v', 'new_v_w_in': 'new_v', 'new_v_ret_decay_logit': 'new_v', 'new_v_dn_conv_w': 'new_v', 'new_v_dn_a_log': 'new_v', 'new_v_dn_dt_bias': 'new_v', 'new_v_dn_norm_w': 'new_v', 'new_v_att_qn_w': 'new_v', 'new_v_att_kn_w': 'new_v', 'new_v_w_o': 'new_v', 'new_v_ln1_w': 'new_v', 'new_v_ln1_b': 'new_v', 'new_v_w_ffn_in': 'new_v', 'new_v_w_ffn_out': 'new_v', 'new_v_ln2_w': 'new_v', 'new_v_ln2_b': 'new_v'}


def _forward(args):
    return _fwd_reference(*[args[k] for k in FWD_PARAMS])


def _output_shape():
    def fwd():
        inp = _fwd_setup_inputs(0)
        return _fwd_reference(*[inp[k] for k in FWD_PARAMS])
    out = _jax.eval_shape(fwd)
    return out.shape, out.dtype

N_MICROBATCH = 1
ADAM_LR = 0.001
ADAM_B1 = 0.9
ADAM_B2 = 0.999
ADAM_EPS = 1e-08
ADAM_WD = 0.01
ADAM_STEP = 10
PER_EXAMPLE_BATCH_AXIS = {'x': 0, 'c': 0, 'ctx': 0, 'loss_target': 0}
SHARED_INPUTS = []
_WEIGHT_DTYPES = {'c_ctx': _jnp.float32, 'w_ada': _jnp.float32, 'b_ada': _jnp.float32, 'w_in': _jnp.float32, 'ret_decay_logit': _jnp.float32, 'dn_conv_w': _jnp.float32, 'dn_a_log': _jnp.float32, 'dn_dt_bias': _jnp.float32, 'dn_norm_w': _jnp.float32, 'att_qn_w': _jnp.float32, 'att_kn_w': _jnp.float32, 'w_o': _jnp.float32, 'ln1_w': _jnp.float32, 'ln1_b': _jnp.float32, 'w_ffn_in': _jnp.float32, 'w_ffn_out': _jnp.float32, 'ln2_w': _jnp.float32, 'ln2_b': _jnp.float32}
MOMENT_SCALE = {'c_ctx': 4.224244e-03, 'w_ada': 8.813865e-03, 'b_ada': 1.556534e-02, 'w_in': 5.344381e-03, 'ret_decay_logit': 2.012984e-02, 'dn_conv_w': 4.432285e-03, 'dn_a_log': 1.345345e-02, 'dn_dt_bias': 1.299636e-02, 'dn_norm_w': 1.334056e-02, 'att_qn_w': 2.223841e-03, 'att_kn_w': 2.280553e-03, 'w_o': 1.256980e-02, 'ln1_w': 1.136481e+00, 'ln1_b': 5.194139e-01, 'w_ffn_in': 4.122652e-03, 'w_ffn_out': 1.603288e-02, 'ln2_w': 1.608292e+01, 'ln2_b': 9.554371e-01}


def _to_microbatches(a, axis):
    t = _jnp.moveaxis(a, axis, 0)
    t = t.reshape((N_MICROBATCH, t.shape[0] // N_MICROBATCH) + t.shape[1:])
    return _jnp.moveaxis(t, 1, axis + 1)


def setup_inputs(seed: int = 0) -> dict:
    inp = _fwd_setup_inputs(seed)
    key = _jax.random.fold_in(_jax.random.key(seed), 7919)
    shape, _ = _output_shape()
    out = dict(inp)
    out["loss_target"] = _jax.random.normal(_jax.random.fold_in(key, 0), shape, _jnp.float32)
    for i, name in enumerate(TWIN_WEIGHTS):
        w = inp[name].astype(_jnp.float32)
        if MOMENT_SCALE is None:
            s = _jnp.sqrt(_jnp.mean(_jnp.square(w)) + 1e-30)
        else:
            s = MOMENT_SCALE[name]
        km, kv = _jax.random.split(_jax.random.fold_in(key, i + 1))
        out[name] = w
        out["m_" + name] = s * _jax.random.normal(km, w.shape, _jnp.float32)
        out["v_" + name] = (s * s) * _jax.random.uniform(kv, w.shape, _jnp.float32, 0.5, 1.5)
    if N_MICROBATCH > 1:
        for name, axis in PER_EXAMPLE_BATCH_AXIS.items():
            out[name] = _to_microbatches(out[name], axis)
    return {'x': out['x'], 'c': out['c'], 'ctx': out['ctx'], 'c_ctx': out['c_ctx'], 'w_ada': out['w_ada'], 'b_ada': out['b_ada'], 'w_in': out['w_in'], 'ret_decay_logit': out['ret_decay_logit'], 'dn_conv_w': out['dn_conv_w'], 'dn_a_log': out['dn_a_log'], 'dn_dt_bias': out['dn_dt_bias'], 'dn_norm_w': out['dn_norm_w'], 'att_qn_w': out['att_qn_w'], 'att_kn_w': out['att_kn_w'], 'w_o': out['w_o'], 'ln1_w': out['ln1_w'], 'ln1_b': out['ln1_b'], 'w_ffn_in': out['w_ffn_in'], 'w_ffn_out': out['w_ffn_out'], 'ln2_w': out['ln2_w'], 'ln2_b': out['ln2_b'], 'loss_target': out['loss_target'], 'm_c_ctx': out['m_c_ctx'], 'm_w_ada': out['m_w_ada'], 'm_b_ada': out['m_b_ada'], 'm_w_in': out['m_w_in'], 'm_ret_decay_logit': out['m_ret_decay_logit'], 'm_dn_conv_w': out['m_dn_conv_w'], 'm_dn_a_log': out['m_dn_a_log'], 'm_dn_dt_bias': out['m_dn_dt_bias'], 'm_dn_norm_w': out['m_dn_norm_w'], 'm_att_qn_w': out['m_att_qn_w'], 'm_att_kn_w': out['m_att_kn_w'], 'm_w_o': out['m_w_o'], 'm_ln1_w': out['m_ln1_w'], 'm_ln1_b': out['m_ln1_b'], 'm_w_ffn_in': out['m_w_ffn_in'], 'm_w_ffn_out': out['m_w_ffn_out'], 'm_ln2_w': out['m_ln2_w'], 'm_ln2_b': out['m_ln2_b'], 'v_c_ctx': out['v_c_ctx'], 'v_w_ada': out['v_w_ada'], 'v_b_ada': out['v_b_ada'], 'v_w_in': out['v_w_in'], 'v_ret_decay_logit': out['v_ret_decay_logit'], 'v_dn_conv_w': out['v_dn_conv_w'], 'v_dn_a_log': out['v_dn_a_log'], 'v_dn_dt_bias': out['v_dn_dt_bias'], 'v_dn_norm_w': out['v_dn_norm_w'], 'v_att_qn_w': out['v_att_qn_w'], 'v_att_kn_w': out['v_att_kn_w'], 'v_w_o': out['v_w_o'], 'v_ln1_w': out['v_ln1_w'], 'v_ln1_b': out['v_ln1_b'], 'v_w_ffn_in': out['v_w_ffn_in'], 'v_w_ffn_out': out['v_w_ffn_out'], 'v_ln2_w': out['v_ln2_w'], 'v_ln2_b': out['v_ln2_b']}


def _loss(weights, diff, rest, loss_target):
    with _jax.named_scope("forward"):
        args = {**rest, TWIN_DIFF_INPUT: diff, **{k: w.astype(_WEIGHT_DTYPES[k]) for k, w in weights.items()}}
        y = _forward(args)
    with _jax.named_scope("loss_head"):
        err = _jnp.square(y.astype(_jnp.float32) - loss_target)
        return 0.5 * _jnp.sum(_jnp.mean(err, axis=-1)) if err.ndim else 0.5 * err


def _adamw(w, g, m, v):
    m = ADAM_B1 * m + (1.0 - ADAM_B1) * g
    v = ADAM_B2 * v + (1.0 - ADAM_B2) * _jnp.square(g)
    m_hat = m / (1.0 - ADAM_B1 ** ADAM_STEP)
    v_hat = v / (1.0 - ADAM_B2 ** ADAM_STEP)
    delta = -ADAM_LR * (m_hat / (_jnp.sqrt(v_hat) + ADAM_EPS) + ADAM_WD * w)
    return delta, m, v


def reference(x, c, ctx, c_ctx, w_ada, b_ada, w_in, ret_decay_logit, dn_conv_w, dn_a_log, dn_dt_bias, dn_norm_w, att_qn_w, att_kn_w, w_o, ln1_w, ln1_b, w_ffn_in, w_ffn_out, ln2_w, ln2_b, loss_target, m_c_ctx, m_w_ada, m_b_ada, m_w_in, m_ret_decay_logit, m_dn_conv_w, m_dn_a_log, m_dn_dt_bias, m_dn_norm_w, m_att_qn_w, m_att_kn_w, m_w_o, m_ln1_w, m_ln1_b, m_w_ffn_in, m_w_ffn_out, m_ln2_w, m_ln2_b, v_c_ctx, v_w_ada, v_b_ada, v_w_in, v_ret_decay_logit, v_dn_conv_w, v_dn_a_log, v_dn_dt_bias, v_dn_norm_w, v_att_qn_w, v_att_kn_w, v_w_o, v_ln1_w, v_ln1_b, v_w_ffn_in, v_w_ffn_out, v_ln2_w, v_ln2_b):
    given = dict(x=x, c=c, ctx=ctx, c_ctx=c_ctx, w_ada=w_ada, b_ada=b_ada, w_in=w_in, ret_decay_logit=ret_decay_logit, dn_conv_w=dn_conv_w, dn_a_log=dn_a_log, dn_dt_bias=dn_dt_bias, dn_norm_w=dn_norm_w, att_qn_w=att_qn_w, att_kn_w=att_kn_w, w_o=w_o, ln1_w=ln1_w, ln1_b=ln1_b, w_ffn_in=w_ffn_in, w_ffn_out=w_ffn_out, ln2_w=ln2_w, ln2_b=ln2_b, loss_target=loss_target, m_c_ctx=m_c_ctx, m_w_ada=m_w_ada, m_b_ada=m_b_ada, m_w_in=m_w_in, m_ret_decay_logit=m_ret_decay_logit, m_dn_conv_w=m_dn_conv_w, m_dn_a_log=m_dn_a_log, m_dn_dt_bias=m_dn_dt_bias, m_dn_norm_w=m_dn_norm_w, m_att_qn_w=m_att_qn_w, m_att_kn_w=m_att_kn_w, m_w_o=m_w_o, m_ln1_w=m_ln1_w, m_ln1_b=m_ln1_b, m_w_ffn_in=m_w_ffn_in, m_w_ffn_out=m_w_ffn_out, m_ln2_w=m_ln2_w, m_ln2_b=m_ln2_b, v_c_ctx=v_c_ctx, v_w_ada=v_w_ada, v_b_ada=v_b_ada, v_w_in=v_w_in, v_ret_decay_logit=v_ret_decay_logit, v_dn_conv_w=v_dn_conv_w, v_dn_a_log=v_dn_a_log, v_dn_dt_bias=v_dn_dt_bias, v_dn_norm_w=v_dn_norm_w, v_att_qn_w=v_att_qn_w, v_att_kn_w=v_att_kn_w, v_w_o=v_w_o, v_ln1_w=v_ln1_w, v_ln1_b=v_ln1_b, v_w_ffn_in=v_w_ffn_in, v_w_ffn_out=v_w_ffn_out, v_ln2_w=v_ln2_w, v_ln2_b=v_ln2_b)
    weights = {n: given[n] for n in TWIN_WEIGHTS}
    shared = {n: given[n] for n in SHARED_INPUTS}
    per_example = {n: given[n] for n in ['x', 'c', 'ctx']}
    grad_fn = _jax.value_and_grad(_loss, argnums=(0, 1))

    def one_microbatch(ex, loss_target):
        ex = dict(ex)
        diff = ex.pop(TWIN_DIFF_INPUT)
        return grad_fn(weights, diff, {**shared, **ex}, loss_target)

    if N_MICROBATCH == 1:
        loss, (grad_w, grad_x) = one_microbatch(per_example, given["loss_target"])
    else:
        def body(carry, xs):
            loss_sum, grad_sum = carry
            l_k, (gw_k, gx_k) = one_microbatch(xs[0], xs[1])
            with _jax.named_scope("update"):
                return (loss_sum + l_k, _jax.tree.map(_jnp.add, grad_sum, gw_k)), gx_k

        init = (_jnp.zeros((), _jnp.float32), _jax.tree.map(_jnp.zeros_like, weights))
        (loss, grad_w), grad_x = _jax.lax.scan(body, init, (per_example, given["loss_target"]))
    with _jax.named_scope("update"):
        delta_w, new_m, new_v = {}, {}, {}
        for n in TWIN_WEIGHTS:
            delta_w[n], new_m[n], new_v[n] = _adamw(weights[n], grad_w[n], given["m_" + n], given["v_" + n])
    return (loss, grad_x, *[grad_w[n] for n in TWIN_WEIGHTS], *[delta_w[n] for n in TWIN_WEIGHTS],
            *[new_m[n] for n in TWIN_WEIGHTS], *[new_v[n] for n in TWIN_WEIGHTS])
```

```python
import functools

import jax
import jax.numpy as jnp
from jax import lax
from jax.experimental import pallas as pl
from jax.experimental.pallas import tpu as pltpu

F32 = jnp.float32
MXU_DTYPE = jnp.bfloat16

D_MODEL = 2048
DEPTH = 4
GRID_W = 64
HEAD_DIM = 128
RET_HEADS = 4
DN_HEADS = 4
ATT_HEADS = 8
ATT_KV_HEADS = 2
RET_CHUNK = 128
DN_CHUNK = 64
DN_CONV_K = 5
ROPE_THETA = 10000.0
D_FF = 5632
PROJ_W = 5648
PROJ_PAD = 5760
DEEPNORM_ALPHA = (2 * DEPTH) ** 0.25
EPS = 1e-6
QK_SCALE = HEAD_DIM ** -0.5
ADAM_LR = 0.001
ADAM_B1 = 0.9
ADAM_B2 = 0.999
ADAM_EPS = 1e-08
ADAM_WD = 0.01
ADAM_STEP = 10
LANES = 128
MESH_ID = pl.DeviceIdType.MESH
NEG_BIG = -1e30


def _pick(n, prefs):
    for p in prefs:
        if n % p == 0:
            return p
    return n


def _contract(a, b, ca, cb, hp):
    dims = (((ca,), (cb,)), ((), ()))
    if hp:
        return lax.dot_general(a, b, dims, precision=lax.Precision.HIGHEST, preferred_element_type=F32)
    return lax.dot_general(a.astype(MXU_DTYPE), b.astype(MXU_DTYPE), dims, preferred_element_type=F32)


@functools.partial(jax.custom_vjp, nondiff_argnums=(2, 3, 4))
def mdot(a, b, ca, cb, hp=False):
    return _contract(a, b, ca, cb, hp)


def _mdot_fwd(a, b, ca, cb, hp):
    return _contract(a, b, ca, cb, hp), (a, b)


def _mdot_bwd(ca, cb, hp, res, g):
    a, b = res
    da = mdot(g, b, 1, 1 - cb, hp) if ca == 1 else mdot(b, g, 1 - cb, 1, hp)
    db = mdot(a, g, 1 - ca, 0, hp) if cb == 0 else mdot(g, a, 0, 1 - ca, hp)
    return da, db


mdot.defvjp(_mdot_fwd, _mdot_bwd)


@jax.custom_vjp
def swap_halves(x):
    return pltpu.roll(x, HEAD_DIM // 2, 1)


def _swap_fwd(x):
    return swap_halves(x), None


def _swap_bwd(_, g):
    return (swap_halves(g),)


swap_halves.defvjp(_swap_fwd, _swap_bwd)


@jax.custom_vjp
def tri_inv(a):
    n = a.shape[0]
    r = lax.broadcasted_iota(jnp.int32, (n, n), 0)
    c = lax.broadcasted_iota(jnp.int32, (n, n), 1)
    eye = (r == c).astype(F32)
    p = -a
    t = eye + p
    k = 2
    while k < n:
        p = _contract(p, p, 1, 0, True)
        t = t + _contract(t, p, 1, 0, True)
        k *= 2
    return t


def _tri_inv_fwd(a):
    t = tri_inv(a)
    return t, t


def _tri_inv_bwd(t, g):
    return (-_contract(_contract(t, g, 0, 0, True), t, 1, 1, True),)


tri_inv.defvjp(_tri_inv_fwd, _tri_inv_bwd)


def _sigmoid(x):
    return 1.0 / (1.0 + jnp.exp(-x))


def _silu(x):
    return x * _sigmoid(x)


def _softplus(x):
    return jnp.maximum(x, 0.0) + jnp.log(1.0 + jnp.exp(-jnp.abs(x)))


def _lane_pick(slab, idx):
    lane = lax.broadcasted_iota(jnp.int32, slab.shape, 1)
    return jnp.sum(jnp.where(lane == idx, slab, 0.0), axis=1, keepdims=True)


def _heads(x):
    return [x[:, h * HEAD_DIM:(h + 1) * HEAD_DIM] for h in range(x.shape[1] // HEAD_DIM)]


def _rope(x, cos, sin):
    return x * cos + swap_halves(x) * sin


def _rms(x):
    return x * lax.rsqrt(jnp.mean(x * x, -1, keepdims=True) + EPS)


def _l2n(x):
    return x * lax.rsqrt(jnp.sum(x * x, -1, keepdims=True) + EPS)


def mm(a, b, mode, out_dtype=F32, name="mm"):
    if mode == "nn":
        (m, k), (k2, n) = a.shape, b.shape
        ca, cb = 1, 0
    elif mode == "nt":
        (m, k), (n, k2) = a.shape, b.shape
        ca, cb = 1, 1
    else:
        (k, m), (k2, n) = a.shape, b.shape
        ca, cb = 0, 0
    assert k == k2, (a.shape, b.shape, mode)
    tm = _pick(m, (768, 512, 384, 256, 128))
    tn = _pick(n, (1024, 768, 640, 512, 384, 256, 128))
    tk = _pick(k, (1024, 768, 640, 512, 384, 256, 128))
    nk = k // tk
    a_spec = pl.BlockSpec((tm, tk), lambda i, j, q: (i, q)) if ca == 1 else pl.BlockSpec((tk, tm), lambda i, j, q: (q, i))
    b_spec = pl.BlockSpec((tk, tn), lambda i, j, q: (q, j)) if cb == 0 else pl.BlockSpec((tn, tk), lambda i, j, q: (j, q))

    def body(a_ref, b_ref, o_ref, acc):
        q = pl.program_id(2)

        @pl.when(q == 0)
        def _():
            acc[...] = jnp.zeros_like(acc)

        acc[...] += _contract(a_ref[...], b_ref[...], ca, cb, False)

        @pl.when(q == nk - 1)
        def _():
            o_ref[...] = acc[...].astype(o_ref.dtype)

    return pl.pallas_call(
        body, name=name, grid=(m // tm, n // tn, nk),
        in_specs=[a_spec, b_spec], out_specs=pl.BlockSpec((tm, tn), lambda i, j, q: (i, j)),
        out_shape=jax.ShapeDtypeStruct((m, n), out_dtype),
        scratch_shapes=[pltpu.VMEM((tm, tn), F32)],
        compiler_params=pltpu.CompilerParams(dimension_semantics=("parallel", "parallel", "arbitrary")),
    )(a, b)


def _row_spec(r, tm):
    if isinstance(r, tuple):
        arr, width, blk = r
        return arr, pl.BlockSpec((tm, width), lambda i, blk=blk: (i, blk))
    return r, pl.BlockSpec((tm, r.shape[1]), lambda i: (i, 0))


def _seg_spec(s, nct):
    return pl.BlockSpec((1, 1, s.shape[2]), lambda i: (jnp.where(i < nct, 0, 1), 0, 0))


def _full_spec(s):
    return pl.BlockSpec(s.shape, lambda i: (0,) * s.ndim)


def rowwise(name, fn, rows, segs, shared, outs, tm, nct):
    arrs, specs = zip(*[_row_spec(r, tm) for r in rows])
    t = arrs[0].shape[0]
    nr, ns = len(rows), len(segs)

    def body(*refs):
        vals = [r[...] for r in refs[:nr]] + [r[0] for r in refs[nr:nr + ns]] + [r[...] for r in refs[nr + ns:nr + ns + len(shared)]]
        res = fn(*vals)
        for o_ref, v in zip(refs[nr + ns + len(shared):], res):
            o_ref[...] = v.astype(o_ref.dtype)

    res = pl.pallas_call(
        body, name=name, grid=(t // tm,),
        in_specs=list(specs) + [_seg_spec(s, nct) for s in segs] + [_full_spec(s) for s in shared],
        out_specs=[pl.BlockSpec((tm, w), lambda i: (i, 0)) for w, _ in outs],
        out_shape=[jax.ShapeDtypeStruct((t, w), d) for w, d in outs],
        compiler_params=pltpu.CompilerParams(dimension_semantics=("parallel",)),
    )(*arrs, *segs, *shared)
    return res


def rowwise_bwd(name, fn, rows, segs, shared, cts, want, tm, nct):
    arrs, specs = zip(*[_row_spec(r, tm) for r in rows])
    t = arrs[0].shape[0]
    nr, ns, nsh, nc = len(rows), len(segs), len(shared), len(cts)
    widths = [sp.block_shape[1] for sp in specs]
    wanted = [i for i in range(nr) if want[i]]

    def body(*refs):
        i = pl.program_id(0)
        ins = refs[:nr + ns + nsh]
        ct_refs = refs[nr + ns + nsh:nr + ns + nsh + nc]
        o_refs = refs[nr + ns + nsh + nc:]
        vals = [r[...] for r in ins[:nr]] + [r[0] for r in ins[nr:nr + ns]] + [r[...] for r in ins[nr + ns:]]
        _, vjp = jax.vjp(fn, *vals)
        grads = vjp(tuple(c[...] for c in ct_refs))
        for o_ref, idx in zip(o_refs[:len(wanted)], wanted):
            o_ref[...] = grads[idx].astype(o_ref.dtype)
        first = jnp.logical_or(i == 0, i == nct)
        for o_ref, g in zip(o_refs[len(wanted):], grads[nr:]):
            @pl.when(first)
            def _(o_ref=o_ref):
                o_ref[...] = jnp.zeros_like(o_ref)

            o_ref[0] += g

    seg_idx = lambda i: jnp.where(i < nct, 0, 1)
    out_specs = [pl.BlockSpec((tm, widths[idx]), lambda i: (i, 0)) for idx in wanted]
    out_shape = [jax.ShapeDtypeStruct((t, widths[idx]), F32) for idx in wanted]
    for s in segs:
        out_specs.append(pl.BlockSpec((1, 1, s.shape[2]), lambda i: (seg_idx(i), 0, 0)))
        out_shape.append(jax.ShapeDtypeStruct((2, 1, s.shape[2]), F32))
    for s in shared:
        out_specs.append(pl.BlockSpec((1,) + s.shape, lambda i, nd=s.ndim: (seg_idx(i),) + (0,) * nd))
        out_shape.append(jax.ShapeDtypeStruct((2,) + s.shape, F32))
    res = pl.pallas_call(
        body, name=name, grid=(t // tm,),
        in_specs=list(specs) + [_seg_spec(s, nct) for s in segs] + [_full_spec(s) for s in shared]
        + [pl.BlockSpec((tm, c.shape[1]), lambda i: (i, 0)) for c in cts],
        out_specs=out_specs, out_shape=out_shape,
        compiler_params=pltpu.CompilerParams(dimension_semantics=("arbitrary",)),
    )(*arrs, *segs, *shared, *cts)
    nw = len(wanted)
    return res[:nw], res[nw:nw + ns], res[nw + ns:]


def modulate_fn(x, shift, scale):
    return (x * (1.0 + scale) + shift,)


def postnorm_fn(x, a, gate, w, b):
    z = DEEPNORM_ALPHA * x + gate * a
    mu = jnp.mean(z, -1, keepdims=True)
    zc = z - mu
    var = jnp.mean(zc * zc, -1, keepdims=True)
    return (zc * lax.rsqrt(var + EPS) * w + b,)


def ret_prep_fn(q, k, cos, sin):
    qs = [_rope(x, cos, sin) for x in _heads(q)]
    ks = [_rope(x, cos, sin) * QK_SCALE for x in _heads(k)]
    return jnp.concatenate(qs, 1), jnp.concatenate(ks, 1)


def ret_out_fn(of, ob, gate):
    ys = [_rms(o) * _silu(g) for o, g in zip(_heads(of + ob), _heads(gate))]
    return (jnp.concatenate(ys, 1),)


def dn_prep_fn(q, k, ab, a_log, dt_b):
    qs = [_l2n(x) * QK_SCALE for x in _heads(q)]
    ks = [_l2n(x) for x in _heads(k)]
    lane = lax.broadcasted_iota(jnp.int32, ab.shape, 1)
    g = -jnp.exp(a_log) * _softplus(ab + dt_b)
    beta = _sigmoid(ab)
    gb = jnp.where(lane < 2 * DN_HEADS, g, jnp.where(lane < 4 * DN_HEADS, beta, 0.0))
    return jnp.concatenate(qs, 1), jnp.concatenate(ks, 1), gb


def dn_out_fn(of, ob, z, w):
    ys = [_rms(o) * w * _silu(g) for o, g in zip(_heads(of + ob), _heads(z))]
    return (jnp.concatenate(ys, 1),)


def att_prep_fn(q, k, cos, sin, qw, kw):
    qs = [_rope(_rms(x) * qw, cos, sin) for x in _heads(q)]
    ks = [_rope(_rms(x) * kw, cos, sin) for x in _heads(k)]
    return jnp.concatenate(qs, 1), jnp.concatenate(ks, 1)


def swiglu_fwd(u, tm, name):
    t, w2 = u.shape
    w = w2 // 2
    cw = _pick(w, (512, 256, 128))
    ncb = w // cw

    def body(g_ref, u_ref, o_ref):
        o_ref[...] = (_silu(g_ref[...]) * u_ref[...]).astype(o_ref.dtype)

    return pl.pallas_call(
        body, name=name, grid=(t // tm, ncb),
        in_specs=[pl.BlockSpec((tm, cw), lambda i, j: (i, j)), pl.BlockSpec((tm, cw), lambda i, j: (i, j + ncb))],
        out_specs=pl.BlockSpec((tm, cw), lambda i, j: (i, j)),
        out_shape=jax.ShapeDtypeStruct((t, w), MXU_DTYPE),
        compiler_params=pltpu.CompilerParams(dimension_semantics=("parallel", "parallel")),
    )(u, u)


def swiglu_bwd(u, dact, tm, name):
    t, w2 = u.shape
    w = w2 // 2
    cw = _pick(w, (512, 256, 128))
    ncb = w // cw

    def body(g_ref, u_ref, d_ref, o_ref):
        j = pl.program_id(1)
        g = g_ref[...]
        s = _sigmoid(g)
        d = d_ref[...]
        dg = d * u_ref[...] * (s * (1.0 + g * (1.0 - s)))
        du = d * g * s
        o_ref[...] = jnp.where(j < ncb, dg, du).astype(o_ref.dtype)

    return pl.pallas_call(
        body, name=name, grid=(t // tm, 2 * ncb),
        in_specs=[pl.BlockSpec((tm, cw), lambda i, j: (i, j % ncb)), pl.BlockSpec((tm, cw), lambda i, j: (i, j % ncb + ncb)),
                  pl.BlockSpec((tm, cw), lambda i, j: (i, j % ncb))],
        out_specs=pl.BlockSpec((tm, cw), lambda i, j: (i, j)),
        out_shape=jax.ShapeDtypeStruct((t, w2), MXU_DTYPE),
        compiler_params=pltpu.CompilerParams(dimension_semantics=("parallel", "parallel")),
    )(u, u, dact)


CONV_HALO = 8


def _conv_mask(t, lc, s):
    r = lax.broadcasted_iota(jnp.int32, (t, 1), 0)
    src = r + s
    return jnp.logical_and(jnp.logical_and(src >= 0, src < t), (r < lc) == (src < lc))


def _conv_taps(pad_ref, w_ref, t, lc, flip):
    acc = None
    for k in range(DN_CONV_K):
        s = k - DN_CONV_K // 2
        off = -s if flip else s
        tap = pad_ref[pl.ds(CONV_HALO + off, t), :]
        term = jnp.where(_conv_mask(t, lc, off), tap, 0.0) * w_ref[k:k + 1, :]
        acc = term if acc is None else acc + term
    return acc


def _fill_pad(pad_ref, val, t):
    pad_ref[pl.ds(0, CONV_HALO), :] = jnp.zeros((CONV_HALO, LANES), F32)
    pad_ref[pl.ds(CONV_HALO + t, CONV_HALO), :] = jnp.zeros((CONV_HALO, LANES), F32)
    pad_ref[pl.ds(CONV_HALO, t), :] = val


def conv_fwd(p, col0, width, w, lc, name):
    t = p.shape[0]
    b0 = col0 // LANES

    def body(x_ref, w_ref, o_ref, pad):
        _fill_pad(pad, x_ref[...], t)
        o_ref[...] = _silu(_conv_taps(pad, w_ref, t, lc, False))

    return pl.pallas_call(
        body, name=name, grid=(width // LANES,),
        in_specs=[pl.BlockSpec((t, LANES), lambda j: (0, j + b0)), pl.BlockSpec((DN_CONV_K, LANES), lambda j: (0, j))],
        out_specs=pl.BlockSpec((t, LANES), lambda j: (0, j)),
        out_shape=jax.ShapeDtypeStruct((t, width), F32),
        scratch_shapes=[pltpu.VMEM((t + 2 * CONV_HALO, LANES), F32)],
        compiler_params=pltpu.CompilerParams(dimension_semantics=("parallel",)),
    )(p, w)


def conv_bwd(p, col0, width, w, dout, lc, name):
    t = p.shape[0]
    b0 = col0 // LANES

    def body(x_ref, w_ref, d_ref, dx_ref, dw_ref, pad):
        _fill_pad(pad, x_ref[...], t)
        y = _conv_taps(pad, w_ref, t, lc, False)
        sg = _sigmoid(y)
        dy = d_ref[...] * (sg * (1.0 + y * (1.0 - sg)))
        krow = lax.broadcasted_iota(jnp.int32, (8, LANES), 0)
        dw = jnp.zeros((8, LANES), F32)
        for k in range(DN_CONV_K):
            s = k - DN_CONV_K // 2
            tap = pad[pl.ds(CONV_HALO + s, t), :]
            dw_k = jnp.sum(jnp.where(_conv_mask(t, lc, s), tap, 0.0) * dy, axis=0, keepdims=True)
            dw = dw + jnp.where(krow == k, dw_k, 0.0)
        dw_ref[...] = dw
        _fill_pad(pad, dy, t)
        dx_ref[...] = _conv_taps(pad, w_ref, t, lc, True)

    return pl.pallas_call(
        body, name=name, grid=(width // LANES,),
        in_specs=[pl.BlockSpec((t, LANES), lambda j: (0, j + b0)), pl.BlockSpec((DN_CONV_K, LANES), lambda j: (0, j)),
                  pl.BlockSpec((t, LANES), lambda j: (0, j))],
        out_specs=[pl.BlockSpec((t, LANES), lambda j: (0, j)), pl.BlockSpec((8, LANES), lambda j: (0, j))],
        out_shape=[jax.ShapeDtypeStruct((t, width), F32), jax.ShapeDtypeStruct((8, width), F32)],
        scratch_shapes=[pltpu.VMEM((t + 2 * CONV_HALO, LANES), F32)],
        compiler_params=pltpu.CompilerParams(dimension_semantics=("parallel",)),
    )(p, w, dout)


def ret_chunk(q, k, v, s, logit_slab, h):
    c = q.shape[0]
    lg = -_softplus(-_lane_pick(logit_slab, h))
    i = lax.broadcasted_iota(jnp.int32, (c, c), 0)
    j = lax.broadcasted_iota(jnp.int32, (c, c), 1)
    rel = (i - j).astype(F32)
    decay = jnp.where(i >= j, jnp.exp(jnp.maximum(rel, 0.0) * lg), 0.0)
    pos = lax.broadcasted_iota(jnp.int32, (c, 1), 0).astype(F32)
    q_decay = jnp.exp((pos + 1.0) * lg)
    k_decay = jnp.exp((c - 1.0 - pos) * lg)
    intra = mdot(q, k, 1, 1) * decay
    o = mdot(intra, v, 1, 0) + mdot(q * q_decay, s, 1, 0)
    s_new = s * jnp.exp(c * lg) + mdot(k * k_decay, v, 0, 0)
    return o, s_new


def dn_chunk(q, k, v, s, gb, h):
    c = q.shape[0]
    g = _lane_pick(gb, h[0])
    beta = _lane_pick(gb, h[1])
    i = lax.broadcasted_iota(jnp.int32, (c, c), 0)
    j = lax.broadcasted_iota(jnp.int32, (c, c), 1)
    tri = i >= j
    gc = _lane_pick(mdot(tri.astype(F32), gb, 1, 0, True), h[0])
    gc_row = jnp.sum(jnp.where(i == j, gc, 0.0), axis=0, keepdims=True)
    decay = jnp.where(tri, jnp.exp(jnp.where(tri, gc - gc_row, 0.0)), 0.0)
    kb = k * beta
    vb = v * beta
    a = jnp.where(i > j, mdot(kb, k, 1, 1) * decay, 0.0)
    t = tri_inv(a)
    e = jnp.exp(gc)
    g_last = jnp.sum(g, axis=0, keepdims=True)
    w_val = mdot(t, vb, 1, 0)
    k_cum = mdot(t, kb * e, 1, 0)
    qk = mdot(q, k, 1, 1) * decay
    v_new = w_val - mdot(k_cum, s, 1, 0)
    o = mdot(q * e, s, 1, 0) + mdot(qk, v_new, 1, 0)
    s_new = s * jnp.exp(g_last) + mdot(k * jnp.exp(g_last - gc), v_new, 0, 0)
    return o, s_new


def scan_fwd(chunk_fn, c, q, k, v, slab, hsel, name):
    t, wid = q.shape
    n, nheads = t // c, wid // HEAD_DIM
    per_token = slab.shape[0] != 1
    row = pl.BlockSpec((c, wid), lambda i: (i, 0))
    slab_spec = pl.BlockSpec((c, LANES), lambda i: (i, 0)) if per_token else _full_spec(slab)
    st_spec = pl.BlockSpec((1, nheads, HEAD_DIM, HEAD_DIM), lambda i: (i, 0, 0, 0))

    def body(q_ref, k_ref, v_ref, slab_ref, o_ref, st_ref, s_scr):
        @pl.when(pl.program_id(0) == 0)
        def _():
            s_scr[...] = jnp.zeros_like(s_scr)

        sb = slab_ref[...]
        for h in range(nheads):
            sl = slice(h * HEAD_DIM, (h + 1) * HEAD_DIM)
            s_h = s_scr[h]
            st_ref[0, h] = s_h
            o, s_new = chunk_fn(q_ref[:, sl], k_ref[:, sl], v_ref[:, sl], s_h, sb, hsel(h))
            o_ref[:, sl] = o
            s_scr[h] = s_new

    return pl.pallas_call(
        body, name=name, grid=(n,),
        in_specs=[row, row, row, slab_spec], out_specs=[row, st_spec],
        out_shape=[jax.ShapeDtypeStruct((t, wid), F32), jax.ShapeDtypeStruct((n, nheads, HEAD_DIM, HEAD_DIM), F32)],
        scratch_shapes=[pltpu.VMEM((nheads, HEAD_DIM, HEAD_DIM), F32)],
        compiler_params=pltpu.CompilerParams(dimension_semantics=("arbitrary",)),
    )(q, k, v, slab)


def scan_bwd(chunk_fn, c, q, k, v, slab, states, do, hsel, name):
    t, wid = q.shape
    n, nheads = t // c, wid // HEAD_DIM
    per_token = slab.shape[0] != 1
    row = pl.BlockSpec((c, wid), lambda i: (n - 1 - i, 0))
    slab_spec = pl.BlockSpec((c, LANES), lambda i: (n - 1 - i, 0)) if per_token else _full_spec(slab)
    st_spec = pl.BlockSpec((1, nheads, HEAD_DIM, HEAD_DIM), lambda i: (n - 1 - i, 0, 0, 0))

    def body(q_ref, k_ref, v_ref, slab_ref, st_ref, do_ref, dq_ref, dk_ref, dv_ref, dslab_ref, ds_scr):
        @pl.when(pl.program_id(0) == 0)
        def _():
            ds_scr[...] = jnp.zeros_like(ds_scr)
            if not per_token:
                dslab_ref[...] = jnp.zeros_like(dslab_ref)

        sb = slab_ref[...]
        dslab = None
        for h in range(nheads):
            sl = slice(h * HEAD_DIM, (h + 1) * HEAD_DIM)
            _, vjp = jax.vjp(lambda a, b, cc, s, z, h=h: chunk_fn(a, b, cc, s, z, hsel(h)),
                             q_ref[:, sl], k_ref[:, sl], v_ref[:, sl], st_ref[0, h], sb)
            dq, dk, dv, ds, dz = vjp((do_ref[:, sl], ds_scr[h]))
            dq_ref[:, sl] = dq
            dk_ref[:, sl] = dk
            dv_ref[:, sl] = dv
            ds_scr[h] = ds
            dslab = dz if dslab is None else dslab + dz
        if per_token:
            dslab_ref[...] = dslab
        else:
            dslab_ref[...] += dslab

    return pl.pallas_call(
        body, name=name, grid=(n,),
        in_specs=[row, row, row, slab_spec, st_spec, row], out_specs=[row, row, row, slab_spec],
        out_shape=[jax.ShapeDtypeStruct((t, wid), F32)] * 3 + [jax.ShapeDtypeStruct(slab.shape, F32)],
        scratch_shapes=[pltpu.VMEM((nheads, HEAD_DIM, HEAD_DIM), F32)],
        compiler_params=pltpu.CompilerParams(dimension_semantics=("arbitrary",)),
    )(q, k, v, slab, states, do)


def _att_tiles(t, lc):
    return _pick(lc, (256, 128)), _pick(t, (768, 384, 256, 128))


def _att_probs(q, k, lse, is_ctx, j, tk, lc):
    s = _contract(q, k, 1, 1, False) * QK_SCALE
    kidx = j * tk + lax.broadcasted_iota(jnp.int32, (1, tk), 1)
    masked = jnp.logical_and(is_ctx, kidx >= lc)
    return s, masked, (None if lse is None else jnp.where(masked, 0.0, jnp.exp(s - lse)))


def attn_fwd(qn, kn, v, lc, name):
    t = qn.shape[0]
    nh, nkv = qn.shape[1] // HEAD_DIM, kn.shape[1] // HEAD_DIM
    grp = nh // nkv
    tq, tk = _att_tiles(t, lc)
    nq, nk, nctq = t // tq, t // tk, lc // tq

    def body(q_ref, k_ref, v_ref, o_ref, lse_ref, m_scr, l_scr, acc):
        i, j = pl.program_id(1), pl.program_id(2)

        @pl.when(j == 0)
        def _():
            m_scr[...] = jnp.full_like(m_scr, NEG_BIG)
            l_scr[...] = jnp.zeros_like(l_scr)
            acc[...] = jnp.zeros_like(acc)

        is_ctx = i < nctq

        @pl.when(jnp.logical_or(jnp.logical_not(is_ctx), j * tk < lc))
        def _():
            s, masked, _ = _att_probs(q_ref[...], k_ref[...], None, is_ctx, j, tk, lc)
            s = jnp.where(masked, NEG_BIG, s)
            m_old = m_scr[...]
            m_new = jnp.maximum(m_old, jnp.max(s, axis=1, keepdims=True))
            alpha = jnp.exp(m_old - m_new)
            p = jnp.where(masked, 0.0, jnp.exp(s - m_new))
            l_scr[...] = alpha * l_scr[...] + jnp.sum(p, axis=1, keepdims=True)
            acc[...] = alpha * acc[...] + _contract(p, v_ref[...], 1, 0, False)
            m_scr[...] = m_new

        @pl.when(j == nk - 1)
        def _():
            o_ref[...] = acc[...] / l_scr[...]
            lse_ref[0] = m_scr[...] + jnp.log(l_scr[...])

    return pl.pallas_call(
        body, name=name, grid=(nh, nq, nk),
        in_specs=[pl.BlockSpec((tq, HEAD_DIM), lambda h, i, j: (i, h)),
                  pl.BlockSpec((tk, HEAD_DIM), lambda h, i, j: (j, h // grp)),
                  pl.BlockSpec((tk, HEAD_DIM), lambda h, i, j: (j, h // grp))],
        out_specs=[pl.BlockSpec((tq, HEAD_DIM), lambda h, i, j: (i, h)),
                   pl.BlockSpec((1, tq, 1), lambda h, i, j: (h, i, 0))],
        out_shape=[jax.ShapeDtypeStruct((t, nh * HEAD_DIM), F32), jax.ShapeDtypeStruct((nh, t, 1), F32)],
        scratch_shapes=[pltpu.VMEM((tq, 1), F32), pltpu.VMEM((tq, 1), F32), pltpu.VMEM((tq, HEAD_DIM), F32)],
        compiler_params=pltpu.CompilerParams(dimension_semantics=("parallel", "parallel", "arbitrary")),
    )(qn, kn, v)


def attn_bwd_dq(qn, kn, v, o, do, lse, lc, name):
    t = qn.shape[0]
    nh, nkv = qn.shape[1] // HEAD_DIM, kn.shape[1] // HEAD_DIM
    grp = nh // nkv
    tq, tk = _att_tiles(t, lc)
    nq, nk, nctq = t // tq, t // tk, lc // tq

    def body(q_ref, k_ref, v_ref, o_ref, do_ref, lse_ref, dq_ref, dl_ref, acc, dl_scr):
        i, j = pl.program_id(1), pl.program_id(2)

        @pl.when(j == 0)
        def _():
            acc[...] = jnp.zeros_like(acc)
            dl_scr[...] = jnp.sum(do_ref[...] * o_ref[...], axis=1, keepdims=True)

        is_ctx = i < nctq

        @pl.when(jnp.logical_or(jnp.logical_not(is_ctx), j * tk < lc))
        def _():
            _, _, p = _att_probs(q_ref[...], k_ref[...], lse_ref[0], is_ctx, j, tk, lc)
            dp = _contract(do_ref[...], v_ref[...], 1, 1, False)
            ds = p * (dp - dl_scr[...]) * QK_SCALE
            acc[...] += _contract(ds, k_ref[...], 1, 0, False)

        @pl.when(j == nk - 1)
        def _():
            dq_ref[...] = acc[...]
            dl_ref[0] = dl_scr[...]

    qspec = pl.BlockSpec((tq, HEAD_DIM), lambda h, i, j: (i, h))
    kspec = pl.BlockSpec((tk, HEAD_DIM), lambda h, i, j: (j, h // grp))
    rspec = pl.BlockSpec((1, tq, 1), lambda h, i, j: (h, i, 0))
    return pl.pallas_call(
        body, name=name, grid=(nh, nq, nk),
        in_specs=[qspec, kspec, kspec, qspec, qspec, rspec],
        out_specs=[qspec, rspec],
        out_shape=[jax.ShapeDtypeStruct((t, nh * HEAD_DIM), F32), jax.ShapeDtypeStruct((nh, t, 1), F32)],
        scratch_shapes=[pltpu.VMEM((tq, HEAD_DIM), F32), pltpu.VMEM((tq, 1), F32)],
        compiler_params=pltpu.CompilerParams(dimension_semantics=("parallel", "parallel", "arbitrary")),
    )(qn, kn, v, o, do, lse)


def attn_bwd_dkv(qn, kn, v, do, lse, delta, lc, name):
    t = qn.shape[0]
    nh, nkv = qn.shape[1] // HEAD_DIM, kn.shape[1] // HEAD_DIM
    grp = nh // nkv
    tq, tk = _att_tiles(t, lc)
    nq, nk, nctq = t // tq, t // tk, lc // tq
    nr = grp * nq

    def body(q_ref, k_ref, v_ref, do_ref, lse_ref, dl_ref, dk_ref, dv_ref, dk_acc, dv_acc):
        j, r = pl.program_id(1), pl.program_id(2)

        @pl.when(r == 0)
        def _():
            dk_acc[...] = jnp.zeros_like(dk_acc)
            dv_acc[...] = jnp.zeros_like(dv_acc)

        is_ctx = (r % nq) < nctq

        @pl.when(jnp.logical_or(jnp.logical_not(is_ctx), j * tk < lc))
        def _():
            _, _, p = _att_probs(q_ref[...], k_ref[...], lse_ref[0], is_ctx, j, tk, lc)
            dv_acc[...] += _contract(p, do_ref[...], 0, 0, False)
            dp = _contract(do_ref[...], v_ref[...], 1, 1, False)
            ds = p * (dp - dl_ref[0]) * QK_SCALE
            dk_acc[...] += _contract(ds, q_ref[...], 0, 0, False)

        @pl.when(r == nr - 1)
        def _():
            dk_ref[...] = dk_acc[...]
            dv_ref[...] = dv_acc[...]

    qspec = pl.BlockSpec((tq, HEAD_DIM), lambda g, j, r: (r % nq, g * grp + r // nq))
    kspec = pl.BlockSpec((tk, HEAD_DIM), lambda g, j, r: (j, g))
    rspec = pl.BlockSpec((1, tq, 1), lambda g, j, r: (g * grp + r // nq, r % nq, 0))
    return pl.pallas_call(
        body, name=name, grid=(nkv, nk, nr),
        in_specs=[qspec, kspec, kspec, qspec, rspec, rspec],
        out_specs=[kspec, kspec],
        out_shape=[jax.ShapeDtypeStruct((t, nkv * HEAD_DIM), F32), jax.ShapeDtypeStruct((t, nkv * HEAD_DIM), F32)],
        scratch_shapes=[pltpu.VMEM((tk, HEAD_DIM), F32), pltpu.VMEM((tk, HEAD_DIM), F32)],
        compiler_params=pltpu.CompilerParams(dimension_semantics=("parallel", "parallel", "arbitrary")),
    )(qn, kn, v, do, lse, delta)


def loss_and_grad(y, target, tm, name):
    t, d = y.shape

    def body(y_ref, t_ref, l_ref, g_ref):
        @pl.when(pl.program_id(0) == 0)
        def _():
            l_ref[...] = jnp.zeros_like(l_ref)

        e = y_ref[...] - t_ref[...]
        g_ref[...] = e * (1.0 / d)
        l_ref[...] += 0.5 * jnp.sum(jnp.mean(e * e, axis=1, keepdims=True), axis=0, keepdims=True)

    return pl.pallas_call(
        body, name=name, grid=(t // tm,),
        in_specs=[pl.BlockSpec((tm, d), lambda i: (i, 0))] * 2,
        out_specs=[pl.BlockSpec((1, 1), lambda i: (0, 0)), pl.BlockSpec((tm, d), lambda i: (i, 0))],
        out_shape=[jax.ShapeDtypeStruct((1, 1), F32), jax.ShapeDtypeStruct((t, d), F32)],
        compiler_params=pltpu.CompilerParams(dimension_semantics=("arbitrary",)),
    )(y, target)


def _row_tile(rows, width):
    budget = max(8, (2 * 1024 * 1024) // (4 * width))
    for cand in (1024, 512, 256, 128, 64, 32, 16, 8):
        if cand <= budget and rows % cand == 0:
            return cand
    return rows


def ew_sum(arrs, name, out_dtype=F32):
    rows, width = arrs[0].shape
    tr = _row_tile(rows, width)

    def body(*refs):
        acc = refs[0][...].astype(F32)
        for r in refs[1:-1]:
            acc = acc + r[...].astype(F32)
        refs[-1][...] = acc.astype(refs[-1].dtype)

    spec = pl.BlockSpec((tr, width), lambda i: (i, 0))
    return pl.pallas_call(
        body, name=name, grid=(rows // tr,), in_specs=[spec] * len(arrs), out_specs=spec,
        out_shape=jax.ShapeDtypeStruct((rows, width), out_dtype),
        compiler_params=pltpu.CompilerParams(dimension_semantics=("parallel",)),
    )(*arrs)


def adamw(g, w, m, v, name):
    rows, width = g.shape
    tr = _row_tile(rows, width)

    def body(g_ref, w_ref, m_ref, v_ref, d_ref, mo_ref, vo_ref):
        gg = g_ref[...]
        m_new = ADAM_B1 * m_ref[...] + (1.0 - ADAM_B1) * gg
        v_new = ADAM_B2 * v_ref[...] + (1.0 - ADAM_B2) * jnp.square(gg)
        m_hat = m_new / (1.0 - ADAM_B1 ** ADAM_STEP)
        v_hat = v_new / (1.0 - ADAM_B2 ** ADAM_STEP)
        d_ref[...] = -ADAM_LR * (m_hat / (jnp.sqrt(v_hat) + ADAM_EPS) + ADAM_WD * w_ref[...])
        mo_ref[...] = m_new
        vo_ref[...] = v_new

    spec = pl.BlockSpec((tr, width), lambda i: (i, 0))
    return pl.pallas_call(
        body, name=name, grid=(rows // tr,), in_specs=[spec] * 4, out_specs=[spec] * 3,
        out_shape=[jax.ShapeDtypeStruct((rows, width), F32)] * 3,
        compiler_params=pltpu.CompilerParams(dimension_semantics=("parallel",)),
    )(g, w, m, v)


def _place():
    return lax.axis_index("x"), lax.axis_index("y"), lax.axis_index("c")


def _other_chips(x, y):
    return [(1 - x, y), (x, 1 - y), (1 - x, 1 - y)]


def allgather8(x_shard, name):
    m_per, n = x_shard.shape

    def body(x_ref, out_ref, send_sems, recv_sems, local_sem):
        x, y, c = _place()
        me, sibling = (x, y, c), (x, y, 1 - c)
        chips = _other_chips(x, y)

        def rows(px, py, pc):
            return out_ref.at[pl.ds((4 * px + 2 * py + pc) * m_per, m_per), :]

        def copy(k, block, to, src=None):
            return pltpu.make_async_remote_copy(
                src_ref=rows(*block) if src is None else src, dst_ref=rows(*block),
                send_sem=send_sems.at[k], recv_sem=recv_sems.at[k], device_id=to, device_id_type=MESH_ID)

        mine = pltpu.make_async_copy(x_ref, rows(*me), local_sem)
        mine.start()
        first = [copy(0, me, sibling, src=x_ref)]
        first += [copy(1 + j, me, (*chip, c), src=x_ref) for j, chip in enumerate(chips)]
        for cp in first:
            cp.start()
        passed = [copy(4 + j, (*chip, c), sibling) for j, chip in enumerate(chips)]
        for j, chip in enumerate(chips):
            copy(1 + j, (*chip, c), me).wait_recv()
            passed[j].start()
        copy(0, sibling, me).wait_recv()
        for j, chip in enumerate(chips):
            copy(4 + j, (*chip, 1 - c), me).wait_recv()
        for cp in first + passed:
            cp.wait_send()
        mine.wait()

    return pl.pallas_call(
        body, name=name,
        out_shape=jax.ShapeDtypeStruct((8 * m_per, n), x_shard.dtype),
        in_specs=[pl.BlockSpec(memory_space=pltpu.VMEM)],
        out_specs=pl.BlockSpec(memory_space=pltpu.VMEM),
        scratch_shapes=[pltpu.SemaphoreType.DMA((7,)), pltpu.SemaphoreType.DMA((7,)), pltpu.SemaphoreType.DMA],
    )(x_shard)


_ANY = pl.BlockSpec(memory_space=pl.ANY)


def gather_chips(shards, name):
    n = len(shards)

    def body(*refs):
        ins, outs = refs[:n], refs[n:2 * n]
        send_sems, recv_sems, local_sems = refs[2 * n:]
        x, y, c = _place()
        chips = _other_chips(x, y)
        started = []
        for a in range(n):
            loc = pltpu.make_async_copy(ins[a], outs[a].at[2 * x + y], local_sems.at[a])
            loc.start()
            started.append(loc)
        sends = []
        for a in range(n):
            for j, chip in enumerate(chips):
                cp = pltpu.make_async_remote_copy(
                    src_ref=ins[a], dst_ref=outs[a].at[2 * x + y], send_sem=send_sems.at[3 * a + j],
                    recv_sem=recv_sems.at[3 * a + j], device_id=(*chip, c), device_id_type=MESH_ID)
                cp.start()
                sends.append(cp)
        for a in range(n):
            for j, chip in enumerate(chips):
                pltpu.make_async_remote_copy(
                    src_ref=ins[a], dst_ref=outs[a].at[2 * chip[0] + chip[1]], send_sem=send_sems.at[3 * a + j],
                    recv_sem=recv_sems.at[3 * a + j], device_id=(*chip, c), device_id_type=MESH_ID).wait_recv()
        for cp in sends:
            cp.wait_send()
        for loc in started:
            loc.wait()

    return pl.pallas_call(
        body, name=name,
        out_shape=[jax.ShapeDtypeStruct((4,) + s.shape, s.dtype) for s in shards],
        in_specs=[_ANY] * n, out_specs=[_ANY] * n,
        scratch_shapes=[pltpu.SemaphoreType.DMA((3 * n,)), pltpu.SemaphoreType.DMA((3 * n,)), pltpu.SemaphoreType.DMA((n,))],
    )(*shards)


def rs_sibling(grads, name):
    n = len(grads)

    def body(*refs):
        ins, mine, got = refs[:n], refs[n:2 * n], refs[2 * n:3 * n]
        send_sems, recv_sems, local_sems = refs[3 * n:]
        x, y, c = _place()
        pend = []
        for a in range(n):
            h = ins[a].shape[1] // 2
            loc = pltpu.make_async_copy(ins[a].at[:, pl.ds(pl.multiple_of(c * h, 8), h), :], mine[a], local_sems.at[a])
            cp = pltpu.make_async_remote_copy(
                src_ref=ins[a].at[:, pl.ds(pl.multiple_of((1 - c) * h, 8), h), :], dst_ref=got[a],
                send_sem=send_sems.at[a], recv_sem=recv_sems.at[a], device_id=(x, y, 1 - c), device_id_type=MESH_ID)
            loc.start()
            cp.start()
            pend.append((loc, cp))
        for loc, cp in pend:
            cp.wait()
            loc.wait()

    half = [jax.ShapeDtypeStruct((g.shape[0], g.shape[1] // 2, g.shape[2]), g.dtype) for g in grads]
    return pl.pallas_call(
        body, name=name, out_shape=half + half, in_specs=[_ANY] * n, out_specs=[_ANY] * (2 * n),
        scratch_shapes=[pltpu.SemaphoreType.DMA((n,)), pltpu.SemaphoreType.DMA((n,)), pltpu.SemaphoreType.DMA((n,))],
    )(*grads)


def rs_chips(parts, name):
    n = len(parts)

    def body(*refs):
        ins, mine, got = refs[:n], refs[n:2 * n], refs[2 * n:3 * n]
        send_sems, recv_sems, local_sems = refs[3 * n:]
        x, y, c = _place()
        chips = _other_chips(x, y)
        pend = []
        for a in range(n):
            loc = pltpu.make_async_copy(ins[a].at[2 * x + y], mine[a], local_sems.at[a])
            loc.start()
            pend.append(loc)
            for j, chip in enumerate(chips):
                cp = pltpu.make_async_remote_copy(
                    src_ref=ins[a].at[2 * chip[0] + chip[1]], dst_ref=got[a].at[j],
                    send_sem=send_sems.at[3 * a + j], recv_sem=recv_sems.at[3 * a + j],
                    device_id=(*chip, c), device_id_type=MESH_ID)
                cp.start()
                pend.append(cp)
        for p in pend:
            p.wait()

    return pl.pallas_call(
        body, name=name,
        out_shape=[jax.ShapeDtypeStruct(p.shape[1:], p.dtype) for p in parts]
        + [jax.ShapeDtypeStruct((3,) + p.shape[1:], p.dtype) for p in parts],
        in_specs=[_ANY] * n, out_specs=[_ANY] * (2 * n),
        scratch_shapes=[pltpu.SemaphoreType.DMA((3 * n,)), pltpu.SemaphoreType.DMA((3 * n,)), pltpu.SemaphoreType.DMA((n,))],
    )(*parts)


def share_sibling(halves, name):
    n = len(halves)

    def body(*refs):
        ins, outs = refs[:n], refs[n:2 * n]
        send_sems, recv_sems, local_sems = refs[2 * n:]
        x, y, c = _place()
        pend = []
        for a in range(n):
            loc = pltpu.make_async_copy(ins[a], outs[a].at[c], local_sems.at[a])
            cp = pltpu.make_async_remote_copy(
                src_ref=ins[a], dst_ref=outs[a].at[c], send_sem=send_sems.at[a], recv_sem=recv_sems.at[a],
                device_id=(x, y, 1 - c), device_id_type=MESH_ID)
            loc.start()
            cp.start()
            pend.append((loc, cp))
        for a, (loc, cp) in enumerate(pend):
            cp.wait_send()
            pltpu.make_async_remote_copy(
                src_ref=ins[a], dst_ref=outs[a].at[1 - c], send_sem=send_sems.at[a], recv_sem=recv_sems.at[a],
                device_id=(x, y, 1 - c), device_id_type=MESH_ID).wait_recv()
            loc.wait()

    return pl.pallas_call(
        body, name=name, out_shape=[jax.ShapeDtypeStruct((2,) + h.shape, h.dtype) for h in halves],
        in_specs=[_ANY] * n, out_specs=[_ANY] * n,
        scratch_shapes=[pltpu.SemaphoreType.DMA((n,)), pltpu.SemaphoreType.DMA((n,)), pltpu.SemaphoreType.DMA((n,))],
    )(*halves)


def reduce_scatter(grads, tag):
    mine, got = _split(rs_sibling(grads, name=f"rs_sibling_{tag}"))
    pair = [ew_sum([a.reshape(-1, a.shape[2]), b.reshape(-1, b.shape[2])], name=f"rs_pair_{tag}_{i}").reshape(a.shape)
            for i, (a, b) in enumerate(zip(mine, got))]
    own, recv = _split(rs_chips(pair, name=f"rs_chips_{tag}"))
    tot = [ew_sum([a, b[0], b[1], b[2]], name=f"rs_quad_{tag}_{i}") for i, (a, b) in enumerate(zip(own, recv))]
    both = share_sibling(tot, name=f"rs_share_{tag}")
    return [b.reshape(-1, b.shape[2]) for b in both]


def _split(lst):
    n = len(lst) // 2
    return lst[:n], lst[n:]


def sum_entries(g, idxs, name):
    _, rows, width = g.shape
    tr = _row_tile(rows, width)

    def body(g_ref, o_ref):
        acc = g_ref[idxs[0]]
        for d in idxs[1:]:
            acc = acc + g_ref[d]
        o_ref[...] = acc

    return pl.pallas_call(
        body, name=name, grid=(rows // tr,),
        in_specs=[pl.BlockSpec((8, tr, width), lambda i: (0, i, 0))], out_specs=pl.BlockSpec((tr, width), lambda i: (i, 0)),
        out_shape=jax.ShapeDtypeStruct((rows, width), F32),
        compiler_params=pltpu.CompilerParams(dimension_semantics=("parallel",)),
    )(g)


def _pack(arrs, rows_multiple=8):
    parts, offs, r = [], [], 0
    for a in arrs:
        flat = a.reshape(-1).astype(F32)
        nrow = -(-flat.shape[0] // LANES)
        parts.append(jnp.pad(flat, (0, nrow * LANES - flat.shape[0])).reshape(nrow, LANES))
        offs.append((r, nrow, a.shape))
        r += nrow
    pad = (-r) % rows_multiple
    if pad:
        parts.append(jnp.zeros((pad, LANES), F32))
    return jnp.concatenate(parts, 0), offs


def _unpack(slab, offs):
    outs = []
    for r, nrow, shape in offs:
        size = 1
        for s in shape:
            size *= s
        outs.append(slab[r:r + nrow].reshape(-1)[:size].reshape(shape))
    return outs


def _seqflip(a, lc):
    return jnp.concatenate([jnp.flip(a[:lc], 0), jnp.flip(a[lc:], 0)], 0)


def _slab(vec8):
    return jnp.pad(vec8.reshape(1, -1).astype(F32), ((0, 0), (0, LANES - vec8.size)))


def _rope_tables(n_lat, lc):
    rows = n_lat // GRID_W
    row = jnp.repeat(jnp.arange(rows, dtype=F32), GRID_W)
    col = jnp.tile(jnp.arange(GRID_W, dtype=F32), rows)
    n_freq = HEAD_DIM // 4
    inv = ROPE_THETA ** (-jnp.arange(n_freq, dtype=F32) / n_freq)
    ang = jnp.concatenate([row[:, None] * inv, col[:, None] * inv], -1)
    cos, sin = jnp.cos(ang), jnp.sin(ang)
    cos_t = jnp.concatenate([jnp.ones((lc, HEAD_DIM), F32), jnp.concatenate([cos, cos], -1)], 0)
    sin_t = jnp.concatenate([jnp.zeros((lc, HEAD_DIM), F32), jnp.concatenate([-sin, sin], -1)], 0)
    return cos_t, sin_t


def _permute_w_in(w):
    return jnp.concatenate([w[:, :4096], w[:, 4112:], w[:, 4096:4112], jnp.zeros((w.shape[0], PROJ_PAD - PROJ_W), w.dtype)], 1)


def _unpermute_dw_in(dw):
    return jnp.concatenate([dw[:, :4096], dw[:, 5632:5648], dw[:, 4096:5632]], 1)


def _layer_fwd(xin, mods, wts, prm, tabs, lc, tm, i):
    nct = lc // tm
    seg = lambda j: mods[:, j:j + 1, :]
    cos_t, sin_t = tabs
    sv = {}
    (h1,) = rowwise(f"mod1_{i}", modulate_fn, [xin], [seg(0), seg(1)], [], [(D_MODEL, MXU_DTYPE)], tm, nct)
    p = mm(h1, wts["w_in"], "nn", name=f"proj_in_{i}")
    rq, rk = rowwise(f"ret_prep_{i}", ret_prep_fn, [(p, 512, 0), (p, 512, 1), cos_t, sin_t], [], [],
                     [(512, F32), (512, F32)], tm, nct)
    rv = p[:, 1024:1536]
    sel = lambda h: h
    r_of, r_stf = scan_fwd(ret_chunk, RET_CHUNK, rq, rk, rv, prm["ret_logit"][0], sel, f"ret_scan_f_{i}")
    rqb, rkb, rvb = (_seqflip(a, lc) for a in (rq, rk, rv))
    r_ob_, r_stb = scan_fwd(ret_chunk, RET_CHUNK, rqb, rkb, rvb, prm["ret_logit"][1], sel, f"ret_scan_b_{i}")
    r_ob = _seqflip(r_ob_, lc)
    (y_ret,) = rowwise(f"ret_out_{i}", ret_out_fn, [r_of, r_ob, (p, 512, 3)], [], [], [(512, F32)], tm, nct)
    qkvc = conv_fwd(p, 2048, 3 * 512, prm["conv_w"], lc, f"dn_conv_{i}")
    dq, dk, gb = rowwise(f"dn_prep_{i}", dn_prep_fn, [(qkvc, 512, 0), (qkvc, 512, 1), (p, LANES, 44)], [],
                         [prm["a_log"], prm["dt_b"]], [(512, F32), (512, F32), (LANES, F32)], tm, nct)
    dv = qkvc[:, 1024:1536]
    sel_f = lambda h: (h, 2 * DN_HEADS + h)
    sel_b = lambda h: (DN_HEADS + h, 3 * DN_HEADS + h)
    d_of, d_stf = scan_fwd(dn_chunk, DN_CHUNK, dq, dk, dv, gb, sel_f, f"dn_scan_f_{i}")
    dqb, dkb, dvb, gbb = (_seqflip(a, lc) for a in (dq, dk, dv, gb))
    d_ob_, d_stb = scan_fwd(dn_chunk, DN_CHUNK, dqb, dkb, dvb, gbb, sel_b, f"dn_scan_b_{i}")
    d_ob = _seqflip(d_ob_, lc)
    (y_dn,) = rowwise(f"dn_out_{i}", dn_out_fn, [d_of, d_ob, (p, 512, 7)], [], [prm["dn_norm_w"]], [(512, F32)], tm, nct)
    aq, ak = rowwise(f"att_prep_{i}", att_prep_fn, [(p, 1024, 4), (p, 256, 20), cos_t, sin_t], [],
                     [prm["qn_w"], prm["kn_w"]], [(1024, F32), (256, F32)], tm, nct)
    av = p[:, 5376:5632]
    ao, lse = attn_fwd(aq, ak, av, lc, f"attn_fwd_{i}")
    y = jnp.concatenate([y_ret, y_dn, ao], 1)
    a1 = mm(y, wts["w_o"], "nn", name=f"proj_out_{i}")
    (x1,) = rowwise(f"postnorm1_{i}", postnorm_fn, [xin, a1], [seg(2)], [prm["ln1_w"], prm["ln1_b"]], [(D_MODEL, F32)], tm, nct)
    (h2,) = rowwise(f"mod2_{i}", modulate_fn, [x1], [seg(3), seg(4)], [], [(D_MODEL, MXU_DTYPE)], tm, nct)
    u = mm(h2, wts["w_ffn_in"], "nn", name=f"ffn_in_{i}")
    act = swiglu_fwd(u, tm, f"swiglu_{i}")
    a2 = mm(act, wts["w_ffn_out"], "nn", name=f"ffn_out_{i}")
    (x2,) = rowwise(f"postnorm2_{i}", postnorm_fn, [x1, a2], [seg(5)], [prm["ln2_w"], prm["ln2_b"]], [(D_MODEL, F32)], tm, nct)
    sv.update(xin=xin, h1=h1, p=p, rq=rq, rk=rk, rv=rv, rqb=rqb, rkb=rkb, rvb=rvb, r_stf=r_stf, r_stb=r_stb, r_of=r_of,
              r_ob=r_ob, qkvc=qkvc, dq=dq, dk=dk, dv=dv, gb=gb, dqb=dqb, dkb=dkb, dvb=dvb, gbb=gbb, d_stf=d_stf,
              d_stb=d_stb, d_of=d_of, d_ob=d_ob, aq=aq, ak=ak, av=av, ao=ao, lse=lse, y=y, a1=a1, x1=x1, h2=h2, u=u,
              act=act, a2=a2)
    return x2, sv


def _layer_bwd(dx2, sv, mods, wts, prm, tabs, lc, tm, i):
    nct = lc // tm
    seg = lambda j: mods[:, j:j + 1, :]
    cos_t, sin_t = tabs
    p = sv["p"]
    both = lambda g: g[0] + g[1]
    (dx1a, da2), (dgate2,), (dln2w, dln2b) = rowwise_bwd(
        f"postnorm2_b_{i}", postnorm_fn, [sv["x1"], sv["a2"]], [seg(5)], [prm["ln2_w"], prm["ln2_b"]], [dx2], [True, True], tm, nct)
    dact = mm(da2, wts["w_ffn_out"], "nt", name=f"ffn_out_dx_{i}")
    dw_ffn_out = mm(sv["act"], da2, "tn", name=f"ffn_out_dw_{i}")
    du = swiglu_bwd(sv["u"], dact, tm, f"swiglu_b_{i}")
    dh2 = mm(du, wts["w_ffn_in"], "nt", name=f"ffn_in_dx_{i}")
    dw_ffn_in = mm(sv["h2"], du, "tn", name=f"ffn_in_dw_{i}")
    (dx1b,), (dshift2, dscale2), _ = rowwise_bwd(
        f"mod2_b_{i}", modulate_fn, [sv["x1"]], [seg(3), seg(4)], [], [dh2], [True], tm, nct)
    dx1 = ew_sum([dx1a, dx1b], name=f"dx1_{i}")
    (dxa, da1), (dgate1,), (dln1w, dln1b) = rowwise_bwd(
        f"postnorm1_b_{i}", postnorm_fn, [sv["xin"], sv["a1"]], [seg(2)], [prm["ln1_w"], prm["ln1_b"]], [dx1], [True, True], tm, nct)
    dy = mm(da1, wts["w_o"], "nt", name=f"proj_out_dx_{i}")
    dw_o = mm(sv["y"], da1, "tn", name=f"proj_out_dw_{i}")
    dy_ret, dy_dn, dao = dy[:, :512], dy[:, 512:1024], dy[:, 1024:]
    daq, delta = attn_bwd_dq(sv["aq"], sv["ak"], sv["av"], sv["ao"], dao, sv["lse"], lc, f"attn_dq_{i}")
    dak, dav = attn_bwd_dkv(sv["aq"], sv["ak"], sv["av"], dao, sv["lse"], delta, lc, f"attn_dkv_{i}")
    (dp_aq, dp_ak), _, (dqn_w, dkn_w) = rowwise_bwd(
        f"att_prep_b_{i}", att_prep_fn, [(p, 1024, 4), (p, 256, 20), cos_t, sin_t], [], [prm["qn_w"], prm["kn_w"]],
        [daq, dak], [True, True, False, False], tm, nct)
    (dd_o, dp_z), _, (ddn_norm_w,) = rowwise_bwd(
        f"dn_out_b_{i}", dn_out_fn, [sv["d_of"], sv["d_ob"], (p, 512, 7)], [], [prm["dn_norm_w"]], [dy_dn], [True, False, True], tm, nct)
    sel_f = lambda h: (h, 2 * DN_HEADS + h)
    sel_b = lambda h: (DN_HEADS + h, 3 * DN_HEADS + h)
    gf = scan_bwd(dn_chunk, DN_CHUNK, sv["dq"], sv["dk"], sv["dv"], sv["gb"], sv["d_stf"], dd_o, sel_f, f"dn_scan_f_b_{i}")
    gbk = scan_bwd(dn_chunk, DN_CHUNK, sv["dqb"], sv["dkb"], sv["dvb"], sv["gbb"], sv["d_stb"], _seqflip(dd_o, lc), sel_b,
                   f"dn_scan_b_b_{i}")
    ddq, ddk, ddv, dgb = (ew_sum([a, _seqflip(b, lc)], name=f"dn_dirsum_{i}_{n}") for n, (a, b) in enumerate(zip(gf, gbk)))
    (dqc, dkc, dp_ab), _, (da_log, ddt_b) = rowwise_bwd(
        f"dn_prep_b_{i}", dn_prep_fn, [(sv["qkvc"], 512, 0), (sv["qkvc"], 512, 1), (p, LANES, 44)], [],
        [prm["a_log"], prm["dt_b"]], [ddq, ddk, dgb], [True, True, True], tm, nct)
    dqkvc = jnp.concatenate([dqc, dkc, ddv], 1)
    dp_qkv, dconv_w = conv_bwd(p, 2048, 3 * 512, prm["conv_w"], dqkvc, lc, f"dn_conv_b_{i}")
    (dr_o, dp_g), _, _ = rowwise_bwd(
        f"ret_out_b_{i}", ret_out_fn, [sv["r_of"], sv["r_ob"], (p, 512, 3)], [], [], [dy_ret], [True, False, True], tm, nct)
    sel = lambda h: h
    rf = scan_bwd(ret_chunk, RET_CHUNK, sv["rq"], sv["rk"], sv["rv"], prm["ret_logit"][0], sv["r_stf"], dr_o, sel,
                  f"ret_scan_f_b_{i}")
    rb = scan_bwd(ret_chunk, RET_CHUNK, sv["rqb"], sv["rkb"], sv["rvb"], prm["ret_logit"][1], sv["r_stb"],
                  _seqflip(dr_o, lc), sel, f"ret_scan_b_b_{i}")
    drq, drk, drv = (ew_sum([a, _seqflip(b, lc)], name=f"ret_dirsum_{i}_{n}") for n, (a, b) in enumerate(zip(rf[:3], rb[:3])))
    dret_logit = jnp.stack([rf[3][0, :RET_HEADS], rb[3][0, :RET_HEADS]])
    (dp_rq, dp_rk), _, _ = rowwise_bwd(
        f"ret_prep_b_{i}", ret_prep_fn, [(p, 512, 0), (p, 512, 1), cos_t, sin_t], [], [], [drq, drk],
        [True, True, False, False], tm, nct)
    dp = jnp.concatenate([dp_rq, dp_rk, drv, dp_g, dp_qkv, dp_z, dp_aq, dp_ak, dav, dp_ab], 1)
    dh1 = mm(dp, wts["w_in"], "nt", name=f"proj_in_dx_{i}")
    dw_in = mm(sv["h1"], dp, "tn", name=f"proj_in_dw_{i}")
    (dxb,), (dshift1, dscale1), _ = rowwise_bwd(
        f"mod1_b_{i}", modulate_fn, [sv["xin"]], [seg(0), seg(1)], [], [dh1], [True], tm, nct)
    dxin = ew_sum([dxa, dxb], name=f"dxin_{i}")
    dmods = jnp.concatenate([dshift1, dscale1, dgate1, dshift2, dscale2, dgate2], 1)
    big = dict(w_in=dw_in, w_o=dw_o, w_ffn_in=dw_ffn_in, w_ffn_out=dw_ffn_out)
    small = dict(ln1_w=both(dln1w)[0], ln1_b=both(dln1b)[0], ln2_w=both(dln2w)[0], ln2_b=both(dln2b)[0],
                 dn_norm_w=both(ddn_norm_w)[0], att_qn_w=both(dqn_w)[0], att_kn_w=both(dkn_w)[0],
                 dn_conv_w=dconv_w[:DN_CONV_K], ret_decay_logit=dret_logit,
                 dn_a_log=both(da_log)[0, :2 * DN_HEADS].reshape(2, DN_HEADS),
                 dn_dt_bias=both(ddt_b)[0, :2 * DN_HEADS].reshape(2, DN_HEADS))
    return dxin, dmods, big, small


BIG = ("w_in", "w_o", "w_ffn_in", "w_ffn_out")
SMALL = ("c_ctx", "b_ada", "ret_decay_logit", "dn_conv_w", "dn_a_log", "dn_dt_bias", "dn_norm_w", "att_qn_w", "att_kn_w",
         "ln1_w", "ln1_b", "ln2_w", "ln2_b")
WEIGHTS = ("c_ctx", "w_ada", "b_ada", "w_in", "ret_decay_logit", "dn_conv_w", "dn_a_log", "dn_dt_bias", "dn_norm_w",
           "att_qn_w", "att_kn_w", "w_o", "ln1_w", "ln1_b", "w_ffn_in", "w_ffn_out", "ln2_w", "ln2_b")


def _chip_major(g, name):
    if name in ("w_in", "w_ffn_in"):
        r, cols = g.shape
        return g.reshape(r, 4, cols // 4).transpose(1, 0, 2)
    return g.reshape(4, g.shape[0] // 4, g.shape[1])


def kernel(x, c, ctx, c_ctx, w_ada, b_ada, w_in, ret_decay_logit, dn_conv_w, dn_a_log, dn_dt_bias, dn_norm_w, att_qn_w, att_kn_w, w_o, ln1_w, ln1_b, w_ffn_in, w_ffn_out, ln2_w, ln2_b, loss_target, m_c_ctx, m_w_ada, m_b_ada, m_w_in, m_ret_decay_logit, m_dn_conv_w, m_dn_a_log, m_dn_dt_bias, m_dn_norm_w, m_att_qn_w, m_att_kn_w, m_w_o, m_ln1_w, m_ln1_b, m_w_ffn_in, m_w_ffn_out, m_ln2_w, m_ln2_b, v_c_ctx, v_w_ada, v_b_ada, v_w_in, v_ret_decay_logit, v_dn_conv_w, v_dn_a_log, v_dn_dt_bias, v_dn_norm_w, v_att_qn_w, v_att_kn_w, v_w_o, v_ln1_w, v_ln1_b, v_w_ffn_in, v_w_ffn_out, v_ln2_w, v_ln2_b):
    wv = dict(c_ctx=c_ctx, w_ada=w_ada, b_ada=b_ada, w_in=w_in, ret_decay_logit=ret_decay_logit, dn_conv_w=dn_conv_w,
              dn_a_log=dn_a_log, dn_dt_bias=dn_dt_bias, dn_norm_w=dn_norm_w, att_qn_w=att_qn_w, att_kn_w=att_kn_w, w_o=w_o,
              ln1_w=ln1_w, ln1_b=ln1_b, w_ffn_in=w_ffn_in, w_ffn_out=w_ffn_out, ln2_w=ln2_w, ln2_b=ln2_b)
    mv = dict(c_ctx=m_c_ctx, w_ada=m_w_ada, b_ada=m_b_ada, w_in=m_w_in, ret_decay_logit=m_ret_decay_logit,
              dn_conv_w=m_dn_conv_w, dn_a_log=m_dn_a_log, dn_dt_bias=m_dn_dt_bias, dn_norm_w=m_dn_norm_w,
              att_qn_w=m_att_qn_w, att_kn_w=m_att_kn_w, w_o=m_w_o, ln1_w=m_ln1_w, ln1_b=m_ln1_b, w_ffn_in=m_w_ffn_in,
              w_ffn_out=m_w_ffn_out, ln2_w=m_ln2_w, ln2_b=m_ln2_b)
    vv = dict(c_ctx=v_c_ctx, w_ada=v_w_ada, b_ada=v_b_ada, w_in=v_w_in, ret_decay_logit=v_ret_decay_logit,
              dn_conv_w=v_dn_conv_w, dn_a_log=v_dn_a_log, dn_dt_bias=v_dn_dt_bias, dn_norm_w=v_dn_norm_w,
              att_qn_w=v_att_qn_w, att_kn_w=v_att_kn_w, w_o=v_w_o, ln1_w=v_ln1_w, ln1_b=v_ln1_b, w_ffn_in=v_w_ffn_in,
              w_ffn_out=v_w_ffn_out, ln2_w=v_ln2_w, ln2_b=v_ln2_b)
    depth = w_in.shape[0]
    n_lat, lc = x.shape[1], ctx.shape[1]
    t = lc + n_lat
    tm = _pick(lc, (256, 128))
    xi, yi, ci = _place()
    bidx = 4 * xi + 2 * yi + ci
    chip = 2 * xi + yi
    ada_w = w_ada.shape[2]
    conv_sh = dn_conv_w.shape[2]

    slab0, offs0 = _pack([c, dn_conv_w])
    g0 = allgather8(slab0, "gather_cond").reshape(8, -1, LANES)
    c_all = jnp.concatenate([_unpack(g0[d], offs0)[0] for d in range(8)], 0)
    conv_full = jnp.concatenate([_unpack(g0[2 * k], offs0)[1] for k in range(4)], 2)
    c_raw = jnp.concatenate([c_all, c_ctx[None], jnp.zeros((LANES - 9, D_MODEL), F32)], 0)
    (cond,) = rowwise("cond_silu", lambda a: (_silu(a),), [c_raw], [], [], [(D_MODEL, F32)], LANES, 0)
    b_sh = lax.dynamic_slice(b_ada, (0, chip * ada_w), (depth, ada_w))
    mods_sh = []
    for i in range(depth):
        mi = mm(cond, w_ada[i], "nn", name=f"ada_{i}")
        (mi,) = rowwise(f"ada_bias_{i}", lambda a, b: (a + b,), [mi], [], [b_sh[i:i + 1]], [(ada_w, F32)], LANES, 0)
        mods_sh.append(mi[:16])
    slab1, offs1 = _pack([jnp.stack(mods_sh)])
    g1 = allgather8(slab1, "gather_mods").reshape(8, -1, LANES)
    mods_all = jnp.concatenate([_unpack(g1[2 * k], offs1)[0] for k in range(4)], 2)
    mod_lat = lax.dynamic_index_in_dim(mods_all, bidx, 1, keepdims=False)
    mod_ctx = mods_all[:, 8]
    mods = jnp.stack([mod_ctx, mod_lat], 1).reshape(depth, 2, 6, D_MODEL)

    wbf = {n: ew_sum([wv[n].reshape(-1, wv[n].shape[2])], name=f"cast_{n}", out_dtype=MXU_DTYPE).reshape(wv[n].shape) for n in BIG}
    layers_w = []
    for i in range(depth):
        g = gather_chips([wbf[n][i] for n in BIG], f"gather_w_{i}")
        gw = dict(zip(BIG, g))
        layers_w.append(dict(
            w_in=_permute_w_in(gw["w_in"].transpose(1, 0, 2).reshape(D_MODEL, PROJ_W)),
            w_o=gw["w_o"].reshape(D_MODEL, D_MODEL),
            w_ffn_in=gw["w_ffn_in"].transpose(1, 0, 2).reshape(D_MODEL, 2 * D_FF),
            w_ffn_out=gw["w_ffn_out"].reshape(D_FF, D_MODEL)))

    tabs = _rope_tables(n_lat, lc)
    prms = []
    for i in range(depth):
        prms.append(dict(
            ret_logit=[_slab(ret_decay_logit[i, 0]), _slab(ret_decay_logit[i, 1])], conv_w=conv_full[i],
            a_log=_slab(dn_a_log[i].reshape(-1)), dt_b=_slab(dn_dt_bias[i].reshape(-1)), dn_norm_w=dn_norm_w[i:i + 1],
            qn_w=att_qn_w[i:i + 1], kn_w=att_kn_w[i:i + 1], ln1_w=ln1_w[i:i + 1], ln1_b=ln1_b[i:i + 1],
            ln2_w=ln2_w[i:i + 1], ln2_b=ln2_b[i:i + 1]))

    rows = jnp.concatenate([ctx[0], x[0]], 0)
    saved = []
    for i in range(depth):
        rows, sv = _layer_fwd(rows, mods[i], layers_w[i], prms[i], tabs, lc, tm, i)
        saved.append(sv)
    loss_local, dy = loss_and_grad(rows[lc:], loss_target[0], tm, "loss")
    loss = lax.psum(loss_local[0, 0], ("x", "y", "c"))

    drows = jnp.concatenate([jnp.zeros((lc, D_MODEL), F32), dy], 0)
    dmods, big_g, small_g = [None] * depth, [None] * depth, [None] * depth
    for i in reversed(range(depth)):
        drows, dmods[i], big, small_g[i] = _layer_bwd(drows, saved[i], mods[i], layers_w[i], prms[i], tabs, lc, tm, i)
        big["w_in"] = _unpermute_dw_in(big["w_in"])
        big_g[i] = dict(zip(BIG, reduce_scatter([_chip_major(big[n], n) for n in BIG], str(i))))
    grad_x = drows[lc:][None]

    names = ("ln1_w", "ln1_b", "ln2_w", "ln2_b", "dn_norm_w", "att_qn_w", "att_kn_w", "dn_conv_w", "ret_decay_logit",
             "dn_a_log", "dn_dt_bias")
    loc = [jnp.stack(dmods)] + [jnp.stack([small_g[i][n] for i in range(depth)]) for n in names]
    slab2, offs2 = _pack(loc)
    g2 = allgather8(slab2, "gather_small").reshape(8, -1, LANES)
    tot = _unpack(sum_entries(g2, tuple(range(8)), "sum_small"), offs2)
    dmods_sum = tot[0]
    gsm = dict(zip(names, tot[1:]))
    dmod_lat = jnp.stack([_unpack(g2[d], offs2)[0][:, 1].reshape(depth, 6 * D_MODEL) for d in range(8)], 1)
    dmod_ctx = dmods_sum[:, 0].reshape(depth, 1, 6 * D_MODEL)
    dm = jnp.concatenate([dmod_lat, dmod_ctx, jnp.zeros((depth, LANES - 9, 6 * D_MODEL), F32)], 1)
    dm_sh = lax.dynamic_slice(dm, (0, 0, chip * ada_w), (depth, LANES, ada_w))
    g_w_ada = jnp.stack([mm(cond, dm_sh[i], "tn", name=f"ada_dw_{i}") for i in range(depth)])
    gsm["b_ada"] = ew_sum([dmods_sum[:, 0].reshape(-1, LANES), dmods_sum[:, 1].reshape(-1, LANES)], name="b_ada_sum").reshape(b_ada.shape)
    dctx_rows = jnp.concatenate([dm_sh[:, 8:9], jnp.zeros((depth, 15, ada_w), F32)], 1)
    part = ew_sum([mm(dctx_rows[i], w_ada[i], "nt", name=f"ada_dcond_{i}") for i in range(depth)], name="ada_dcond_sum")[0]
    slab3, offs3 = _pack([part])
    g3 = allgather8(slab3, "gather_dcond").reshape(8, -1, LANES)
    dcond_ctx = _unpack(sum_entries(g3, (0, 2, 4, 6), "sum_dcond"), offs3)[0]
    (dc_ctx,), _, _ = rowwise_bwd("c_ctx_silu_b", lambda a: (_silu(a),), [c_ctx.reshape(16, LANES)], [], [],
                                  [dcond_ctx.reshape(16, LANES)], [True], 16, 0)
    gsm["c_ctx"] = dc_ctx.reshape(c_ctx.shape)
    gsm["dn_conv_w"] = lax.dynamic_slice(gsm["dn_conv_w"], (0, 0, chip * conv_sh), (depth, DN_CONV_K, conv_sh))

    grads, delta, new_m, new_v = {}, {}, {}, {}
    gs, offs = _pack([gsm[n] for n in SMALL])
    ws, _ = _pack([wv[n] for n in SMALL])
    ms, _ = _pack([mv[n] for n in SMALL])
    vs, _ = _pack([vv[n] for n in SMALL])
    res = [_unpack(o, offs) for o in adamw(gs, ws, ms, vs, "adamw_small")]
    for j, n in enumerate(SMALL):
        grads[n], delta[n], new_m[n], new_v[n] = gsm[n], res[0][j], res[1][j], res[2][j]
    bigs = {n: jnp.stack([big_g[i][n] for i in range(depth)]) for n in BIG}
    bigs["w_ada"] = g_w_ada
    for n, g in bigs.items():
        shp = wv[n].shape
        flat = lambda a: a.reshape(-1, shp[2])
        d_, m_, v_ = adamw(flat(g), flat(wv[n]), flat(mv[n]), flat(vv[n]), f"adamw_{n}")
        grads[n], delta[n], new_m[n], new_v[n] = g.reshape(shp), d_.reshape(shp), m_.reshape(shp), v_.reshape(shp)
    return (loss, grad_x, *[grads[n] for n in WEIGHTS], *[delta[n] for n in WEIGHTS], *[new_m[n] for n in WEIGHTS],
            *[new_v[n] for n in WEIGHTS])
```

```python
import functools

import jax
import jax.numpy as jnp
from jax import lax
from jax.experimental import pallas as pl
from jax.experimental.pallas import tpu as pltpu

F32 = jnp.float32
MXU_DTYPE = jnp.bfloat16

D_MODEL = 2048
DEPTH = 4
GRID_W = 64
HEAD_DIM = 128
RET_HEADS = 4
DN_HEADS = 4
ATT_HEADS = 8
ATT_KV_HEADS = 2
RET_CHUNK = 128
DN_CHUNK = 64
DN_CONV_K = 5
ROPE_THETA = 10000.0
D_FF = 5632
PROJ_W = 5648
PROJ_PAD = 5760
DEEPNORM_ALPHA = (2 * DEPTH) ** 0.25
EPS = 1e-6
QK_SCALE = HEAD_DIM ** -0.5
ADAM_LR = 0.001
ADAM_B1 = 0.9
ADAM_B2 = 0.999
ADAM_EPS = 1e-08
ADAM_WD = 0.01
ADAM_STEP = 10
LANES = 128
MESH_ID = pl.DeviceIdType.MESH
NEG_BIG = -1e30


def _pick(n, prefs):
    for p in prefs:
        if n % p == 0:
            return p
    return n


def _contract(a, b, ca, cb, hp):
    dims = (((ca,), (cb,)), ((), ()))
    if hp:
        return lax.dot_general(a, b, dims, precision=lax.Precision.HIGHEST, preferred_element_type=F32)
    return lax.dot_general(a.astype(MXU_DTYPE), b.astype(MXU_DTYPE), dims, preferred_element_type=F32)


@functools.partial(jax.custom_vjp, nondiff_argnums=(2, 3, 4))
def mdot(a, b, ca, cb, hp=False):
    return _contract(a, b, ca, cb, hp)


def _mdot_fwd(a, b, ca, cb, hp):
    return _contract(a, b, ca, cb, hp), (a, b)


def _mdot_bwd(ca, cb, hp, res, g):
    a, b = res
    da = mdot(g, b, 1, 1 - cb, hp) if ca == 1 else mdot(b, g, 1 - cb, 1, hp)
    db = mdot(a, g, 1 - ca, 0, hp) if cb == 0 else mdot(g, a, 0, 1 - ca, hp)
    return da, db


mdot.defvjp(_mdot_fwd, _mdot_bwd)


@jax.custom_vjp
def swap_halves(x):
    return pltpu.roll(x, HEAD_DIM // 2, 1)


def _swap_fwd(x):
    return swap_halves(x), None


def _swap_bwd(_, g):
    return (swap_halves(g),)


swap_halves.defvjp(_swap_fwd, _swap_bwd)


@jax.custom_vjp
def tri_inv(a):
    n = a.shape[0]
    r = lax.broadcasted_iota(jnp.int32, (n, n), 0)
    c = lax.broadcasted_iota(jnp.int32, (n, n), 1)
    eye = (r == c).astype(F32)
    p = -a
    t = eye + p
    k = 2
    while k < n:
        p = _contract(p, p, 1, 0, True)
        t = t + _contract(t, p, 1, 0, True)
        k *= 2
    return t


def _tri_inv_fwd(a):
    t = tri_inv(a)
    return t, t


def _tri_inv_bwd(t, g):
    return (-_contract(_contract(t, g, 0, 0, True), t, 1, 1, True),)


tri_inv.defvjp(_tri_inv_fwd, _tri_inv_bwd)


def _sigmoid(x):
    return 1.0 / (1.0 + jnp.exp(-x))


def _silu(x):
    return x * _sigmoid(x)


def _softplus(x):
    return jnp.maximum(x, 0.0) + jnp.log(1.0 + jnp.exp(-jnp.abs(x)))


def _lane_pick(slab, idx):
    lane = lax.broadcasted_iota(jnp.int32, slab.shape, 1)
    return jnp.sum(jnp.where(lane == idx, slab, 0.0), axis=1, keepdims=True)


def _heads(x):
    return [x[:, h * HEAD_DIM:(h + 1) * HEAD_DIM] for h in range(x.shape[1] // HEAD_DIM)]


def _rope(x, cos, sin):
    return x * cos + swap_halves(x) * sin


def _rms(x):
    return x * lax.rsqrt(jnp.mean(x * x, -1, keepdims=True) + EPS)


def _l2n(x):
    return x * lax.rsqrt(jnp.sum(x * x, -1, keepdims=True) + EPS)


def mm(a, b, mode, out_dtype=F32, name="mm"):
    if mode == "nn":
        (m, k), (k2, n) = a.shape, b.shape
        ca, cb = 1, 0
    elif mode == "nt":
        (m, k), (n, k2) = a.shape, b.shape
        ca, cb = 1, 1
    else:
        (k, m), (k2, n) = a.shape, b.shape
        ca, cb = 0, 0
    assert k == k2, (a.shape, b.shape, mode)
    tm = _pick(m, (768, 512, 384, 256, 128))
    tn = _pick(n, (1024, 768, 640, 512, 384, 256, 128))
    tk = _pick(k, (1024, 768, 640, 512, 384, 256, 128))
    nk = k // tk
    a_spec = pl.BlockSpec((tm, tk), lambda i, j, q: (i, q)) if ca == 1 else pl.BlockSpec((tk, tm), lambda i, j, q: (q, i))
    b_spec = pl.BlockSpec((tk, tn), lambda i, j, q: (q, j)) if cb == 0 else pl.BlockSpec((tn, tk), lambda i, j, q: (j, q))

    def body(a_ref, b_ref, o_ref, acc):
        q = pl.program_id(2)

        @pl.when(q == 0)
        def _():
            acc[...] = jnp.zeros_like(acc)

        acc[...] += _contract(a_ref[...], b_ref[...], ca, cb, False)

        @pl.when(q == nk - 1)
        def _():
            o_ref[...] = acc[...].astype(o_ref.dtype)

    return pl.pallas_call(
        body, name=name, grid=(m // tm, n // tn, nk),
        in_specs=[a_spec, b_spec], out_specs=pl.BlockSpec((tm, tn), lambda i, j, q: (i, j)),
        out_shape=jax.ShapeDtypeStruct((m, n), out_dtype),
        scratch_shapes=[pltpu.VMEM((tm, tn), F32)],
        compiler_params=pltpu.CompilerParams(dimension_semantics=("parallel", "parallel", "arbitrary")),
    )(a, b)


def _row_spec(r, tm):
    if isinstance(r, tuple):
        arr, width, blk = r
        return arr, pl.BlockSpec((tm, width), lambda i, blk=blk: (i, blk))
    return r, pl.BlockSpec((tm, r.shape[1]), lambda i: (i, 0))


def _seg_spec(s, nct):
    return pl.BlockSpec((1, 1, s.shape[2]), lambda i: (jnp.where(i < nct, 0, 1), 0, 0))


def _full_spec(s):
    return pl.BlockSpec(s.shape, lambda i: (0,) * s.ndim)


def rowwise(name, fn, rows, segs, shared, outs, tm, nct):
    arrs, specs = zip(*[_row_spec(r, tm) for r in rows])
    t = arrs[0].shape[0]
    nr, ns = len(rows), len(segs)

    def body(*refs):
        vals = [r[...] for r in refs[:nr]] + [r[0] for r in refs[nr:nr + ns]] + [r[...] for r in refs[nr + ns:nr + ns + len(shared)]]
        res = fn(*vals)
        for o_ref, v in zip(refs[nr + ns + len(shared):], res):
            o_ref[...] = v.astype(o_ref.dtype)

    res = pl.pallas_call(
        body, name=name, grid=(t // tm,),
        in_specs=list(specs) + [_seg_spec(s, nct) for s in segs] + [_full_spec(s) for s in shared],
        out_specs=[pl.BlockSpec((tm, w), lambda i: (i, 0)) for w, _ in outs],
        out_shape=[jax.ShapeDtypeStruct((t, w), d) for w, d in outs],
        compiler_params=pltpu.CompilerParams(dimension_semantics=("parallel",)),
    )(*arrs, *segs, *shared)
    return res


def rowwise_bwd(name, fn, rows, segs, shared, cts, want, tm, nct):
    arrs, specs = zip(*[_row_spec(r, tm) for r in rows])
    t = arrs[0].shape[0]
    groups = [list(c) if isinstance(c, (list, tuple)) else [c] for c in cts]
    cts = [c for g in groups for c in g]
    nr, ns, nsh, nc = len(rows), len(segs), len(shared), len(cts)
    widths = [sp.block_shape[1] for sp in specs]
    wanted = [i for i in range(nr) if want[i]]

    def body(*refs):
        i = pl.program_id(0)
        ins = refs[:nr + ns + nsh]
        ct_refs = refs[nr + ns + nsh:nr + ns + nsh + nc]
        o_refs = refs[nr + ns + nsh + nc:]
        vals = [r[...] for r in ins[:nr]] + [r[0] for r in ins[nr:nr + ns]] + [r[...] for r in ins[nr + ns:]]
        _, vjp = jax.vjp(fn, *vals)
        ct_vals, pos = [], 0
        for g in groups:
            tot = ct_refs[pos][...]
            for c_ref in ct_refs[pos + 1:pos + len(g)]:
                tot = tot + c_ref[...]
            ct_vals.append(tot)
            pos += len(g)
        grads = vjp(tuple(ct_vals))
        for o_ref, idx in zip(o_refs[:len(wanted)], wanted):
            o_ref[...] = grads[idx].astype(o_ref.dtype)
        first = jnp.logical_or(i == 0, i == nct)
        for o_ref, g in zip(o_refs[len(wanted):], grads[nr:]):
            @pl.when(first)
            def _(o_ref=o_ref):
                o_ref[...] = jnp.zeros_like(o_ref)

            o_ref[0] += g

    seg_idx = lambda i: jnp.where(i < nct, 0, 1)
    out_specs = [pl.BlockSpec((tm, widths[idx]), lambda i: (i, 0)) for idx in wanted]
    out_shape = [jax.ShapeDtypeStruct((t, widths[idx]), F32) for idx in wanted]
    for s in segs:
        out_specs.append(pl.BlockSpec((1, 1, s.shape[2]), lambda i: (seg_idx(i), 0, 0)))
        out_shape.append(jax.ShapeDtypeStruct((2, 1, s.shape[2]), F32))
    for s in shared:
        out_specs.append(pl.BlockSpec((1,) + s.shape, lambda i, nd=s.ndim: (seg_idx(i),) + (0,) * nd))
        out_shape.append(jax.ShapeDtypeStruct((2,) + s.shape, F32))
    res = pl.pallas_call(
        body, name=name, grid=(t // tm,),
        in_specs=list(specs) + [_seg_spec(s, nct) for s in segs] + [_full_spec(s) for s in shared]
        + [pl.BlockSpec((tm, c.shape[1]), lambda i: (i, 0)) for c in cts],
        out_specs=out_specs, out_shape=out_shape,
        compiler_params=pltpu.CompilerParams(dimension_semantics=("arbitrary",)),
    )(*arrs, *segs, *shared, *cts)
    nw = len(wanted)
    return res[:nw], res[nw:nw + ns], res[nw + ns:]


def modulate_fn(x, shift, scale):
    return (x * (1.0 + scale) + shift,)


def postnorm_fn(x, a, gate, w, b):
    z = DEEPNORM_ALPHA * x + gate * a
    mu = jnp.mean(z, -1, keepdims=True)
    zc = z - mu
    var = jnp.mean(zc * zc, -1, keepdims=True)
    return (zc * lax.rsqrt(var + EPS) * w + b,)


def ret_prep_fn(q, k, cos, sin):
    qs = [_rope(x, cos, sin) for x in _heads(q)]
    ks = [_rope(x, cos, sin) * QK_SCALE for x in _heads(k)]
    return jnp.concatenate(qs, 1), jnp.concatenate(ks, 1)


def ret_out_fn(of, ob, gate):
    ys = [_rms(o) * _silu(g) for o, g in zip(_heads(of + ob), _heads(gate))]
    return (jnp.concatenate(ys, 1),)


def dn_prep_fn(q, k, ab, a_log, dt_b):
    qs = [_l2n(x) * QK_SCALE for x in _heads(q)]
    ks = [_l2n(x) for x in _heads(k)]
    lane = lax.broadcasted_iota(jnp.int32, ab.shape, 1)
    g = -jnp.exp(a_log) * _softplus(ab + dt_b)
    beta = _sigmoid(ab)
    gb = jnp.where(lane < 2 * DN_HEADS, g, jnp.where(lane < 4 * DN_HEADS, beta, 0.0))
    return jnp.concatenate(qs, 1), jnp.concatenate(ks, 1), gb


def dn_out_fn(of, ob, z, w):
    ys = [_rms(o) * w * _silu(g) for o, g in zip(_heads(of + ob), _heads(z))]
    return (jnp.concatenate(ys, 1),)


def att_prep_fn(q, k, cos, sin, qw, kw):
    qs = [_rope(_rms(x) * qw, cos, sin) for x in _heads(q)]
    ks = [_rope(_rms(x) * kw, cos, sin) for x in _heads(k)]
    return jnp.concatenate(qs, 1), jnp.concatenate(ks, 1)


def swiglu_fwd(u, tm, name):
    t, w2 = u.shape
    w = w2 // 2
    cw = _pick(w, (512, 256, 128))
    ncb = w // cw

    def body(g_ref, u_ref, o_ref):
        o_ref[...] = (_silu(g_ref[...]) * u_ref[...]).astype(o_ref.dtype)

    return pl.pallas_call(
        body, name=name, grid=(t // tm, ncb),
        in_specs=[pl.BlockSpec((tm, cw), lambda i, j: (i, j)), pl.BlockSpec((tm, cw), lambda i, j: (i, j + ncb))],
        out_specs=pl.BlockSpec((tm, cw), lambda i, j: (i, j)),
        out_shape=jax.ShapeDtypeStruct((t, w), MXU_DTYPE),
        compiler_params=pltpu.CompilerParams(dimension_semantics=("parallel", "parallel")),
    )(u, u)


def swiglu_bwd(u, dact, tm, name):
    t, w2 = u.shape
    w = w2 // 2
    cw = _pick(w, (512, 256, 128))
    ncb = w // cw

    def body(g_ref, u_ref, d_ref, o_ref):
        j = pl.program_id(1)
        g = g_ref[...]
        s = _sigmoid(g)
        d = d_ref[...]
        dg = d * u_ref[...] * (s * (1.0 + g * (1.0 - s)))
        du = d * g * s
        o_ref[...] = jnp.where(j < ncb, dg, du).astype(o_ref.dtype)

    return pl.pallas_call(
        body, name=name, grid=(t // tm, 2 * ncb),
        in_specs=[pl.BlockSpec((tm, cw), lambda i, j: (i, j % ncb)), pl.BlockSpec((tm, cw), lambda i, j: (i, j % ncb + ncb)),
                  pl.BlockSpec((tm, cw), lambda i, j: (i, j % ncb))],
        out_specs=pl.BlockSpec((tm, cw), lambda i, j: (i, j)),
        out_shape=jax.ShapeDtypeStruct((t, w2), MXU_DTYPE),
        compiler_params=pltpu.CompilerParams(dimension_semantics=("parallel", "parallel")),
    )(u, u, dact)


CONV_HALO = 8


def _conv_mask(t, lc, s):
    r = lax.broadcasted_iota(jnp.int32, (t, 1), 0)
    src = r + s
    return jnp.logical_and(jnp.logical_and(src >= 0, src < t), (r < lc) == (src < lc))


def _conv_taps(pad_ref, w_ref, t, lc, flip):
    acc = None
    for k in range(DN_CONV_K):
        s = k - DN_CONV_K // 2
        off = -s if flip else s
        tap = pad_ref[pl.ds(CONV_HALO + off, t), :]
        term = jnp.where(_conv_mask(t, lc, off), tap, 0.0) * w_ref[k:k + 1, :]
        acc = term if acc is None else acc + term
    return acc


def _fill_pad(pad_ref, val, t):
    pad_ref[pl.ds(0, CONV_HALO), :] = jnp.zeros((CONV_HALO, LANES), F32)
    pad_ref[pl.ds(CONV_HALO + t, CONV_HALO), :] = jnp.zeros((CONV_HALO, LANES), F32)
    pad_ref[pl.ds(CONV_HALO, t), :] = val


def conv_fwd(p, col0, width, w, lc, name):
    t = p.shape[0]
    b0 = col0 // LANES

    def body(x_ref, w_ref, o_ref, pad):
        _fill_pad(pad, x_ref[...], t)
        o_ref[...] = _silu(_conv_taps(pad, w_ref, t, lc, False))

    return pl.pallas_call(
        body, name=name, grid=(width // LANES,),
        in_specs=[pl.BlockSpec((t, LANES), lambda j: (0, j + b0)), pl.BlockSpec((DN_CONV_K, LANES), lambda j: (0, j))],
        out_specs=pl.BlockSpec((t, LANES), lambda j: (0, j)),
        out_shape=jax.ShapeDtypeStruct((t, width), F32),
        scratch_shapes=[pltpu.VMEM((t + 2 * CONV_HALO, LANES), F32)],
        compiler_params=pltpu.CompilerParams(dimension_semantics=("parallel",)),
    )(p, w)


def conv_bwd(p, col0, width, w, dout, lc, name):
    t = p.shape[0]
    b0 = col0 // LANES

    def body(x_ref, w_ref, d_ref, dx_ref, dw_ref, pad):
        _fill_pad(pad, x_ref[...], t)
        y = _conv_taps(pad, w_ref, t, lc, False)
        sg = _sigmoid(y)
        dy = d_ref[...] * (sg * (1.0 + y * (1.0 - sg)))
        krow = lax.broadcasted_iota(jnp.int32, (8, LANES), 0)
        dw = jnp.zeros((8, LANES), F32)
        for k in range(DN_CONV_K):
            s = k - DN_CONV_K // 2
            tap = pad[pl.ds(CONV_HALO + s, t), :]
            dw_k = jnp.sum(jnp.where(_conv_mask(t, lc, s), tap, 0.0) * dy, axis=0, keepdims=True)
            dw = dw + jnp.where(krow == k, dw_k, 0.0)
        dw_ref[...] = dw
        _fill_pad(pad, dy, t)
        dx_ref[...] = _conv_taps(pad, w_ref, t, lc, True)

    return pl.pallas_call(
        body, name=name, grid=(width // LANES,),
        in_specs=[pl.BlockSpec((t, LANES), lambda j: (0, j + b0)), pl.BlockSpec((DN_CONV_K, LANES), lambda j: (0, j)),
                  pl.BlockSpec((t, LANES), lambda j: (0, j))],
        out_specs=[pl.BlockSpec((t, LANES), lambda j: (0, j)), pl.BlockSpec((8, LANES), lambda j: (0, j))],
        out_shape=[jax.ShapeDtypeStruct((t, width), F32), jax.ShapeDtypeStruct((8, width), F32)],
        scratch_shapes=[pltpu.VMEM((t + 2 * CONV_HALO, LANES), F32)],
        compiler_params=pltpu.CompilerParams(dimension_semantics=("parallel",)),
    )(p, w, dout)


def ret_chunk(q, k, v, s, logit_slab, h):
    c = q.shape[0]
    lg = -_softplus(-_lane_pick(logit_slab, h))
    i = lax.broadcasted_iota(jnp.int32, (c, c), 0)
    j = lax.broadcasted_iota(jnp.int32, (c, c), 1)
    rel = (i - j).astype(F32)
    decay = jnp.where(i >= j, jnp.exp(jnp.maximum(rel, 0.0) * lg), 0.0)
    pos = lax.broadcasted_iota(jnp.int32, (c, 1), 0).astype(F32)
    q_decay = jnp.exp((pos + 1.0) * lg)
    k_decay = jnp.exp((c - 1.0 - pos) * lg)
    intra = mdot(q, k, 1, 1) * decay
    o = mdot(intra, v, 1, 0) + mdot(q * q_decay, s, 1, 0)
    s_new = s * jnp.exp(c * lg) + mdot(k * k_decay, v, 0, 0)
    return o, s_new


def dn_chunk(q, k, v, s, gb, h):
    c = q.shape[0]
    g = _lane_pick(gb, h[0])
    beta = _lane_pick(gb, h[1])
    i = lax.broadcasted_iota(jnp.int32, (c, c), 0)
    j = lax.broadcasted_iota(jnp.int32, (c, c), 1)
    tri = i >= j
    gc = _lane_pick(mdot(tri.astype(F32), gb, 1, 0, True), h[0])
    gc_row = jnp.sum(jnp.where(i == j, gc, 0.0), axis=0, keepdims=True)
    decay = jnp.where(tri, jnp.exp(jnp.where(tri, gc - gc_row, 0.0)), 0.0)
    kb = k * beta
    vb = v * beta
    a = jnp.where(i > j, mdot(kb, k, 1, 1) * decay, 0.0)
    t = tri_inv(a)
    e = jnp.exp(gc)
    g_last = jnp.sum(g, axis=0, keepdims=True)
    w_val = mdot(t, vb, 1, 0)
    k_cum = mdot(t, kb * e, 1, 0)
    qk = mdot(q, k, 1, 1) * decay
    v_new = w_val - mdot(k_cum, s, 1, 0)
    o = mdot(q * e, s, 1, 0) + mdot(qk, v_new, 1, 0)
    s_new = s * jnp.exp(g_last) + mdot(k * jnp.exp(g_last - gc), v_new, 0, 0)
    return o, s_new


def scan_fwd(chunk_fn, c, q, k, v, slab, hsel, name):
    t, wid = q.shape
    n, nheads = t // c, wid // HEAD_DIM
    per_token = slab.shape[0] != 1
    row = pl.BlockSpec((c, wid), lambda i: (i, 0))
    slab_spec = pl.BlockSpec((c, LANES), lambda i: (i, 0)) if per_token else _full_spec(slab)
    st_spec = pl.BlockSpec((1, nheads, HEAD_DIM, HEAD_DIM), lambda i: (i, 0, 0, 0))

    def body(q_ref, k_ref, v_ref, slab_ref, o_ref, st_ref, s_scr):
        @pl.when(pl.program_id(0) == 0)
        def _():
            s_scr[...] = jnp.zeros_like(s_scr)

        sb = slab_ref[...]
        for h in range(nheads):
            sl = slice(h * HEAD_DIM, (h + 1) * HEAD_DIM)
            s_h = s_scr[h]
            st_ref[0, h] = s_h
            o, s_new = chunk_fn(q_ref[:, sl], k_ref[:, sl], v_ref[:, sl], s_h, sb, hsel(h))
            o_ref[:, sl] = o
            s_scr[h] = s_new

    return pl.pallas_call(
        body, name=name, grid=(n,),
        in_specs=[row, row, row, slab_spec], out_specs=[row, st_spec],
        out_shape=[jax.ShapeDtypeStruct((t, wid), F32), jax.ShapeDtypeStruct((n, nheads, HEAD_DIM, HEAD_DIM), F32)],
        scratch_shapes=[pltpu.VMEM((nheads, HEAD_DIM, HEAD_DIM), F32)],
        compiler_params=pltpu.CompilerParams(dimension_semantics=("arbitrary",)),
    )(q, k, v, slab)


def scan_bwd(chunk_fn, c, q, k, v, slab, states, do, hsel, name):
    t, wid = q.shape
    n, nheads = t // c, wid // HEAD_DIM
    per_token = slab.shape[0] != 1
    row = pl.BlockSpec((c, wid), lambda i: (n - 1 - i, 0))
    slab_spec = pl.BlockSpec((c, LANES), lambda i: (n - 1 - i, 0)) if per_token else _full_spec(slab)
    st_spec = pl.BlockSpec((1, nheads, HEAD_DIM, HEAD_DIM), lambda i: (n - 1 - i, 0, 0, 0))

    def body(q_ref, k_ref, v_ref, slab_ref, st_ref, do_ref, dq_ref, dk_ref, dv_ref, dslab_ref, ds_scr):
        @pl.when(pl.program_id(0) == 0)
        def _():
            ds_scr[...] = jnp.zeros_like(ds_scr)
            if not per_token:
                dslab_ref[...] = jnp.zeros_like(dslab_ref)

        sb = slab_ref[...]
        dslab = None
        for h in range(nheads):
            sl = slice(h * HEAD_DIM, (h + 1) * HEAD_DIM)
            _, vjp = jax.vjp(lambda a, b, cc, s, z, h=h: chunk_fn(a, b, cc, s, z, hsel(h)),
                             q_ref[:, sl], k_ref[:, sl], v_ref[:, sl], st_ref[0, h], sb)
            dq, dk, dv, ds, dz = vjp((do_ref[:, sl], ds_scr[h]))
            dq_ref[:, sl] = dq
            dk_ref[:, sl] = dk
            dv_ref[:, sl] = dv
            ds_scr[h] = ds
            dslab = dz if dslab is None else dslab + dz
        if per_token:
            dslab_ref[...] = dslab
        else:
            dslab_ref[...] += dslab

    return pl.pallas_call(
        body, name=name, grid=(n,),
        in_specs=[row, row, row, slab_spec, st_spec, row], out_specs=[row, row, row, slab_spec],
        out_shape=[jax.ShapeDtypeStruct((t, wid), F32)] * 3 + [jax.ShapeDtypeStruct(slab.shape, F32)],
        scratch_shapes=[pltpu.VMEM((nheads, HEAD_DIM, HEAD_DIM), F32)],
        compiler_params=pltpu.CompilerParams(dimension_semantics=("arbitrary",)),
    )(q, k, v, slab, states, do)


def ret_chunk(q, k, v, s, logit_slab, sel):
    lane, rev = sel
    c = q.shape[0]
    lg = -_softplus(-_lane_pick(logit_slab, lane))
    i = lax.broadcasted_iota(jnp.int32, (c, c), 0)
    j = lax.broadcasted_iota(jnp.int32, (c, c), 1)
    rel = ((j - i) if rev else (i - j)).astype(F32)
    decay = jnp.where(rel >= 0, jnp.exp(jnp.maximum(rel, 0.0) * lg), 0.0)
    pos = lax.broadcasted_iota(jnp.int32, (c, 1), 0).astype(F32)
    pos = (c - 1.0 - pos) if rev else pos
    q_decay = jnp.exp((pos + 1.0) * lg)
    k_decay = jnp.exp((c - 1.0 - pos) * lg)
    intra = mdot(q, k, 1, 1) * decay
    o = mdot(intra, v, 1, 0) + mdot(q * q_decay, s, 1, 0)
    s_new = s * jnp.exp(c * lg) + mdot(k * k_decay, v, 0, 0)
    return o, s_new


def dn_chunk(q, k, v, s, gb, sel):
    g_lane, b_lane, rev = sel
    c = q.shape[0]
    g = _lane_pick(gb, g_lane)
    beta = _lane_pick(gb, b_lane)
    i = lax.broadcasted_iota(jnp.int32, (c, c), 0)
    j = lax.broadcasted_iota(jnp.int32, (c, c), 1)
    tri = (i <= j) if rev else (i >= j)
    strict = (i < j) if rev else (i > j)
    gc = _lane_pick(mdot(tri.astype(F32), gb, 1, 0, True), g_lane)
    gc_row = jnp.sum(jnp.where(i == j, gc, 0.0), axis=0, keepdims=True)
    decay = jnp.where(tri, jnp.exp(jnp.where(tri, gc - gc_row, 0.0)), 0.0)
    kb = k * beta
    vb = v * beta
    a = jnp.where(strict, mdot(kb, k, 1, 1) * decay, 0.0)
    t = tri_inv(a)
    e = jnp.exp(gc)
    g_last = jnp.sum(g, axis=0, keepdims=True)
    w_val = mdot(t, vb, 1, 0)
    k_cum = mdot(t, kb * e, 1, 0)
    qk = mdot(q, k, 1, 1) * decay
    v_new = w_val - mdot(k_cum, s, 1, 0)
    o = mdot(q * e, s, 1, 0) + mdot(qk, v_new, 1, 0)
    s_new = s * jnp.exp(g_last) + mdot(k * jnp.exp(g_last - gc), v_new, 0, 0)
    return o, s_new


def _scan_maps(n, ncc):
    return (lambda s: s), (lambda s: jnp.where(s < ncc, ncc - 1 - s, n - 1 - (s - ncc)))


def scan_fwd(chunk_fn, c, q, k, v, slab, sels, lc, name):
    t, wid = q.shape
    n, nheads, ncc = t // c, wid // HEAD_DIM, lc // c
    per_token = slab.shape[0] != 1
    maps = _scan_maps(n, ncc)
    rows = [pl.BlockSpec((c, wid), lambda s, m=m: (m(s), 0)) for m in maps]
    slabs = [pl.BlockSpec((c, LANES), lambda s, m=m: (m(s), 0)) if per_token else _full_spec(slab) for m in maps]
    st_spec = pl.BlockSpec((1, 2 * nheads, HEAD_DIM, HEAD_DIM), lambda s: (s, 0, 0, 0))

    def body(qf, kf, vf, sbf, qb, kb, vb, sbb, of_ref, ob_ref, st_ref, s_scr):
        @pl.when(pl.program_id(0) == 0)
        def _():
            s_scr[...] = jnp.zeros_like(s_scr)

        work = []
        for d, (qr, kr, vr, sr) in enumerate(((qf, kf, vf, sbf), (qb, kb, vb, sbb))):
            sb = sr[...]
            for h in range(nheads):
                sl = slice(h * HEAD_DIM, (h + 1) * HEAD_DIM)
                work.append((d, h, sl, qr[:, sl], kr[:, sl], vr[:, sl], s_scr[d * nheads + h], sb))
        res = [chunk_fn(qq, kk, vv, ss, sb, sels[d][h]) for d, h, sl, qq, kk, vv, ss, sb in work]
        for (d, h, sl, _, _, _, ss, _), (o, s_new) in zip(work, res):
            st_ref[0, d * nheads + h] = ss
            (of_ref, ob_ref)[d][:, sl] = o
            s_scr[d * nheads + h] = s_new

    return pl.pallas_call(
        body, name=name, grid=(n,),
        in_specs=[rows[0]] * 3 + [slabs[0]] + [rows[1]] * 3 + [slabs[1]], out_specs=[rows[0], rows[1], st_spec],
        out_shape=[jax.ShapeDtypeStruct((t, wid), F32)] * 2 + [jax.ShapeDtypeStruct((n, 2 * nheads, HEAD_DIM, HEAD_DIM), F32)],
        scratch_shapes=[pltpu.VMEM((2 * nheads, HEAD_DIM, HEAD_DIM), F32)],
        compiler_params=pltpu.CompilerParams(dimension_semantics=("arbitrary",)),
    )(q, k, v, slab, q, k, v, slab)


def scan_bwd(chunk_fn, c, q, k, v, slab, states, do, sels, lc, name):
    t, wid = q.shape
    n, nheads, ncc = t // c, wid // HEAD_DIM, lc // c
    per_token = slab.shape[0] != 1
    maps = [lambda s, m=m: m(n - 1 - s) for m in _scan_maps(n, ncc)]
    rows = [pl.BlockSpec((c, wid), lambda s, m=m: (m(s), 0)) for m in maps]
    slabs = [pl.BlockSpec((c, LANES), lambda s, m=m: (m(s), 0)) if per_token else _full_spec(slab) for m in maps]
    st_spec = pl.BlockSpec((1, 2 * nheads, HEAD_DIM, HEAD_DIM), lambda s: (n - 1 - s, 0, 0, 0))
    n_slab_out = 2 if per_token else 1

    def body(*refs):
        ins = (refs[0:4] + (refs[9],), refs[4:8] + (refs[10],))
        st_ref = refs[8]
        outs = refs[11:11 + 6 + n_slab_out]
        ds_scr = refs[-1]

        @pl.when(pl.program_id(0) == 0)
        def _():
            ds_scr[...] = jnp.zeros_like(ds_scr)
            if not per_token:
                outs[6][...] = jnp.zeros_like(outs[6])

        work = []
        for d, (qr, kr, vr, sr, dor) in enumerate(ins):
            sb = sr[...]
            for h in range(nheads):
                sl = slice(h * HEAD_DIM, (h + 1) * HEAD_DIM)
                idx = d * nheads + h
                work.append((d, h, sl, idx, (qr[:, sl], kr[:, sl], vr[:, sl], st_ref[0, idx], sb), (dor[:, sl], ds_scr[idx])))
        res = []
        for d, h, sl, idx, prim, cot in work:
            _, vjp = jax.vjp(lambda a, b, cc, s, z, d=d, h=h: chunk_fn(a, b, cc, s, z, sels[d][h]), *prim)
            res.append(vjp(cot))
        dslab = [None, None]
        for (d, h, sl, idx, _, _), (dq, dk, dv, ds, dz) in zip(work, res):
            outs[3 * d][:, sl] = dq
            outs[3 * d + 1][:, sl] = dk
            outs[3 * d + 2][:, sl] = dv
            ds_scr[idx] = ds
            dslab[d] = dz if dslab[d] is None else dslab[d] + dz
        if per_token:
            outs[6][...] = dslab[0]
            outs[7][...] = dslab[1]
        else:
            outs[6][...] += dslab[0] + dslab[1]

    return pl.pallas_call(
        body, name=name, grid=(n,),
        in_specs=[rows[0]] * 3 + [slabs[0]] + [rows[1]] * 3 + [slabs[1]] + [st_spec, rows[0], rows[1]],
        out_specs=[rows[0]] * 3 + [rows[1]] * 3 + ([slabs[0], slabs[1]] if per_token else [slabs[0]]),
        out_shape=[jax.ShapeDtypeStruct((t, wid), F32)] * 6 + [jax.ShapeDtypeStruct(slab.shape, F32)] * n_slab_out,
        scratch_shapes=[pltpu.VMEM((2 * nheads, HEAD_DIM, HEAD_DIM), F32)],
        compiler_params=pltpu.CompilerParams(dimension_semantics=("arbitrary",)),
    )(q, k, v, slab, q, k, v, slab, states, do, do)


@jax.custom_vjp
def tri_inv_all(mats):
    n = mats[0].shape[0]
    r = lax.broadcasted_iota(jnp.int32, (n, n), 0)
    c = lax.broadcasted_iota(jnp.int32, (n, n), 1)
    eye = (r == c).astype(F32)
    ps = [-a for a in mats]
    ts = [eye + p for p in ps]
    k = 2
    while k < n:
        ps = [_contract(p, p, 1, 0, True) for p in ps]
        ts = [t + _contract(t, p, 1, 0, True) for t, p in zip(ts, ps)]
        k *= 2
    return tuple(ts)


def _tri_inv_all_fwd(mats):
    ts = tri_inv_all(mats)
    return ts, ts


def _tri_inv_all_bwd(ts, gs):
    left = [_contract(t, g, 0, 0, True) for t, g in zip(ts, gs)]
    return (tuple(-_contract(l, t, 1, 1, True) for l, t in zip(left, ts)),)


tri_inv_all.defvjp(_tri_inv_all_fwd, _tri_inv_all_bwd)


def ret_chunk(qs, ks, vs, ss, slabs, sels):
    c = qs[0].shape[0]
    i = lax.broadcasted_iota(jnp.int32, (c, c), 0)
    j = lax.broadcasted_iota(jnp.int32, (c, c), 1)
    pos0 = lax.broadcasted_iota(jnp.int32, (c, 1), 0).astype(F32)
    lgs = [-_softplus(-_lane_pick(slabs[d], lane)) for d, lane, _ in sels]
    rels = [((j - i) if rev else (i - j)).astype(F32) for _, _, rev in sels]
    decays = [jnp.where(rel >= 0, jnp.exp(jnp.maximum(rel, 0.0) * lg), 0.0) for rel, lg in zip(rels, lgs)]
    poss = [(c - 1.0 - pos0) if rev else pos0 for _, _, rev in sels]
    intra = [mdot(q, k, 1, 1) * dec for q, k, dec in zip(qs, ks, decays)]
    kv = [mdot(k * jnp.exp((c - 1.0 - pos) * lg), v, 0, 0) for k, v, pos, lg in zip(ks, vs, poss, lgs)]
    o1 = [mdot(a, v, 1, 0) for a, v in zip(intra, vs)]
    o2 = [mdot(q * jnp.exp((pos + 1.0) * lg), s, 1, 0) for q, s, pos, lg in zip(qs, ss, poss, lgs)]
    outs = [a + b for a, b in zip(o1, o2)]
    s_new = [s * jnp.exp(c * lg) + u for s, lg, u in zip(ss, lgs, kv)]
    return outs, s_new


def dn_chunk(qs, ks, vs, ss, slabs, sels):
    c = qs[0].shape[0]
    i = lax.broadcasted_iota(jnp.int32, (c, c), 0)
    j = lax.broadcasted_iota(jnp.int32, (c, c), 1)
    tris = [(i <= j) if rev else (i >= j) for _, _, _, rev in sels]
    stricts = [(i < j) if rev else (i > j) for _, _, _, rev in sels]
    gs = [_lane_pick(slabs[d], gl) for d, gl, _, _ in sels]
    betas = [_lane_pick(slabs[d], bl) for d, _, bl, _ in sels]
    sums = [mdot(tri.astype(F32), slabs[sel[0]], 1, 0, True) for tri, sel in zip(tris, sels)]
    gcs = [_lane_pick(cs, sel[1]) for cs, sel in zip(sums, sels)]
    gc_rows = [jnp.sum(jnp.where(i == j, gc, 0.0), axis=0, keepdims=True) for gc in gcs]
    decays = [jnp.where(tri, jnp.exp(jnp.where(tri, gc - gr, 0.0)), 0.0) for tri, gc, gr in zip(tris, gcs, gc_rows)]
    kbs = [k * b for k, b in zip(ks, betas)]
    vbs = [v * b for v, b in zip(vs, betas)]
    kk = [mdot(kb, k, 1, 1) for kb, k in zip(kbs, ks)]
    qk = [mdot(q, k, 1, 1) * dec for q, k, dec in zip(qs, ks, decays)]
    ts = tri_inv_all(tuple(jnp.where(st, a * dec, 0.0) for st, a, dec in zip(stricts, kk, decays)))
    es = [jnp.exp(gc) for gc in gcs]
    g_last = [jnp.sum(g, axis=0, keepdims=True) for g in gs]
    w_val = [mdot(t, vb, 1, 0) for t, vb in zip(ts, vbs)]
    k_cum = [mdot(t, kb * e, 1, 0) for t, kb, e in zip(ts, kbs, es)]
    ks_s = [mdot(kc, s, 1, 0) for kc, s in zip(k_cum, ss)]
    qs_s = [mdot(q * e, s, 1, 0) for q, e, s in zip(qs, es, ss)]
    v_new = [w - u for w, u in zip(w_val, ks_s)]
    o2 = [mdot(a, vn, 1, 0) for a, vn in zip(qk, v_new)]
    upd = [mdot(k * jnp.exp(gl - gc), vn, 0, 0) for k, gl, gc, vn in zip(ks, g_last, gcs, v_new)]
    outs = [a + b for a, b in zip(qs_s, o2)]
    s_new = [s * jnp.exp(gl) + u for s, gl, u in zip(ss, g_last, upd)]
    return outs, s_new


def scan_fwd(chunk_fn, c, q, k, v, slab, sels, lc, name):
    t, wid = q.shape
    n, nheads, ncc = t // c, wid // HEAD_DIM, lc // c
    per_token = slab.shape[0] != 1
    maps = _scan_maps(n, ncc)
    rows = [pl.BlockSpec((c, wid), lambda s, m=m: (m(s), 0)) for m in maps]
    slabs = [pl.BlockSpec((c, LANES), lambda s, m=m: (m(s), 0)) if per_token else _full_spec(slab) for m in maps]
    st_spec = pl.BlockSpec((1, 2 * nheads, HEAD_DIM, HEAD_DIM), lambda s: (s, 0, 0, 0))
    items = [(d, h) for d in range(2) for h in range(nheads)]
    flat_sels = [(d,) + tuple(sels[d][h]) for d, h in items]

    def body(qf, kf, vf, sbf, qb, kb, vb, sbb, of_ref, ob_ref, st_ref, s_scr):
        @pl.when(pl.program_id(0) == 0)
        def _():
            s_scr[...] = jnp.zeros_like(s_scr)

        qr, kr, vr = (qf, qb), (kf, kb), (vf, vb)
        sl = lambda h: slice(h * HEAD_DIM, (h + 1) * HEAD_DIM)
        ss = [s_scr[n_] for n_ in range(len(items))]
        outs, s_new = chunk_fn([qr[d][:, sl(h)] for d, h in items], [kr[d][:, sl(h)] for d, h in items],
                               [vr[d][:, sl(h)] for d, h in items], ss, [sbf[...], sbb[...]], flat_sels)
        for n_, (d, h) in enumerate(items):
            st_ref[0, n_] = ss[n_]
            (of_ref, ob_ref)[d][:, sl(h)] = outs[n_]
            s_scr[n_] = s_new[n_]

    return pl.pallas_call(
        body, name=name, grid=(n,),
        in_specs=[rows[0]] * 3 + [slabs[0]] + [rows[1]] * 3 + [slabs[1]], out_specs=[rows[0], rows[1], st_spec],
        out_shape=[jax.ShapeDtypeStruct((t, wid), F32)] * 2 + [jax.ShapeDtypeStruct((n, 2 * nheads, HEAD_DIM, HEAD_DIM), F32)],
        scratch_shapes=[pltpu.VMEM((2 * nheads, HEAD_DIM, HEAD_DIM), F32)],
        compiler_params=pltpu.CompilerParams(dimension_semantics=("arbitrary",)),
    )(q, k, v, slab, q, k, v, slab)


def scan_bwd(chunk_fn, c, q, k, v, slab, states, do, sels, lc, name):
    t, wid = q.shape
    n, nheads, ncc = t // c, wid // HEAD_DIM, lc // c
    per_token = slab.shape[0] != 1
    maps = [lambda s, m=m: m(n - 1 - s) for m in _scan_maps(n, ncc)]
    rows = [pl.BlockSpec((c, wid), lambda s, m=m: (m(s), 0)) for m in maps]
    slabs = [pl.BlockSpec((c, LANES), lambda s, m=m: (m(s), 0)) if per_token else _full_spec(slab) for m in maps]
    st_spec = pl.BlockSpec((1, 2 * nheads, HEAD_DIM, HEAD_DIM), lambda s: (n - 1 - s, 0, 0, 0))
    n_slab_out = 2 if per_token else 1
    items = [(d, h) for d in range(2) for h in range(nheads)]
    flat_sels = [(d,) + tuple(sels[d][h]) for d, h in items]

    def body(*refs):
        qr, kr, vr, sr = (refs[0], refs[4]), (refs[1], refs[5]), (refs[2], refs[6]), (refs[3], refs[7])
        st_ref, dor = refs[8], (refs[9], refs[10])
        outs = refs[11:11 + 6 + n_slab_out]
        ds_scr = refs[-1]

        @pl.when(pl.program_id(0) == 0)
        def _():
            ds_scr[...] = jnp.zeros_like(ds_scr)
            if not per_token:
                outs[6][...] = jnp.zeros_like(outs[6])

        sl = lambda h: slice(h * HEAD_DIM, (h + 1) * HEAD_DIM)
        prim = ([qr[d][:, sl(h)] for d, h in items], [kr[d][:, sl(h)] for d, h in items],
                [vr[d][:, sl(h)] for d, h in items], [st_ref[0, n_] for n_ in range(len(items))], [sr[0][...], sr[1][...]])
        cot = ([dor[d][:, sl(h)] for d, h in items], [ds_scr[n_] for n_ in range(len(items))])
        _, vjp = jax.vjp(lambda *a: chunk_fn(*a, flat_sels), *prim)
        dqs, dks, dvs, dss, dslabs = vjp(cot)
        for n_, (d, h) in enumerate(items):
            outs[3 * d][:, sl(h)] = dqs[n_]
            outs[3 * d + 1][:, sl(h)] = dks[n_]
            outs[3 * d + 2][:, sl(h)] = dvs[n_]
            ds_scr[n_] = dss[n_]
        if per_token:
            outs[6][...] = dslabs[0]
            outs[7][...] = dslabs[1]
        else:
            outs[6][...] += dslabs[0] + dslabs[1]

    return pl.pallas_call(
        body, name=name, grid=(n,),
        in_specs=[rows[0]] * 3 + [slabs[0]] + [rows[1]] * 3 + [slabs[1]] + [st_spec, rows[0], rows[1]],
        out_specs=[rows[0]] * 3 + [rows[1]] * 3 + ([slabs[0], slabs[1]] if per_token else [slabs[0]]),
        out_shape=[jax.ShapeDtypeStruct((t, wid), F32)] * 6 + [jax.ShapeDtypeStruct(slab.shape, F32)] * n_slab_out,
        scratch_shapes=[pltpu.VMEM((2 * nheads, HEAD_DIM, HEAD_DIM), F32)],
        compiler_params=pltpu.CompilerParams(dimension_semantics=("arbitrary",)),
    )(q, k, v, slab, q, k, v, slab, states, do, do)


def _att_tiles(t, lc):
    return _pick(lc, (256, 128)), _pick(t, (768, 384, 256, 128))


def _att_probs(q, k, lse, is_ctx, j, tk, lc):
    s = _contract(q, k, 1, 1, False) * QK_SCALE
    kidx = j * tk + lax.broadcasted_iota(jnp.int32, (1, tk), 1)
    masked = jnp.logical_and(is_ctx, kidx >= lc)
    return s, masked, (None if lse is None else jnp.where(masked, 0.0, jnp.exp(s - lse)))


def attn_fwd(qn, kn, v, lc, name):
    t = qn.shape[0]
    nh, nkv = qn.shape[1] // HEAD_DIM, kn.shape[1] // HEAD_DIM
    grp = nh // nkv
    tq, tk = _att_tiles(t, lc)
    nq, nk, nctq = t // tq, t // tk, lc // tq

    def body(q_ref, k_ref, v_ref, o_ref, lse_ref, m_scr, l_scr, acc):
        i, j = pl.program_id(1), pl.program_id(2)

        @pl.when(j == 0)
        def _():
            m_scr[...] = jnp.full_like(m_scr, NEG_BIG)
            l_scr[...] = jnp.zeros_like(l_scr)
            acc[...] = jnp.zeros_like(acc)

        is_ctx = i < nctq

        @pl.when(jnp.logical_or(jnp.logical_not(is_ctx), j * tk < lc))
        def _():
            s, masked, _ = _att_probs(q_ref[...], k_ref[...], None, is_ctx, j, tk, lc)
            s = jnp.where(masked, NEG_BIG, s)
            m_old = m_scr[...]
            m_new = jnp.maximum(m_old, jnp.max(s, axis=1, keepdims=True))
            alpha = jnp.exp(m_old - m_new)
            p = jnp.where(masked, 0.0, jnp.exp(s - m_new))
            l_scr[...] = alpha * l_scr[...] + jnp.sum(p, axis=1, keepdims=True)
            acc[...] = alpha * acc[...] + _contract(p, v_ref[...], 1, 0, False)
            m_scr[...] = m_new

        @pl.when(j == nk - 1)
        def _():
            o_ref[...] = acc[...] / l_scr[...]
            lse_ref[0] = m_scr[...] + jnp.log(l_scr[...])

    return pl.pallas_call(
        body, name=name, grid=(nh, nq, nk),
        in_specs=[pl.BlockSpec((tq, HEAD_DIM), lambda h, i, j: (i, h)),
                  pl.BlockSpec((tk, HEAD_DIM), lambda h, i, j: (j, h // grp)),
                  pl.BlockSpec((tk, HEAD_DIM), lambda h, i, j: (j, h // grp))],
        out_specs=[pl.BlockSpec((tq, HEAD_DIM), lambda h, i, j: (i, h)),
                   pl.BlockSpec((1, tq, 1), lambda h, i, j: (h, i, 0))],
        out_shape=[jax.ShapeDtypeStruct((t, nh * HEAD_DIM), F32), jax.ShapeDtypeStruct((nh, t, 1), F32)],
        scratch_shapes=[pltpu.VMEM((tq, 1), F32), pltpu.VMEM((tq, 1), F32), pltpu.VMEM((tq, HEAD_DIM), F32)],
        compiler_params=pltpu.CompilerParams(dimension_semantics=("parallel", "parallel", "arbitrary")),
    )(qn, kn, v)


def attn_bwd_dq(qn, kn, v, o, do, lse, lc, name):
    t = qn.shape[0]
    nh, nkv = qn.shape[1] // HEAD_DIM, kn.shape[1] // HEAD_DIM
    grp = nh // nkv
    tq, tk = _att_tiles(t, lc)
    nq, nk, nctq = t // tq, t // tk, lc // tq

    def body(q_ref, k_ref, v_ref, o_ref, do_ref, lse_ref, dq_ref, dl_ref, acc, dl_scr):
        i, j = pl.program_id(1), pl.program_id(2)

        @pl.when(j == 0)
        def _():
            acc[...] = jnp.zeros_like(acc)
            dl_scr[...] = jnp.sum(do_ref[...] * o_ref[...], axis=1, keepdims=True)

        is_ctx = i < nctq

        @pl.when(jnp.logical_or(jnp.logical_not(is_ctx), j * tk < lc))
        def _():
            _, _, p = _att_probs(q_ref[...], k_ref[...], lse_ref[0], is_ctx, j, tk, lc)
            dp = _contract(do_ref[...], v_ref[...], 1, 1, False)
            ds = p * (dp - dl_scr[...]) * QK_SCALE
            acc[...] += _contract(ds, k_ref[...], 1, 0, False)

        @pl.when(j == nk - 1)
        def _():
            dq_ref[...] = acc[...]
            dl_ref[0] = dl_scr[...]

    qspec = pl.BlockSpec((tq, HEAD_DIM), lambda h, i, j: (i, h))
    kspec = pl.BlockSpec((tk, HEAD_DIM), lambda h, i, j: (j, h // grp))
    rspec = pl.BlockSpec((1, tq, 1), lambda h, i, j: (h, i, 0))
    return pl.pallas_call(
        body, name=name, grid=(nh, nq, nk),
        in_specs=[qspec, kspec, kspec, qspec, qspec, rspec],
        out_specs=[qspec, rspec],
        out_shape=[jax.ShapeDtypeStruct((t, nh * HEAD_DIM), F32), jax.ShapeDtypeStruct((nh, t, 1), F32)],
        scratch_shapes=[pltpu.VMEM((tq, HEAD_DIM), F32), pltpu.VMEM((tq, 1), F32)],
        compiler_params=pltpu.CompilerParams(dimension_semantics=("parallel", "parallel", "arbitrary")),
    )(qn, kn, v, o, do, lse)


def attn_bwd_dkv(qn, kn, v, do, lse, delta, lc, name):
    t = qn.shape[0]
    nh, nkv = qn.shape[1] // HEAD_DIM, kn.shape[1] // HEAD_DIM
    grp = nh // nkv
    tq, tk = _att_tiles(t, lc)
    nq, nk, nctq = t // tq, t // tk, lc // tq
    nr = grp * nq

    def body(q_ref, k_ref, v_ref, do_ref, lse_ref, dl_ref, dk_ref, dv_ref, dk_acc, dv_acc):
        j, r = pl.program_id(1), pl.program_id(2)

        @pl.when(r == 0)
        def _():
            dk_acc[...] = jnp.zeros_like(dk_acc)
            dv_acc[...] = jnp.zeros_like(dv_acc)

        is_ctx = (r % nq) < nctq

        @pl.when(jnp.logical_or(jnp.logical_not(is_ctx), j * tk < lc))
        def _():
            _, _, p = _att_probs(q_ref[...], k_ref[...], lse_ref[0], is_ctx, j, tk, lc)
            dv_acc[...] += _contract(p, do_ref[...], 0, 0, False)
            dp = _contract(do_ref[...], v_ref[...], 1, 1, False)
            ds = p * (dp - dl_ref[0]) * QK_SCALE
            dk_acc[...] += _contract(ds, q_ref[...], 0, 0, False)

        @pl.when(r == nr - 1)
        def _():
            dk_ref[...] = dk_acc[...]
            dv_ref[...] = dv_acc[...]

    qspec = pl.BlockSpec((tq, HEAD_DIM), lambda g, j, r: (r % nq, g * grp + r // nq))
    kspec = pl.BlockSpec((tk, HEAD_DIM), lambda g, j, r: (j, g))
    rspec = pl.BlockSpec((1, tq, 1), lambda g, j, r: (g * grp + r // nq, r % nq, 0))
    return pl.pallas_call(
        body, name=name, grid=(nkv, nk, nr),
        in_specs=[qspec, kspec, kspec, qspec, rspec, rspec],
        out_specs=[kspec, kspec],
        out_shape=[jax.ShapeDtypeStruct((t, nkv * HEAD_DIM), F32), jax.ShapeDtypeStruct((t, nkv * HEAD_DIM), F32)],
        scratch_shapes=[pltpu.VMEM((tk, HEAD_DIM), F32), pltpu.VMEM((tk, HEAD_DIM), F32)],
        compiler_params=pltpu.CompilerParams(dimension_semantics=("parallel", "parallel", "arbitrary")),
    )(qn, kn, v, do, lse, delta)


LOG2E = 1.4426950408889634
LN2 = 0.6931471805599453


def _att_grid(qn, kn, lc):
    t = qn.shape[0]
    nkv = kn.shape[1] // HEAD_DIM
    grp = qn.shape[1] // HEAD_DIM // nkv
    tq, tk = _att_tiles(t, lc)
    return t, nkv, grp, tq, tk, t // tq, t // tk, lc // tq


def _att_paths(i, j, nctq, tk, lc, step):
    is_ctx = i < nctq

    @pl.when(jnp.logical_and(is_ctx, j * tk < lc))
    def _():
        kidx = j * tk + lax.broadcasted_iota(jnp.int32, (1, tk), 1)
        step(kidx >= lc)

    @pl.when(jnp.logical_not(is_ctx))
    def _():
        step(None)


def _att_scores2(q, k, hidden):
    s = _contract(q, k, 1, 1, False) * (QK_SCALE * LOG2E)
    return s if hidden is None else jnp.where(hidden, NEG_BIG, s)


def attn_fwd(qn, kn, v, lc, name):
    t, nkv, grp, tq, tk, nq, nk, nctq = _att_grid(qn, kn, lc)

    def body(q_ref, k_ref, v_ref, o_ref, lse_ref, m_scr, l_scr, acc):
        i, j = pl.program_id(1), pl.program_id(2)

        @pl.when(j == 0)
        def _():
            m_scr[...] = jnp.full_like(m_scr, NEG_BIG)
            l_scr[...] = jnp.zeros_like(l_scr)
            acc[...] = jnp.zeros_like(acc)

        def step(hidden):
            k, vv = k_ref[...], v_ref[...]
            qs = [q_ref[:, h * HEAD_DIM:(h + 1) * HEAD_DIM] for h in range(grp)]
            ms = [m_scr[h] for h in range(grp)]
            ls = [l_scr[h] for h in range(grp)]
            accs = [acc[h] for h in range(grp)]
            ss = [_att_scores2(q, k, hidden) for q in qs]
            m_new = [jnp.maximum(m, jnp.max(s, axis=1, keepdims=True)) for m, s in zip(ms, ss)]
            alpha = [jnp.exp2(m - mn) for m, mn in zip(ms, m_new)]
            ps = [jnp.exp2(s - mn) for s, mn in zip(ss, m_new)]
            pv = [_contract(p, vv, 1, 0, False) for p in ps]
            for h in range(grp):
                m_scr[h] = m_new[h]
                l_scr[h] = alpha[h] * ls[h] + jnp.sum(ps[h], axis=1, keepdims=True)
                acc[h] = alpha[h] * accs[h] + pv[h]

        _att_paths(i, j, nctq, tk, lc, step)

        @pl.when(j == nk - 1)
        def _():
            for h in range(grp):
                o_ref[:, h * HEAD_DIM:(h + 1) * HEAD_DIM] = acc[h] / l_scr[h]
                lse_ref[h] = m_scr[h] * LN2 + jnp.log(l_scr[h])

    wid = grp * HEAD_DIM
    return pl.pallas_call(
        body, name=name, grid=(nkv, nq, nk),
        in_specs=[pl.BlockSpec((tq, wid), lambda g, i, j: (i, g)),
                  pl.BlockSpec((tk, HEAD_DIM), lambda g, i, j: (j, g)),
                  pl.BlockSpec((tk, HEAD_DIM), lambda g, i, j: (j, g))],
        out_specs=[pl.BlockSpec((tq, wid), lambda g, i, j: (i, g)),
                   pl.BlockSpec((grp, tq, 1), lambda g, i, j: (g, i, 0))],
        out_shape=[jax.ShapeDtypeStruct((t, nkv * wid), F32), jax.ShapeDtypeStruct((nkv * grp, t, 1), F32)],
        scratch_shapes=[pltpu.VMEM((grp, tq, 1), F32), pltpu.VMEM((grp, tq, 1), F32), pltpu.VMEM((grp, tq, HEAD_DIM), F32)],
        compiler_params=pltpu.CompilerParams(dimension_semantics=("parallel", "parallel", "arbitrary")),
    )(qn, kn, v)


def attn_bwd_dq(qn, kn, v, o, do, lse, lc, name):
    t, nkv, grp, tq, tk, nq, nk, nctq = _att_grid(qn, kn, lc)

    def body(q_ref, k_ref, v_ref, o_ref, do_ref, lse_ref, dq_ref, dl_ref, acc, dl_scr):
        i, j = pl.program_id(1), pl.program_id(2)

        @pl.when(j == 0)
        def _():
            acc[...] = jnp.zeros_like(acc)
            for h in range(grp):
                sl = slice(h * HEAD_DIM, (h + 1) * HEAD_DIM)
                dl_scr[h] = jnp.sum(do_ref[:, sl] * o_ref[:, sl], axis=1, keepdims=True)

        def step(hidden):
            k, vv = k_ref[...], v_ref[...]
            sls = [slice(h * HEAD_DIM, (h + 1) * HEAD_DIM) for h in range(grp)]
            ss = [_att_scores2(q_ref[:, sl], k, hidden) for sl in sls]
            dps = [_contract(do_ref[:, sl], vv, 1, 1, False) for sl in sls]
            ps = [jnp.exp2(s - lse_ref[h] * LOG2E) for h, s in enumerate(ss)]
            dss = [p * (dp - dl_scr[h]) * QK_SCALE for h, (p, dp) in enumerate(zip(ps, dps))]
            upd = [_contract(ds, k, 1, 0, False) for ds in dss]
            for h in range(grp):
                acc[h] += upd[h]

        _att_paths(i, j, nctq, tk, lc, step)

        @pl.when(j == nk - 1)
        def _():
            for h in range(grp):
                dq_ref[:, h * HEAD_DIM:(h + 1) * HEAD_DIM] = acc[h]
                dl_ref[h] = dl_scr[h]

    wid = grp * HEAD_DIM
    qspec = pl.BlockSpec((tq, wid), lambda g, i, j: (i, g))
    kspec = pl.BlockSpec((tk, HEAD_DIM), lambda g, i, j: (j, g))
    rspec = pl.BlockSpec((grp, tq, 1), lambda g, i, j: (g, i, 0))
    return pl.pallas_call(
        body, name=name, grid=(nkv, nq, nk),
        in_specs=[qspec, kspec, kspec, qspec, qspec, rspec],
        out_specs=[qspec, rspec],
        out_shape=[jax.ShapeDtypeStruct((t, nkv * wid), F32), jax.ShapeDtypeStruct((nkv * grp, t, 1), F32)],
        scratch_shapes=[pltpu.VMEM((grp, tq, HEAD_DIM), F32), pltpu.VMEM((grp, tq, 1), F32)],
        compiler_params=pltpu.CompilerParams(dimension_semantics=("parallel", "parallel", "arbitrary")),
    )(qn, kn, v, o, do, lse)


def attn_bwd_dkv(qn, kn, v, do, lse, delta, lc, name):
    t, nkv, grp, tq, tk, nq, nk, nctq = _att_grid(qn, kn, lc)

    def body(q_ref, k_ref, v_ref, do_ref, lse_ref, dl_ref, dk_ref, dv_ref, dk_acc, dv_acc):
        j, i = pl.program_id(1), pl.program_id(2)

        @pl.when(i == 0)
        def _():
            dk_acc[...] = jnp.zeros_like(dk_acc)
            dv_acc[...] = jnp.zeros_like(dv_acc)

        def step(hidden):
            k, vv = k_ref[...], v_ref[...]
            sls = [slice(h * HEAD_DIM, (h + 1) * HEAD_DIM) for h in range(grp)]
            qs = [q_ref[:, sl] for sl in sls]
            dos = [do_ref[:, sl] for sl in sls]
            ss = [_att_scores2(q, k, hidden) for q in qs]
            dps = [_contract(do, vv, 1, 1, False) for do in dos]
            ps = [jnp.exp2(s - lse_ref[h] * LOG2E) for h, s in enumerate(ss)]
            dss = [p * (dp - dl_ref[h]) * QK_SCALE for h, (p, dp) in enumerate(zip(ps, dps))]
            dv_new = [_contract(p, do, 0, 0, False) for p, do in zip(ps, dos)]
            dk_new = [_contract(ds, q, 0, 0, False) for ds, q in zip(dss, qs)]
            dv_acc[...] += (dv_new[0] + dv_new[1]) + (dv_new[2] + dv_new[3]) if grp == 4 else sum(dv_new)
            dk_acc[...] += (dk_new[0] + dk_new[1]) + (dk_new[2] + dk_new[3]) if grp == 4 else sum(dk_new)

        _att_paths(i, j, nctq, tk, lc, step)

        @pl.when(i == nq - 1)
        def _():
            dk_ref[...] = dk_acc[...]
            dv_ref[...] = dv_acc[...]

    wid = grp * HEAD_DIM
    qspec = pl.BlockSpec((tq, wid), lambda g, j, i: (i, g))
    kspec = pl.BlockSpec((tk, HEAD_DIM), lambda g, j, i: (j, g))
    rspec = pl.BlockSpec((grp, tq, 1), lambda g, j, i: (g, i, 0))
    return pl.pallas_call(
        body, name=name, grid=(nkv, nk, nq),
        in_specs=[qspec, kspec, kspec, qspec, rspec, rspec],
        out_specs=[kspec, kspec],
        out_shape=[jax.ShapeDtypeStruct((t, nkv * HEAD_DIM), F32), jax.ShapeDtypeStruct((t, nkv * HEAD_DIM), F32)],
        scratch_shapes=[pltpu.VMEM((tk, HEAD_DIM), F32), pltpu.VMEM((tk, HEAD_DIM), F32)],
        compiler_params=pltpu.CompilerParams(dimension_semantics=("parallel", "parallel", "arbitrary")),
    )(qn, kn, v, do, lse, delta)


def loss_and_grad(y, target, tm, name):
    t, d = y.shape

    def body(y_ref, t_ref, l_ref, g_ref):
        @pl.when(pl.program_id(0) == 0)
        def _():
            l_ref[...] = jnp.zeros_like(l_ref)

        e = y_ref[...] - t_ref[...]
        g_ref[...] = e * (1.0 / d)
        l_ref[...] += 0.5 * jnp.sum(jnp.mean(e * e, axis=1, keepdims=True), axis=0, keepdims=True)

    return pl.pallas_call(
        body, name=name, grid=(t // tm,),
        in_specs=[pl.BlockSpec((tm, d), lambda i: (i, 0))] * 2,
        out_specs=[pl.BlockSpec((1, 1), lambda i: (0, 0)), pl.BlockSpec((tm, d), lambda i: (i, 0))],
        out_shape=[jax.ShapeDtypeStruct((1, 1), F32), jax.ShapeDtypeStruct((t, d), F32)],
        compiler_params=pltpu.CompilerParams(dimension_semantics=("arbitrary",)),
    )(y, target)


def _row_tile(rows, width):
    budget = max(8, (2 * 1024 * 1024) // (4 * width))
    for cand in (1024, 512, 256, 128, 64, 32, 16, 8):
        if cand <= budget and rows % cand == 0:
            return cand
    return rows


def ew_sum(arrs, name, out_dtype=F32):
    rows, width = arrs[0].shape
    tr = _row_tile(rows, width)

    def body(*refs):
        acc = refs[0][...].astype(F32)
        for r in refs[1:-1]:
            acc = acc + r[...].astype(F32)
        refs[-1][...] = acc.astype(refs[-1].dtype)

    spec = pl.BlockSpec((tr, width), lambda i: (i, 0))
    return pl.pallas_call(
        body, name=name, grid=(rows // tr,), in_specs=[spec] * len(arrs), out_specs=spec,
        out_shape=jax.ShapeDtypeStruct((rows, width), out_dtype),
        compiler_params=pltpu.CompilerParams(dimension_semantics=("parallel",)),
    )(*arrs)


def adamw(g, w, m, v, name):
    rows, width = g.shape
    tr = _row_tile(rows, width)

    def body(g_ref, w_ref, m_ref, v_ref, d_ref, mo_ref, vo_ref):
        gg = g_ref[...]
        m_new = ADAM_B1 * m_ref[...] + (1.0 - ADAM_B1) * gg
        v_new = ADAM_B2 * v_ref[...] + (1.0 - ADAM_B2) * jnp.square(gg)
        m_hat = m_new / (1.0 - ADAM_B1 ** ADAM_STEP)
        v_hat = v_new / (1.0 - ADAM_B2 ** ADAM_STEP)
        d_ref[...] = -ADAM_LR * (m_hat / (jnp.sqrt(v_hat) + ADAM_EPS) + ADAM_WD * w_ref[...])
        mo_ref[...] = m_new
        vo_ref[...] = v_new

    spec = pl.BlockSpec((tr, width), lambda i: (i, 0))
    return pl.pallas_call(
        body, name=name, grid=(rows // tr,), in_specs=[spec] * 4, out_specs=[spec] * 3,
        out_shape=[jax.ShapeDtypeStruct((rows, width), F32)] * 3,
        compiler_params=pltpu.CompilerParams(dimension_semantics=("parallel",)),
    )(g, w, m, v)


def _place():
    return lax.axis_index("x"), lax.axis_index("y"), lax.axis_index("c")


def _other_chips(x, y):
    return [(1 - x, y), (x, 1 - y), (1 - x, 1 - y)]


def allgather8(x_shard, name):
    m_per, n = x_shard.shape

    def body(x_ref, out_ref, send_sems, recv_sems, local_sem):
        x, y, c = _place()
        me, sibling = (x, y, c), (x, y, 1 - c)
        chips = _other_chips(x, y)

        def rows(px, py, pc):
            return out_ref.at[pl.ds((4 * px + 2 * py + pc) * m_per, m_per), :]

        def copy(k, block, to, src=None):
            return pltpu.make_async_remote_copy(
                src_ref=rows(*block) if src is None else src, dst_ref=rows(*block),
                send_sem=send_sems.at[k], recv_sem=recv_sems.at[k], device_id=to, device_id_type=MESH_ID)

        mine = pltpu.make_async_copy(x_ref, rows(*me), local_sem)
        mine.start()
        first = [copy(0, me, sibling, src=x_ref)]
        first += [copy(1 + j, me, (*chip, c), src=x_ref) for j, chip in enumerate(chips)]
        for cp in first:
            cp.start()
        passed = [copy(4 + j, (*chip, c), sibling) for j, chip in enumerate(chips)]
        for j, chip in enumerate(chips):
            copy(1 + j, (*chip, c), me).wait_recv()
            passed[j].start()
        copy(0, sibling, me).wait_recv()
        for j, chip in enumerate(chips):
            copy(4 + j, (*chip, 1 - c), me).wait_recv()
        for cp in first + passed:
            cp.wait_send()
        mine.wait()

    return pl.pallas_call(
        body, name=name,
        out_shape=jax.ShapeDtypeStruct((8 * m_per, n), x_shard.dtype),
        in_specs=[pl.BlockSpec(memory_space=pltpu.VMEM)],
        out_specs=pl.BlockSpec(memory_space=pltpu.VMEM),
        scratch_shapes=[pltpu.SemaphoreType.DMA((7,)), pltpu.SemaphoreType.DMA((7,)), pltpu.SemaphoreType.DMA],
    )(x_shard)


_ANY = pl.BlockSpec(memory_space=pl.ANY)


def gather_chips(shards, name):
    n = len(shards)

    def body(*refs):
        ins, outs = refs[:n], refs[n:2 * n]
        send_sems, recv_sems, local_sems = refs[2 * n:]
        x, y, c = _place()
        chips = _other_chips(x, y)
        started = []
        for a in range(n):
            loc = pltpu.make_async_copy(ins[a], outs[a].at[2 * x + y], local_sems.at[a])
            loc.start()
            started.append(loc)
        sends = []
        for a in range(n):
            for j, chip in enumerate(chips):
                cp = pltpu.make_async_remote_copy(
                    src_ref=ins[a], dst_ref=outs[a].at[2 * x + y], send_sem=send_sems.at[3 * a + j],
                    recv_sem=recv_sems.at[3 * a + j], device_id=(*chip, c), device_id_type=MESH_ID)
                cp.start()
                sends.append(cp)
        for a in range(n):
            for j, chip in enumerate(chips):
                pltpu.make_async_remote_copy(
                    src_ref=ins[a], dst_ref=outs[a].at[2 * chip[0] + chip[1]], send_sem=send_sems.at[3 * a + j],
                    recv_sem=recv_sems.at[3 * a + j], device_id=(*chip, c), device_id_type=MESH_ID).wait_recv()
        for cp in sends:
            cp.wait_send()
        for loc in started:
            loc.wait()

    return pl.pallas_call(
        body, name=name,
        out_shape=[jax.ShapeDtypeStruct((4,) + s.shape, s.dtype) for s in shards],
        in_specs=[_ANY] * n, out_specs=[_ANY] * n,
        scratch_shapes=[pltpu.SemaphoreType.DMA((3 * n,)), pltpu.SemaphoreType.DMA((3 * n,)), pltpu.SemaphoreType.DMA((n,))],
    )(*shards)


def rs_sibling(grads, name):
    n = len(grads)

    def body(*refs):
        ins, mine, got = refs[:n], refs[n:2 * n], refs[2 * n:3 * n]
        send_sems, recv_sems, local_sems = refs[3 * n:]
        x, y, c = _place()
        pend = []
        for a in range(n):
            h = ins[a].shape[1] // 2
            loc = pltpu.make_async_copy(ins[a].at[:, pl.ds(pl.multiple_of(c * h, 8), h), :], mine[a], local_sems.at[a])
            cp = pltpu.make_async_remote_copy(
                src_ref=ins[a].at[:, pl.ds(pl.multiple_of((1 - c) * h, 8), h), :], dst_ref=got[a],
                send_sem=send_sems.at[a], recv_sem=recv_sems.at[a], device_id=(x, y, 1 - c), device_id_type=MESH_ID)
            loc.start()
            cp.start()
            pend.append((loc, cp))
        for loc, cp in pend:
            cp.wait()
            loc.wait()

    half = [jax.ShapeDtypeStruct((g.shape[0], g.shape[1] // 2, g.shape[2]), g.dtype) for g in grads]
    return pl.pallas_call(
        body, name=name, out_shape=half + half, in_specs=[_ANY] * n, out_specs=[_ANY] * (2 * n),
        scratch_shapes=[pltpu.SemaphoreType.DMA((n,)), pltpu.SemaphoreType.DMA((n,)), pltpu.SemaphoreType.DMA((n,))],
    )(*grads)


def rs_chips(parts, name):
    n = len(parts)

    def body(*refs):
        ins, mine, got = refs[:n], refs[n:2 * n], refs[2 * n:3 * n]
        send_sems, recv_sems, local_sems = refs[3 * n:]
        x, y, c = _place()
        chips = _other_chips(x, y)
        pend = []
        for a in range(n):
            loc = pltpu.make_async_copy(ins[a].at[2 * x + y], mine[a], local_sems.at[a])
            loc.start()
            pend.append(loc)
            for j, chip in enumerate(chips):
                cp = pltpu.make_async_remote_copy(
                    src_ref=ins[a].at[2 * chip[0] + chip[1]], dst_ref=got[a].at[j],
                    send_sem=send_sems.at[3 * a + j], recv_sem=recv_sems.at[3 * a + j],
                    device_id=(*chip, c), device_id_type=MESH_ID)
                cp.start()
                pend.append(cp)
        for p in pend:
            p.wait()

    return pl.pallas_call(
        body, name=name,
        out_shape=[jax.ShapeDtypeStruct(p.shape[1:], p.dtype) for p in parts]
        + [jax.ShapeDtypeStruct((3,) + p.shape[1:], p.dtype) for p in parts],
        in_specs=[_ANY] * n, out_specs=[_ANY] * (2 * n),
        scratch_shapes=[pltpu.SemaphoreType.DMA((3 * n,)), pltpu.SemaphoreType.DMA((3 * n,)), pltpu.SemaphoreType.DMA((n,))],
    )(*parts)


def share_sibling(halves, name):
    n = len(halves)

    def body(*refs):
        ins, outs = refs[:n], refs[n:2 * n]
        send_sems, recv_sems, local_sems = refs[2 * n:]
        x, y, c = _place()
        pend = []
        for a in range(n):
            loc = pltpu.make_async_copy(ins[a], outs[a].at[c], local_sems.at[a])
            cp = pltpu.make_async_remote_copy(
                src_ref=ins[a], dst_ref=outs[a].at[c], send_sem=send_sems.at[a], recv_sem=recv_sems.at[a],
                device_id=(x, y, 1 - c), device_id_type=MESH_ID)
            loc.start()
            cp.start()
            pend.append((loc, cp))
        for a, (loc, cp) in enumerate(pend):
            cp.wait_send()
            pltpu.make_async_remote_copy(
                src_ref=ins[a], dst_ref=outs[a].at[1 - c], send_sem=send_sems.at[a], recv_sem=recv_sems.at[a],
                device_id=(x, y, 1 - c), device_id_type=MESH_ID).wait_recv()
            loc.wait()

    return pl.pallas_call(
        body, name=name, out_shape=[jax.ShapeDtypeStruct((2,) + h.shape, h.dtype) for h in halves],
        in_specs=[_ANY] * n, out_specs=[_ANY] * n,
        scratch_shapes=[pltpu.SemaphoreType.DMA((n,)), pltpu.SemaphoreType.DMA((n,)), pltpu.SemaphoreType.DMA((n,))],
    )(*halves)


def reduce_scatter(grads, tag):
    mine, got = _split(rs_sibling(grads, name=f"rs_sibling_{tag}"))
    pair = [ew_sum([a.reshape(-1, a.shape[2]), b.reshape(-1, b.shape[2])], name=f"rs_pair_{tag}_{i}").reshape(a.shape)
            for i, (a, b) in enumerate(zip(mine, got))]
    own, recv = _split(rs_chips(pair, name=f"rs_chips_{tag}"))
    tot = [ew_sum([a, b[0], b[1], b[2]], name=f"rs_quad_{tag}_{i}") for i, (a, b) in enumerate(zip(own, recv))]
    both = share_sibling(tot, name=f"rs_share_{tag}")
    return [b.reshape(-1, b.shape[2]) for b in both]


def _split(lst):
    n = len(lst) // 2
    return lst[:n], lst[n:]


def _sibling():
    x, y, c = _place()
    return (x, y, 1 - c)


def send_rows(src, name):
    r, c = src.shape
    tr = _row_tile(r, c)
    n = r // tr

    def body(x_ref, out_ref, send_sem, recv_sem):
        i = pl.program_id(0)
        cp = pltpu.make_async_remote_copy(
            src_ref=x_ref, dst_ref=out_ref.at[pl.ds(pl.multiple_of(i * tr, 8), tr), :], send_sem=send_sem,
            recv_sem=recv_sem, device_id=_sibling(), device_id_type=MESH_ID)
        cp.start()
        cp.wait_send()

        @pl.when(i == n - 1)
        def _():
            pltpu.make_async_remote_copy(src_ref=out_ref, dst_ref=out_ref, send_sem=send_sem, recv_sem=recv_sem,
                                         device_id=_sibling(), device_id_type=MESH_ID).wait_recv()

    return pl.pallas_call(
        body, name=name, grid=(n,), in_specs=[pl.BlockSpec((tr, c), lambda i: (i, 0))], out_specs=_ANY,
        out_shape=jax.ShapeDtypeStruct((r, c), src.dtype),
        scratch_shapes=[pltpu.SemaphoreType.DMA, pltpu.SemaphoreType.DMA],
        compiler_params=pltpu.CompilerParams(dimension_semantics=("arbitrary",)),
    )(src)


def gather_chips(shards, name):
    n = len(shards)

    def body(*refs):
        ins, outs = refs[:n], refs[n:2 * n]
        send_sems, recv_sems = refs[2 * n:]
        x, y, c = _place()
        chips = _other_chips(x, y)
        sends = []
        for a in range(n):
            for j, chip in enumerate(chips):
                cp = pltpu.make_async_remote_copy(
                    src_ref=ins[a], dst_ref=outs[a].at[2 * x + y], send_sem=send_sems.at[3 * a + j],
                    recv_sem=recv_sems.at[3 * a + j], device_id=(*chip, c), device_id_type=MESH_ID)
                cp.start()
                sends.append(cp)
        for a in range(n):
            for j, chip in enumerate(chips):
                pltpu.make_async_remote_copy(
                    src_ref=ins[a], dst_ref=outs[a].at[2 * chip[0] + chip[1]], send_sem=send_sems.at[3 * a + j],
                    recv_sem=recv_sems.at[3 * a + j], device_id=(*chip, c), device_id_type=MESH_ID).wait_recv()
        for cp in sends:
            cp.wait_send()

    res = pl.pallas_call(
        body, name=name,
        out_shape=[jax.ShapeDtypeStruct((4,) + s.shape, s.dtype) for s in shards],
        in_specs=[_ANY] * n, out_specs=[_ANY] * n,
        scratch_shapes=[pltpu.SemaphoreType.DMA((3 * n,)), pltpu.SemaphoreType.DMA((3 * n,))],
    )(*shards)
    x, y, _ = _place()
    return [lax.dynamic_update_slice(g, s[None], (2 * x + y,) + (0,) * s.ndim) for g, s in zip(res, shards)]


def rs_chips(parts, name):
    n = len(parts)

    def body(*refs):
        ins, got = refs[:n], refs[n:2 * n]
        send_sems, recv_sems = refs[2 * n:]
        x, y, c = _place()
        chips = _other_chips(x, y)
        pend = []
        for a in range(n):
            for j, chip in enumerate(chips):
                cp = pltpu.make_async_remote_copy(
                    src_ref=ins[a].at[2 * chip[0] + chip[1]], dst_ref=got[a].at[j],
                    send_sem=send_sems.at[3 * a + j], recv_sem=recv_sems.at[3 * a + j],
                    device_id=(*chip, c), device_id_type=MESH_ID)
                cp.start()
                pend.append(cp)
        for p in pend:
            p.wait()

    return pl.pallas_call(
        body, name=name,
        out_shape=[jax.ShapeDtypeStruct((3,) + p.shape[1:], p.dtype) for p in parts],
        in_specs=[_ANY] * n, out_specs=[_ANY] * n,
        scratch_shapes=[pltpu.SemaphoreType.DMA((3 * n,)), pltpu.SemaphoreType.DMA((3 * n,))],
    )(*parts)


def reduce_scatter(grads, tag):
    x, y, ci = _place()
    chip = 2 * x + y
    out = []
    pairs = []
    for i, g in enumerate(grads):
        _, r, c = g.shape
        h = r // 2
        keep = lax.dynamic_slice_in_dim(g, ci * h, h, 1).reshape(4 * h, c)
        give = lax.dynamic_slice_in_dim(g, (1 - ci) * h, h, 1).reshape(4 * h, c).astype(MXU_DTYPE)
        got = send_rows(give, f"rs_sibling_{tag}_{i}")
        pairs.append(ew_sum([keep, got], name=f"rs_pair_{tag}_{i}").reshape(4, h, c))
    recv = rs_chips([p.astype(MXU_DTYPE) for p in pairs], f"rs_chips_{tag}")
    for i, (p, b) in enumerate(zip(pairs, recv)):
        own = lax.dynamic_index_in_dim(p, chip, 0, keepdims=False)
        tot = ew_sum([own, b[0], b[1], b[2]], name=f"rs_quad_{tag}_{i}")
        other = send_rows(tot, f"rs_share_{tag}_{i}")
        h = tot.shape[0]
        out.append(lax.dynamic_update_slice(jnp.concatenate([other, other], 0), tot, (ci * h, 0)))
    return out


def sum_entries(g, idxs, name):
    _, rows, width = g.shape
    tr = _row_tile(rows, width)

    def body(g_ref, o_ref):
        acc = g_ref[idxs[0]]
        for d in idxs[1:]:
            acc = acc + g_ref[d]
        o_ref[...] = acc

    return pl.pallas_call(
        body, name=name, grid=(rows // tr,),
        in_specs=[pl.BlockSpec((8, tr, width), lambda i: (0, i, 0))], out_specs=pl.BlockSpec((tr, width), lambda i: (i, 0)),
        out_shape=jax.ShapeDtypeStruct((rows, width), F32),
        compiler_params=pltpu.CompilerParams(dimension_semantics=("parallel",)),
    )(g)


def _pack(arrs, rows_multiple=8):
    parts, offs, r = [], [], 0
    for a in arrs:
        flat = a.reshape(-1).astype(F32)
        nrow = -(-flat.shape[0] // LANES)
        parts.append(jnp.pad(flat, (0, nrow * LANES - flat.shape[0])).reshape(nrow, LANES))
        offs.append((r, nrow, a.shape))
        r += nrow
    pad = (-r) % rows_multiple
    if pad:
        parts.append(jnp.zeros((pad, LANES), F32))
    return jnp.concatenate(parts, 0), offs


def _unpack(slab, offs):
    outs = []
    for r, nrow, shape in offs:
        size = 1
        for s in shape:
            size *= s
        outs.append(slab[r:r + nrow].reshape(-1)[:size].reshape(shape))
    return outs


def _seqflip(a, lc):
    return jnp.concatenate([jnp.flip(a[:lc], 0), jnp.flip(a[lc:], 0)], 0)


def _slab(vec8):
    return jnp.pad(vec8.reshape(1, -1).astype(F32), ((0, 0), (0, LANES - vec8.size)))


def _rope_tables(n_lat, lc):
    rows = n_lat // GRID_W
    row = jnp.repeat(jnp.arange(rows, dtype=F32), GRID_W)
    col = jnp.tile(jnp.arange(GRID_W, dtype=F32), rows)
    n_freq = HEAD_DIM // 4
    inv = ROPE_THETA ** (-jnp.arange(n_freq, dtype=F32) / n_freq)
    ang = jnp.concatenate([row[:, None] * inv, col[:, None] * inv], -1)
    cos, sin = jnp.cos(ang), jnp.sin(ang)
    cos_t = jnp.concatenate([jnp.ones((lc, HEAD_DIM), F32), jnp.concatenate([cos, cos], -1)], 0)
    sin_t = jnp.concatenate([jnp.zeros((lc, HEAD_DIM), F32), jnp.concatenate([-sin, sin], -1)], 0)
    return cos_t, sin_t


def _permute_w_in(w):
    return jnp.concatenate([w[:, :4096], w[:, 4112:], w[:, 4096:4112], jnp.zeros((w.shape[0], PROJ_PAD - PROJ_W), w.dtype)], 1)


def _unpermute_dw_in(dw):
    return jnp.concatenate([dw[:, :4096], dw[:, 5632:5648], dw[:, 4096:5632]], 1)


RET_SELS = [[(d * RET_HEADS + h, d == 1) for h in range(RET_HEADS)] for d in range(2)]
DN_SELS = [[(d * DN_HEADS + h, (2 + d) * DN_HEADS + h, d == 1) for h in range(DN_HEADS)] for d in range(2)]


def _layer_fwd(xin, mods, wts, prm, tabs, lc, tm, i):
    nct = lc // tm
    seg = lambda j: mods[:, j:j + 1, :]
    cos_t, sin_t = tabs
    sv = {}
    (h1,) = rowwise(f"mod1_{i}", modulate_fn, [xin], [seg(0), seg(1)], [], [(D_MODEL, MXU_DTYPE)], tm, nct)
    p = mm(h1, wts["w_in"], "nn", name=f"proj_in_{i}")
    rq, rk = rowwise(f"ret_prep_{i}", ret_prep_fn, [(p, 512, 0), (p, 512, 1), cos_t, sin_t], [], [],
                     [(512, F32), (512, F32)], tm, nct)
    rv = p[:, 1024:1536]
    r_of, r_ob, r_st = scan_fwd(ret_chunk, RET_CHUNK, rq, rk, rv, prm["ret_logit"], RET_SELS, lc, f"ret_scan_{i}")
    (y_ret,) = rowwise(f"ret_out_{i}", ret_out_fn, [r_of, r_ob, (p, 512, 3)], [], [], [(512, F32)], tm, nct)
    qkvc = conv_fwd(p, 2048, 3 * 512, prm["conv_w"], lc, f"dn_conv_{i}")
    dq, dk, gb = rowwise(f"dn_prep_{i}", dn_prep_fn, [(qkvc, 512, 0), (qkvc, 512, 1), (p, LANES, 44)], [],
                         [prm["a_log"], prm["dt_b"]], [(512, F32), (512, F32), (LANES, F32)], tm, nct)
    dv = qkvc[:, 1024:1536]
    d_of, d_ob, d_st = scan_fwd(dn_chunk, DN_CHUNK, dq, dk, dv, gb, DN_SELS, lc, f"dn_scan_{i}")
    (y_dn,) = rowwise(f"dn_out_{i}", dn_out_fn, [d_of, d_ob, (p, 512, 7)], [], [prm["dn_norm_w"]], [(512, F32)], tm, nct)
    aq, ak, av = rowwise(f"att_prep_{i}", lambda q, k, v, *rest: att_prep_fn(q, k, *rest) + (v,),
                         [(p, 1024, 4), (p, 256, 20), (p, 256, 21), cos_t, sin_t], [], [prm["qn_w"], prm["kn_w"]],
                         [(1024, MXU_DTYPE), (256, MXU_DTYPE), (256, MXU_DTYPE)], tm, nct)
    ao, lse = attn_fwd(aq, ak, av, lc, f"attn_fwd_{i}")
    y = jnp.concatenate([y_ret, y_dn, ao], 1)
    a1 = mm(y, wts["w_o"], "nn", name=f"proj_out_{i}")
    (x1,) = rowwise(f"postnorm1_{i}", postnorm_fn, [xin, a1], [seg(2)], [prm["ln1_w"], prm["ln1_b"]], [(D_MODEL, F32)], tm, nct)
    (h2,) = rowwise(f"mod2_{i}", modulate_fn, [x1], [seg(3), seg(4)], [], [(D_MODEL, MXU_DTYPE)], tm, nct)
    u = mm(h2, wts["w_ffn_in"], "nn", name=f"ffn_in_{i}")
    act = swiglu_fwd(u, tm, f"swiglu_{i}")
    a2 = mm(act, wts["w_ffn_out"], "nn", name=f"ffn_out_{i}")
    (x2,) = rowwise(f"postnorm2_{i}", postnorm_fn, [x1, a2], [seg(5)], [prm["ln2_w"], prm["ln2_b"]], [(D_MODEL, F32)], tm, nct)
    sv.update(xin=xin, h1=h1, p=p, rq=rq, rk=rk, rv=rv, r_st=r_st, r_of=r_of, r_ob=r_ob, qkvc=qkvc, dq=dq, dk=dk, dv=dv,
              gb=gb, d_st=d_st, d_of=d_of, d_ob=d_ob, aq=aq, ak=ak, av=av, ao=ao, lse=lse, y=y, a1=a1, x1=x1, h2=h2,
              u=u, act=act, a2=a2)
    return x2, sv


def _layer_bwd(dx2, sv, mods, wts, prm, tabs, lc, tm, i):
    nct = lc // tm
    seg = lambda j: mods[:, j:j + 1, :]
    cos_t, sin_t = tabs
    p = sv["p"]
    both = lambda g: g[0] + g[1]
    (dx1a, da2), (dgate2,), (dln2w, dln2b) = rowwise_bwd(
        f"postnorm2_b_{i}", postnorm_fn, [sv["x1"], sv["a2"]], [seg(5)], [prm["ln2_w"], prm["ln2_b"]], [dx2], [True, True], tm, nct)
    dact = mm(da2, wts["w_ffn_out"], "nt", name=f"ffn_out_dx_{i}")
    dw_ffn_out = mm(sv["act"], da2, "tn", name=f"ffn_out_dw_{i}")
    du = swiglu_bwd(sv["u"], dact, tm, f"swiglu_b_{i}")
    dh2 = mm(du, wts["w_ffn_in"], "nt", name=f"ffn_in_dx_{i}")
    dw_ffn_in = mm(sv["h2"], du, "tn", name=f"ffn_in_dw_{i}")
    (dx1b,), (dshift2, dscale2), _ = rowwise_bwd(
        f"mod2_b_{i}", modulate_fn, [sv["x1"]], [seg(3), seg(4)], [], [dh2], [True], tm, nct)
    dx1 = ew_sum([dx1a, dx1b], name=f"dx1_{i}")
    (dxa, da1), (dgate1,), (dln1w, dln1b) = rowwise_bwd(
        f"postnorm1_b_{i}", postnorm_fn, [sv["xin"], sv["a1"]], [seg(2)], [prm["ln1_w"], prm["ln1_b"]], [dx1], [True, True], tm, nct)
    dy = mm(da1, wts["w_o"], "nt", name=f"proj_out_dx_{i}")
    dw_o = mm(sv["y"], da1, "tn", name=f"proj_out_dw_{i}")
    dy_ret, dy_dn, dao = dy[:, :512], dy[:, 512:1024], dy[:, 1024:]
    daq, delta = attn_bwd_dq(sv["aq"], sv["ak"], sv["av"], sv["ao"], dao, sv["lse"], lc, f"attn_dq_{i}")
    dak, dav = attn_bwd_dkv(sv["aq"], sv["ak"], sv["av"], dao, sv["lse"], delta, lc, f"attn_dkv_{i}")
    (dp_aq, dp_ak), _, (dqn_w, dkn_w) = rowwise_bwd(
        f"att_prep_b_{i}", att_prep_fn, [(p, 1024, 4), (p, 256, 20), cos_t, sin_t], [], [prm["qn_w"], prm["kn_w"]],
        [daq, dak], [True, True, False, False], tm, nct)
    (dd_o, dp_z), _, (ddn_norm_w,) = rowwise_bwd(
        f"dn_out_b_{i}", dn_out_fn, [sv["d_of"], sv["d_ob"], (p, 512, 7)], [], [prm["dn_norm_w"]], [dy_dn], [True, False, True], tm, nct)
    dqf, dkf, dvf, dqb, dkb, dvb, dgbf, dgbb = scan_bwd(
        dn_chunk, DN_CHUNK, sv["dq"], sv["dk"], sv["dv"], sv["gb"], sv["d_st"], dd_o, DN_SELS, lc, f"dn_scan_b_{i}")
    ddv = ew_sum([dvf, dvb], name=f"dn_dv_{i}")
    (dqc, dkc, dp_ab), _, (da_log, ddt_b) = rowwise_bwd(
        f"dn_prep_b_{i}", dn_prep_fn, [(sv["qkvc"], 512, 0), (sv["qkvc"], 512, 1), (p, LANES, 44)], [],
        [prm["a_log"], prm["dt_b"]], [[dqf, dqb], [dkf, dkb], [dgbf, dgbb]], [True, True, True], tm, nct)
    dqkvc = jnp.concatenate([dqc, dkc, ddv], 1)
    dp_qkv, dconv_w = conv_bwd(p, 2048, 3 * 512, prm["conv_w"], dqkvc, lc, f"dn_conv_b_{i}")
    (dr_o, dp_g), _, _ = rowwise_bwd(
        f"ret_out_b_{i}", ret_out_fn, [sv["r_of"], sv["r_ob"], (p, 512, 3)], [], [], [dy_ret], [True, False, True], tm, nct)
    drqf, drkf, drvf, drqb, drkb, drvb, dlogit = scan_bwd(
        ret_chunk, RET_CHUNK, sv["rq"], sv["rk"], sv["rv"], prm["ret_logit"], sv["r_st"], dr_o, RET_SELS, lc, f"ret_scan_b_{i}")
    drv = ew_sum([drvf, drvb], name=f"ret_dv_{i}")
    dret_logit = dlogit[0, :2 * RET_HEADS].reshape(2, RET_HEADS)
    (dp_rq, dp_rk), _, _ = rowwise_bwd(
        f"ret_prep_b_{i}", ret_prep_fn, [(p, 512, 0), (p, 512, 1), cos_t, sin_t], [], [], [[drqf, drqb], [drkf, drkb]],
        [True, True, False, False], tm, nct)
    dp = jnp.concatenate([dp_rq, dp_rk, drv, dp_g, dp_qkv, dp_z, dp_aq, dp_ak, dav, dp_ab], 1)
    dh1 = mm(dp, wts["w_in"], "nt", name=f"proj_in_dx_{i}")
    dw_in = mm(sv["h1"], dp, "tn", name=f"proj_in_dw_{i}")
    (dxb,), (dshift1, dscale1), _ = rowwise_bwd(
        f"mod1_b_{i}", modulate_fn, [sv["xin"]], [seg(0), seg(1)], [], [dh1], [True], tm, nct)
    dxin = ew_sum([dxa, dxb], name=f"dxin_{i}")
    dmods = jnp.concatenate([dshift1, dscale1, dgate1, dshift2, dscale2, dgate2], 1)
    big = dict(w_in=dw_in, w_o=dw_o, w_ffn_in=dw_ffn_in, w_ffn_out=dw_ffn_out)
    small = dict(ln1_w=both(dln1w)[0], ln1_b=both(dln1b)[0], ln2_w=both(dln2w)[0], ln2_b=both(dln2b)[0],
                 dn_norm_w=both(ddn_norm_w)[0], att_qn_w=both(dqn_w)[0], att_kn_w=both(dkn_w)[0],
                 dn_conv_w=dconv_w[:DN_CONV_K], ret_decay_logit=dret_logit,
                 dn_a_log=both(da_log)[0, :2 * DN_HEADS].reshape(2, DN_HEADS),
                 dn_dt_bias=both(ddt_b)[0, :2 * DN_HEADS].reshape(2, DN_HEADS))
    return dxin, dmods, big, small


BIG = ("w_in", "w_o", "w_ffn_in", "w_ffn_out")
SMALL = ("c_ctx", "b_ada", "ret_decay_logit", "dn_conv_w", "dn_a_log", "dn_dt_bias", "dn_norm_w", "att_qn_w", "att_kn_w",
         "ln1_w", "ln1_b", "ln2_w", "ln2_b")
WEIGHTS = ("c_ctx", "w_ada", "b_ada", "w_in", "ret_decay_logit", "dn_conv_w", "dn_a_log", "dn_dt_bias", "dn_norm_w",
           "att_qn_w", "att_kn_w", "w_o", "ln1_w", "ln1_b", "w_ffn_in", "w_ffn_out", "ln2_w", "ln2_b")


def _chip_major(g, name):
    if name in ("w_in", "w_ffn_in"):
        r, cols = g.shape
        return g.reshape(r, 4, cols // 4).transpose(1, 0, 2)
    return g.reshape(4, g.shape[0] // 4, g.shape[1])


def kernel(x, c, ctx, c_ctx, w_ada, b_ada, w_in, ret_decay_logit, dn_conv_w, dn_a_log, dn_dt_bias, dn_norm_w, att_qn_w, att_kn_w, w_o, ln1_w, ln1_b, w_ffn_in, w_ffn_out, ln2_w, ln2_b, loss_target, m_c_ctx, m_w_ada, m_b_ada, m_w_in, m_ret_decay_logit, m_dn_conv_w, m_dn_a_log, m_dn_dt_bias, m_dn_norm_w, m_att_qn_w, m_att_kn_w, m_w_o, m_ln1_w, m_ln1_b, m_w_ffn_in, m_w_ffn_out, m_ln2_w, m_ln2_b, v_c_ctx, v_w_ada, v_b_ada, v_w_in, v_ret_decay_logit, v_dn_conv_w, v_dn_a_log, v_dn_dt_bias, v_dn_norm_w, v_att_qn_w, v_att_kn_w, v_w_o, v_ln1_w, v_ln1_b, v_w_ffn_in, v_w_ffn_out, v_ln2_w, v_ln2_b):
    wv = dict(c_ctx=c_ctx, w_ada=w_ada, b_ada=b_ada, w_in=w_in, ret_decay_logit=ret_decay_logit, dn_conv_w=dn_conv_w,
              dn_a_log=dn_a_log, dn_dt_bias=dn_dt_bias, dn_norm_w=dn_norm_w, att_qn_w=att_qn_w, att_kn_w=att_kn_w, w_o=w_o,
              ln1_w=ln1_w, ln1_b=ln1_b, w_ffn_in=w_ffn_in, w_ffn_out=w_ffn_out, ln2_w=ln2_w, ln2_b=ln2_b)
    mv = dict(c_ctx=m_c_ctx, w_ada=m_w_ada, b_ada=m_b_ada, w_in=m_w_in, ret_decay_logit=m_ret_decay_logit,
              dn_conv_w=m_dn_conv_w, dn_a_log=m_dn_a_log, dn_dt_bias=m_dn_dt_bias, dn_norm_w=m_dn_norm_w,
              att_qn_w=m_att_qn_w, att_kn_w=m_att_kn_w, w_o=m_w_o, ln1_w=m_ln1_w, ln1_b=m_ln1_b, w_ffn_in=m_w_ffn_in,
              w_ffn_out=m_w_ffn_out, ln2_w=m_ln2_w, ln2_b=m_ln2_b)
    vv = dict(c_ctx=v_c_ctx, w_ada=v_w_ada, b_ada=v_b_ada, w_in=v_w_in, ret_decay_logit=v_ret_decay_logit,
              dn_conv_w=v_dn_conv_w, dn_a_log=v_dn_a_log, dn_dt_bias=v_dn_dt_bias, dn_norm_w=v_dn_norm_w,
              att_qn_w=v_att_qn_w, att_kn_w=v_att_kn_w, w_o=v_w_o, ln1_w=v_ln1_w, ln1_b=v_ln1_b, w_ffn_in=v_w_ffn_in,
              w_ffn_out=v_w_ffn_out, ln2_w=v_ln2_w, ln2_b=v_ln2_b)
    depth = w_in.shape[0]
    n_lat, lc = x.shape[1], ctx.shape[1]
    t = lc + n_lat
    tm = _pick(lc, (256, 128))
    xi, yi, ci = _place()
    bidx = 4 * xi + 2 * yi + ci
    chip = 2 * xi + yi
    ada_w = w_ada.shape[2]
    conv_sh = dn_conv_w.shape[2]

    slab0, offs0 = _pack([c, dn_conv_w])
    g0 = allgather8(slab0, "gather_cond").reshape(8, -1, LANES)
    c_all = jnp.concatenate([_unpack(g0[d], offs0)[0] for d in range(8)], 0)
    conv_full = jnp.concatenate([_unpack(g0[2 * k], offs0)[1] for k in range(4)], 2)
    c_raw = jnp.concatenate([c_all, c_ctx[None], jnp.zeros((LANES - 9, D_MODEL), F32)], 0)
    (cond,) = rowwise("cond_silu", lambda a: (_silu(a),), [c_raw], [], [], [(D_MODEL, F32)], LANES, 0)
    b_sh = lax.dynamic_slice(b_ada, (0, chip * ada_w), (depth, ada_w))
    mods_sh = []
    for i in range(depth):
        mi = mm(cond, w_ada[i], "nn", name=f"ada_{i}")
        (mi,) = rowwise(f"ada_bias_{i}", lambda a, b: (a + b,), [mi], [], [b_sh[i:i + 1]], [(ada_w, F32)], LANES, 0)
        mods_sh.append(mi[:16])
    slab1, offs1 = _pack([jnp.stack(mods_sh)])
    g1 = allgather8(slab1, "gather_mods").reshape(8, -1, LANES)
    mods_all = jnp.concatenate([_unpack(g1[2 * k], offs1)[0] for k in range(4)], 2)
    mod_lat = lax.dynamic_index_in_dim(mods_all, bidx, 1, keepdims=False)
    mod_ctx = mods_all[:, 8]
    mods = jnp.stack([mod_ctx, mod_lat], 1).reshape(depth, 2, 6, D_MODEL)

    wbf = {n: ew_sum([wv[n].reshape(-1, wv[n].shape[2])], name=f"cast_{n}", out_dtype=MXU_DTYPE).reshape(wv[n].shape) for n in BIG}
    layers_w = []
    for i in range(depth):
        g = gather_chips([wbf[n][i] for n in BIG], f"gather_w_{i}")
        gw = dict(zip(BIG, g))
        layers_w.append(dict(
            w_in=_permute_w_in(gw["w_in"].transpose(1, 0, 2).reshape(D_MODEL, PROJ_W)),
            w_o=gw["w_o"].reshape(D_MODEL, D_MODEL),
            w_ffn_in=gw["w_ffn_in"].transpose(1, 0, 2).reshape(D_MODEL, 2 * D_FF),
            w_ffn_out=gw["w_ffn_out"].reshape(D_FF, D_MODEL)))

    tabs = _rope_tables(n_lat, lc)
    prms = []
    for i in range(depth):
        prms.append(dict(
            ret_logit=_slab(ret_decay_logit[i].reshape(-1)), conv_w=conv_full[i],
            a_log=_slab(dn_a_log[i].reshape(-1)), dt_b=_slab(dn_dt_bias[i].reshape(-1)), dn_norm_w=dn_norm_w[i:i + 1],
            qn_w=att_qn_w[i:i + 1], kn_w=att_kn_w[i:i + 1], ln1_w=ln1_w[i:i + 1], ln1_b=ln1_b[i:i + 1],
            ln2_w=ln2_w[i:i + 1], ln2_b=ln2_b[i:i + 1]))

    rows = jnp.concatenate([ctx[0], x[0]], 0)
    saved = []
    for i in range(depth):
        rows, sv = _layer_fwd(rows, mods[i], layers_w[i], prms[i], tabs, lc, tm, i)
        saved.append(sv)
    loss_local, dy = loss_and_grad(rows[lc:], loss_target[0], tm, "loss")
    loss = lax.psum(loss_local[0, 0], ("x", "y", "c"))

    drows = jnp.concatenate([jnp.zeros((lc, D_MODEL), F32), dy], 0)
    dmods, big_g, small_g = [None] * depth, [None] * depth, [None] * depth
    for i in reversed(range(depth)):
        drows, dmods[i], big, small_g[i] = _layer_bwd(drows, saved[i], mods[i], layers_w[i], prms[i], tabs, lc, tm, i)
        big["w_in"] = _unpermute_dw_in(big["w_in"])
        big_g[i] = dict(zip(BIG, reduce_scatter([_chip_major(big[n], n) for n in BIG], str(i))))
    grad_x = drows[lc:][None]

    names = ("ln1_w", "ln1_b", "ln2_w", "ln2_b", "dn_norm_w", "att_qn_w", "att_kn_w", "dn_conv_w", "ret_decay_logit",
             "dn_a_log", "dn_dt_bias")
    loc = [jnp.stack(dmods)] + [jnp.stack([small_g[i][n] for i in range(depth)]) for n in names]
    slab2, offs2 = _pack(loc)
    g2 = allgather8(slab2, "gather_small").reshape(8, -1, LANES)
    tot = _unpack(sum_entries(g2, tuple(range(8)), "sum_small"), offs2)
    dmods_sum = tot[0]
    gsm = dict(zip(names, tot[1:]))
    dmod_lat = jnp.stack([_unpack(g2[d], offs2)[0][:, 1].reshape(depth, 6 * D_MODEL) for d in range(8)], 1)
    dmod_ctx = dmods_sum[:, 0].reshape(depth, 1, 6 * D_MODEL)
    dm = jnp.concatenate([dmod_lat, dmod_ctx, jnp.zeros((depth, LANES - 9, 6 * D_MODEL), F32)], 1)
    dm_sh = lax.dynamic_slice(dm, (0, 0, chip * ada_w), (depth, LANES, ada_w))
    g_w_ada = jnp.stack([mm(cond, dm_sh[i], "tn", name=f"ada_dw_{i}") for i in range(depth)])
    gsm["b_ada"] = ew_sum([dmods_sum[:, 0].reshape(-1, LANES), dmods_sum[:, 1].reshape(-1, LANES)], name="b_ada_sum").reshape(b_ada.shape)
    dctx_rows = jnp.concatenate([dm_sh[:, 8:9], jnp.zeros((depth, 15, ada_w), F32)], 1)
    part = ew_sum([mm(dctx_rows[i], w_ada[i], "nt", name=f"ada_dcond_{i}") for i in range(depth)], name="ada_dcond_sum")[0]
    slab3, offs3 = _pack([part])
    g3 = allgather8(slab3, "gather_dcond").reshape(8, -1, LANES)
    dcond_ctx = _unpack(sum_entries(g3, (0, 2, 4, 6), "sum_dcond"), offs3)[0]
    (dc_ctx,), _, _ = rowwise_bwd("c_ctx_silu_b", lambda a: (_silu(a),), [c_ctx.reshape(16, LANES)], [], [],
                                  [dcond_ctx.reshape(16, LANES)], [True], 16, 0)
    gsm["c_ctx"] = dc_ctx.reshape(c_ctx.shape)
    gsm["dn_conv_w"] = lax.dynamic_slice(gsm["dn_conv_w"], (0, 0, chip * conv_sh), (depth, DN_CONV_K, conv_sh))

    grads, delta, new_m, new_v = {}, {}, {}, {}
    gs, offs = _pack([gsm[n] for n in SMALL])
    ws, _ = _pack([wv[n] for n in SMALL])
    ms, _ = _pack([mv[n] for n in SMALL])
    vs, _ = _pack([vv[n] for n in SMALL])
    res = [_unpack(o, offs) for o in adamw(gs, ws, ms, vs, "adamw_small")]
    for j, n in enumerate(SMALL):
        grads[n], delta[n], new_m[n], new_v[n] = gsm[n], res[0][j], res[1][j], res[2][j]
    bigs = {n: jnp.stack([big_g[i][n] for i in range(depth)]) for n in BIG}
    bigs["w_ada"] = g_w_ada
    for n, g in bigs.items():
        shp = wv[n].shape
        flat = lambda a: a.reshape(-1, shp[2])
        d_, m_, v_ = adamw(flat(g), flat(wv[n]), flat(mv[n]), flat(vv[n]), f"adamw_{n}")
        grads[n], delta[n], new_m[n], new_v[n] = g.reshape(shp), d_.reshape(shp), m_.reshape(shp), v_.reshape(shp)
    return (loss, grad_x, *[grads[n] for n in WEIGHTS], *[delta[n] for n in WEIGHTS], *[new_m[n] for n in WEIGHTS],
            *[new_v[n] for n in WEIGHTS])
```

```python
import functools

import jax
import jax.numpy as jnp
from jax import lax
from jax.experimental import pallas as pl
from jax.experimental.pallas import tpu as pltpu

F32 = jnp.float32
MXU_DTYPE = jnp.bfloat16

D_MODEL = 2048
DEPTH = 4
GRID_W = 64
HEAD_DIM = 128
RET_HEADS = 4
DN_HEADS = 4
ATT_HEADS = 8
ATT_KV_HEADS = 2
RET_CHUNK = 128
DN_CHUNK = 64
DN_CONV_K = 5
ROPE_THETA = 10000.0
D_FF = 5632
PROJ_W = 5648
PROJ_PAD = 5760
DEEPNORM_ALPHA = (2 * DEPTH) ** 0.25
EPS = 1e-6
QK_SCALE = HEAD_DIM ** -0.5
ADAM_LR = 0.001
ADAM_B1 = 0.9
ADAM_B2 = 0.999
ADAM_EPS = 1e-08
ADAM_WD = 0.01
ADAM_STEP = 10
LANES = 128
MM_VMEM_BUDGET = 44 * 1024 * 1024
MESH_ID = pl.DeviceIdType.MESH
NEG_BIG = -1e30


def _pick(n, prefs):
    for p in prefs:
        if n % p == 0:
            return p
    return n


def _contract(a, b, ca, cb, hp):
    dims = (((ca,), (cb,)), ((), ()))
    if hp and MXU_DTYPE == F32:
        return lax.dot_general(a, b, dims, precision=lax.Precision.HIGHEST, preferred_element_type=F32)
    if hp:
        ah, bh = a.astype(MXU_DTYPE), b.astype(MXU_DTYPE)
        al = (a - ah.astype(F32)).astype(MXU_DTYPE)
        bl = (b - bh.astype(F32)).astype(MXU_DTYPE)
        dot = lambda u, v: lax.dot_general(u, v, dims, preferred_element_type=F32)
        return dot(ah, bh) + (dot(ah, bl) + dot(al, bh))
    return lax.dot_general(a.astype(MXU_DTYPE), b.astype(MXU_DTYPE), dims, preferred_element_type=F32)


@functools.partial(jax.custom_vjp, nondiff_argnums=(2, 3, 4))
def mdot(a, b, ca, cb, hp=False):
    return _contract(a, b, ca, cb, hp)


def _mdot_fwd(a, b, ca, cb, hp):
    return _contract(a, b, ca, cb, hp), (a, b)


def _mdot_bwd(ca, cb, hp, res, g):
    a, b = res
    da = mdot(g, b, 1, 1 - cb, hp) if ca == 1 else mdot(b, g, 1 - cb, 1, hp)
    db = mdot(a, g, 1 - ca, 0, hp) if cb == 0 else mdot(g, a, 0, 1 - ca, hp)
    return da, db


mdot.defvjp(_mdot_fwd, _mdot_bwd)


@jax.custom_vjp
def swap_halves(x):
    return pltpu.roll(x, HEAD_DIM // 2, 1)


def _swap_fwd(x):
    return swap_halves(x), None


def _swap_bwd(_, g):
    return (swap_halves(g),)


swap_halves.defvjp(_swap_fwd, _swap_bwd)


@jax.custom_vjp
def tri_inv(a):
    n = a.shape[0]
    r = lax.broadcasted_iota(jnp.int32, (n, n), 0)
    c = lax.broadcasted_iota(jnp.int32, (n, n), 1)
    eye = (r == c).astype(F32)
    p = -a
    t = eye + p
    k = 2
    while k < n:
        p = _contract(p, p, 1, 0, True)
        t = t + _contract(t, p, 1, 0, True)
        k *= 2
    return t


def _tri_inv_fwd(a):
    t = tri_inv(a)
    return t, t


def _tri_inv_bwd(t, g):
    return (-_contract(_contract(t, g, 0, 0, True), t, 1, 1, True),)


tri_inv.defvjp(_tri_inv_fwd, _tri_inv_bwd)


def _sigmoid(x):
    return 1.0 / (1.0 + jnp.exp(-x))


def _silu(x):
    return x * _sigmoid(x)


def _softplus(x):
    return jnp.maximum(x, 0.0) + jnp.log(1.0 + jnp.exp(-jnp.abs(x)))


def _lane_pick(slab, idx):
    lane = lax.broadcasted_iota(jnp.int32, slab.shape, 1)
    return jnp.sum(jnp.where(lane == idx, slab, 0.0), axis=1, keepdims=True)


def _heads(x):
    return [x[:, h * HEAD_DIM:(h + 1) * HEAD_DIM] for h in range(x.shape[1] // HEAD_DIM)]


def _rope(x, cos, sin):
    return x * cos + swap_halves(x) * sin


def _rms(x):
    return x * lax.rsqrt(jnp.mean(x * x, -1, keepdims=True) + EPS)


def _l2n(x):
    return x * lax.rsqrt(jnp.sum(x * x, -1, keepdims=True) + EPS)


def mm(a, b, mode, out_dtype=F32, name="mm"):
    if mode == "nn":
        (m, k), (k2, n) = a.shape, b.shape
        ca, cb = 1, 0
    elif mode == "nt":
        (m, k), (n, k2) = a.shape, b.shape
        ca, cb = 1, 1
    else:
        (k, m), (k2, n) = a.shape, b.shape
        ca, cb = 0, 0
    assert k == k2, (a.shape, b.shape, mode)
    tm = _pick(m, (768, 512, 384, 256, 128))
    tn = _pick(n, (1024, 768, 640, 512, 384, 256, 128))
    sa, sb, so = a.dtype.itemsize, b.dtype.itemsize, jnp.dtype(out_dtype).itemsize
    for tk in [k] + [d for d in (4224, 2816, 2048, 1920, 1408, 1152, 1024, 768, 640, 512, 384, 256, 128) if d < k and k % d == 0]:
        if 2 * (tm * tk * sa + tk * tn * sb) + 2 * tm * tn * so + (tm * tn * 4 if tk < k else 0) <= MM_VMEM_BUDGET:
            break
    nk = k // tk
    a_spec = pl.BlockSpec((tm, tk), lambda i, j, q: (i, q)) if ca == 1 else pl.BlockSpec((tk, tm), lambda i, j, q: (q, i))
    b_spec = pl.BlockSpec((tk, tn), lambda i, j, q: (q, j)) if cb == 0 else pl.BlockSpec((tn, tk), lambda i, j, q: (j, q))

    def body(a_ref, b_ref, o_ref, *scratch):
        d = _contract(a_ref[...], b_ref[...], ca, cb, False)
        if nk == 1:
            o_ref[...] = d.astype(o_ref.dtype)
            return
        acc = scratch[0]
        q = pl.program_id(2)

        @pl.when(q == 0)
        def _():
            acc[...] = d

        @pl.when(jnp.logical_and(q > 0, q < nk - 1))
        def _():
            acc[...] += d

        @pl.when(q == nk - 1)
        def _():
            o_ref[...] = (acc[...] + d).astype(o_ref.dtype)

    return pl.pallas_call(
        body, name=name, grid=(m // tm, n // tn, nk),
        in_specs=[a_spec, b_spec], out_specs=pl.BlockSpec((tm, tn), lambda i, j, q: (i, j)),
        out_shape=jax.ShapeDtypeStruct((m, n), out_dtype),
        scratch_shapes=[pltpu.VMEM((tm, tn), F32)] if nk > 1 else [],
        compiler_params=pltpu.CompilerParams(dimension_semantics=("parallel", "parallel", "arbitrary")),
    )(a, b)


def _row_spec(r, tm):
    if isinstance(r, tuple):
        arr, width, blk = r
        return arr, pl.BlockSpec((tm, width), lambda i, blk=blk: (i, blk))
    return r, pl.BlockSpec((tm, r.shape[1]), lambda i: (i, 0))


def _seg_spec(s, nct):
    return pl.BlockSpec((1, 1, s.shape[2]), lambda i: (jnp.where(i < nct, 0, 1), 0, 0))


def _full_spec(s):
    return pl.BlockSpec(s.shape, lambda i: (0,) * s.ndim)


def rowwise(name, fn, rows, segs, shared, outs, tm, nct):
    arrs, specs = zip(*[_row_spec(r, tm) for r in rows])
    t = arrs[0].shape[0]
    nr, ns = len(rows), len(segs)

    def body(*refs):
        vals = [r[...] for r in refs[:nr]] + [r[0] for r in refs[nr:nr + ns]] + [r[...] for r in refs[nr + ns:nr + ns + len(shared)]]
        res = fn(*vals)
        for o_ref, v in zip(refs[nr + ns + len(shared):], res):
            o_ref[...] = v.astype(o_ref.dtype)

    res = pl.pallas_call(
        body, name=name, grid=(t // tm,),
        in_specs=list(specs) + [_seg_spec(s, nct) for s in segs] + [_full_spec(s) for s in shared],
        out_specs=[pl.BlockSpec((tm, w), lambda i: (i, 0)) for w, _ in outs],
        out_shape=[jax.ShapeDtypeStruct((t, w), d) for w, d in outs],
        compiler_params=pltpu.CompilerParams(dimension_semantics=("parallel",)),
    )(*arrs, *segs, *shared)
    return res


def rowwise_bwd(name, fn, rows, segs, shared, cts, want, tm, nct):
    arrs, specs = zip(*[_row_spec(r, tm) for r in rows])
    t = arrs[0].shape[0]
    groups = [list(c) if isinstance(c, (list, tuple)) else [c] for c in cts]
    cts = [c for g in groups for c in g]
    nr, ns, nsh, nc = len(rows), len(segs), len(shared), len(cts)
    widths = [sp.block_shape[1] for sp in specs]
    wanted = [i for i in range(nr) if want[i]]

    def body(*refs):
        i = pl.program_id(0)
        ins = refs[:nr + ns + nsh]
        ct_refs = refs[nr + ns + nsh:nr + ns + nsh + nc]
        o_refs = refs[nr + ns + nsh + nc:]
        vals = [r[...] for r in ins[:nr]] + [r[0] for r in ins[nr:nr + ns]] + [r[...] for r in ins[nr + ns:]]
        _, vjp = jax.vjp(fn, *vals)
        ct_vals, pos = [], 0
        for g in groups:
            tot = ct_refs[pos][...]
            for c_ref in ct_refs[pos + 1:pos + len(g)]:
                tot = tot + c_ref[...]
            ct_vals.append(tot)
            pos += len(g)
        grads = vjp(tuple(ct_vals))
        for o_ref, idx in zip(o_refs[:len(wanted)], wanted):
            o_ref[...] = grads[idx].astype(o_ref.dtype)
        first = jnp.logical_or(i == 0, i == nct)
        for o_ref, g in zip(o_refs[len(wanted):], grads[nr:]):
            @pl.when(first)
            def _(o_ref=o_ref):
                o_ref[...] = jnp.zeros_like(o_ref)

            o_ref[0] += g

    seg_idx = lambda i: jnp.where(i < nct, 0, 1)
    out_specs = [pl.BlockSpec((tm, widths[idx]), lambda i: (i, 0)) for idx in wanted]
    out_shape = [jax.ShapeDtypeStruct((t, widths[idx]), F32) for idx in wanted]
    for s in segs:
        out_specs.append(pl.BlockSpec((1, 1, s.shape[2]), lambda i: (seg_idx(i), 0, 0)))
        out_shape.append(jax.ShapeDtypeStruct((2, 1, s.shape[2]), F32))
    for s in shared:
        out_specs.append(pl.BlockSpec((1,) + s.shape, lambda i, nd=s.ndim: (seg_idx(i),) + (0,) * nd))
        out_shape.append(jax.ShapeDtypeStruct((2,) + s.shape, F32))
    res = pl.pallas_call(
        body, name=name, grid=(t // tm,),
        in_specs=list(specs) + [_seg_spec(s, nct) for s in segs] + [_full_spec(s) for s in shared]
        + [pl.BlockSpec((tm, c.shape[1]), lambda i: (i, 0)) for c in cts],
        out_specs=out_specs, out_shape=out_shape,
        compiler_params=pltpu.CompilerParams(dimension_semantics=("arbitrary",)),
    )(*arrs, *segs, *shared, *cts)
    nw = len(wanted)
    return res[:nw], res[nw:nw + ns], res[nw + ns:]


def modulate_fn(x, shift, scale):
    return (x * (1.0 + scale) + shift,)


def postnorm_fn(x, a, gate, w, b):
    z = DEEPNORM_ALPHA * x + gate * a
    mu = jnp.mean(z, -1, keepdims=True)
    zc = z - mu
    var = jnp.mean(zc * zc, -1, keepdims=True)
    return (zc * lax.rsqrt(var + EPS) * w + b,)


def ret_prep_fn(q, k, cos, sin):
    qs = [_rope(x, cos, sin) for x in _heads(q)]
    ks = [_rope(x, cos, sin) * QK_SCALE for x in _heads(k)]
    return jnp.concatenate(qs, 1), jnp.concatenate(ks, 1)


def ret_out_fn(of, ob, gate):
    ys = [_rms(o) * _silu(g) for o, g in zip(_heads(of + ob), _heads(gate))]
    return (jnp.concatenate(ys, 1),)


def dn_prep_fn(q, k, ab, a_log, dt_b):
    qs = [_l2n(x) * QK_SCALE for x in _heads(q)]
    ks = [_l2n(x) for x in _heads(k)]
    lane = lax.broadcasted_iota(jnp.int32, ab.shape, 1)
    g = -jnp.exp(a_log) * _softplus(ab + dt_b)
    beta = _sigmoid(ab)
    gb = jnp.where(lane < 2 * DN_HEADS, g, jnp.where(lane < 4 * DN_HEADS, beta, 0.0))
    return jnp.concatenate(qs, 1), jnp.concatenate(ks, 1), gb


def dn_out_fn(of, ob, z, w):
    ys = [_rms(o) * w * _silu(g) for o, g in zip(_heads(of + ob), _heads(z))]
    return (jnp.concatenate(ys, 1),)


def att_prep_fn(q, k, cos, sin, qw, kw):
    qs = [_rope(_rms(x) * qw, cos, sin) for x in _heads(q)]
    ks = [_rope(_rms(x) * kw, cos, sin) for x in _heads(k)]
    return jnp.concatenate(qs, 1), jnp.concatenate(ks, 1)


def swiglu_fwd(u, tm, name):
    t, w2 = u.shape
    w = w2 // 2
    cw = _pick(w, (512, 256, 128))
    ncb = w // cw

    def body(g_ref, u_ref, o_ref):
        o_ref[...] = (_silu(g_ref[...]) * u_ref[...]).astype(o_ref.dtype)

    return pl.pallas_call(
        body, name=name, grid=(t // tm, ncb),
        in_specs=[pl.BlockSpec((tm, cw), lambda i, j: (i, j)), pl.BlockSpec((tm, cw), lambda i, j: (i, j + ncb))],
        out_specs=pl.BlockSpec((tm, cw), lambda i, j: (i, j)),
        out_shape=jax.ShapeDtypeStruct((t, w), MXU_DTYPE),
        compiler_params=pltpu.CompilerParams(dimension_semantics=("parallel", "parallel")),
    )(u, u)


def swiglu_bwd(u, dact, tm, name):
    t, w2 = u.shape
    w = w2 // 2
    cw = _pick(w, (512, 256, 128))
    ncb = w // cw

    def body(g_ref, u_ref, d_ref, o_ref):
        j = pl.program_id(1)
        g = g_ref[...]
        s = _sigmoid(g)
        d = d_ref[...]
        dg = d * u_ref[...] * (s * (1.0 + g * (1.0 - s)))
        du = d * g * s
        o_ref[...] = jnp.where(j < ncb, dg, du).astype(o_ref.dtype)

    return pl.pallas_call(
        body, name=name, grid=(t // tm, 2 * ncb),
        in_specs=[pl.BlockSpec((tm, cw), lambda i, j: (i, j % ncb)), pl.BlockSpec((tm, cw), lambda i, j: (i, j % ncb + ncb)),
                  pl.BlockSpec((tm, cw), lambda i, j: (i, j % ncb))],
        out_specs=pl.BlockSpec((tm, cw), lambda i, j: (i, j)),
        out_shape=jax.ShapeDtypeStruct((t, w2), MXU_DTYPE),
        compiler_params=pltpu.CompilerParams(dimension_semantics=("parallel", "parallel")),
    )(u, u, dact)


CONV_HALO = 8


def _conv_mask(t, lc, s):
    r = lax.broadcasted_iota(jnp.int32, (t, 1), 0)
    src = r + s
    return jnp.logical_and(jnp.logical_and(src >= 0, src < t), (r < lc) == (src < lc))


def _conv_taps(pad_ref, w_ref, t, lc, flip):
    acc = None
    for k in range(DN_CONV_K):
        s = k - DN_CONV_K // 2
        off = -s if flip else s
        tap = pad_ref[pl.ds(CONV_HALO + off, t), :]
        term = jnp.where(_conv_mask(t, lc, off), tap, 0.0) * w_ref[k:k + 1, :]
        acc = term if acc is None else acc + term
    return acc


def _fill_pad(pad_ref, val, t):
    pad_ref[pl.ds(0, CONV_HALO), :] = jnp.zeros((CONV_HALO, LANES), F32)
    pad_ref[pl.ds(CONV_HALO + t, CONV_HALO), :] = jnp.zeros((CONV_HALO, LANES), F32)
    pad_ref[pl.ds(CONV_HALO, t), :] = val


def conv_fwd(p, col0, width, w, lc, name):
    t = p.shape[0]
    b0 = col0 // LANES

    def body(x_ref, w_ref, o_ref, pad):
        _fill_pad(pad, x_ref[...], t)
        o_ref[...] = _silu(_conv_taps(pad, w_ref, t, lc, False))

    return pl.pallas_call(
        body, name=name, grid=(width // LANES,),
        in_specs=[pl.BlockSpec((t, LANES), lambda j: (0, j + b0)), pl.BlockSpec((DN_CONV_K, LANES), lambda j: (0, j))],
        out_specs=pl.BlockSpec((t, LANES), lambda j: (0, j)),
        out_shape=jax.ShapeDtypeStruct((t, width), F32),
        scratch_shapes=[pltpu.VMEM((t + 2 * CONV_HALO, LANES), F32)],
        compiler_params=pltpu.CompilerParams(dimension_semantics=("parallel",)),
    )(p, w)


def conv_bwd(p, col0, width, w, dout, lc, name):
    t = p.shape[0]
    b0 = col0 // LANES

    def body(x_ref, w_ref, d_ref, dx_ref, dw_ref, pad):
        _fill_pad(pad, x_ref[...], t)
        y = _conv_taps(pad, w_ref, t, lc, False)
        sg = _sigmoid(y)
        dy = d_ref[...] * (sg * (1.0 + y * (1.0 - sg)))
        krow = lax.broadcasted_iota(jnp.int32, (8, LANES), 0)
        dw = jnp.zeros((8, LANES), F32)
        for k in range(DN_CONV_K):
            s = k - DN_CONV_K // 2
            tap = pad[pl.ds(CONV_HALO + s, t), :]
            dw_k = jnp.sum(jnp.where(_conv_mask(t, lc, s), tap, 0.0) * dy, axis=0, keepdims=True)
            dw = dw + jnp.where(krow == k, dw_k, 0.0)
        dw_ref[...] = dw
        _fill_pad(pad, dy, t)
        dx_ref[...] = _conv_taps(pad, w_ref, t, lc, True)

    return pl.pallas_call(
        body, name=name, grid=(width // LANES,),
        in_specs=[pl.BlockSpec((t, LANES), lambda j: (0, j + b0)), pl.BlockSpec((DN_CONV_K, LANES), lambda j: (0, j)),
                  pl.BlockSpec((t, LANES), lambda j: (0, j))],
        out_specs=[pl.BlockSpec((t, LANES), lambda j: (0, j)), pl.BlockSpec((8, LANES), lambda j: (0, j))],
        out_shape=[jax.ShapeDtypeStruct((t, width), F32), jax.ShapeDtypeStruct((8, width), F32)],
        scratch_shapes=[pltpu.VMEM((t + 2 * CONV_HALO, LANES), F32)],
        compiler_params=pltpu.CompilerParams(dimension_semantics=("parallel",)),
    )(p, w, dout)


def ret_chunk(q, k, v, s, logit_slab, h):
    c = q.shape[0]
    lg = -_softplus(-_lane_pick(logit_slab, h))
    i = lax.broadcasted_iota(jnp.int32, (c, c), 0)
    j = lax.broadcasted_iota(jnp.int32, (c, c), 1)
    rel = (i - j).astype(F32)
    decay = jnp.where(i >= j, jnp.exp(jnp.maximum(rel, 0.0) * lg), 0.0)
    pos = lax.broadcasted_iota(jnp.int32, (c, 1), 0).astype(F32)
    q_decay = jnp.exp((pos + 1.0) * lg)
    k_decay = jnp.exp((c - 1.0 - pos) * lg)
    intra = mdot(q, k, 1, 1) * decay
    o = mdot(intra, v, 1, 0) + mdot(q * q_decay, s, 1, 0)
    s_new = s * jnp.exp(c * lg) + mdot(k * k_decay, v, 0, 0)
    return o, s_new


def dn_chunk(q, k, v, s, gb, h):
    c = q.shape[0]
    g = _lane_pick(gb, h[0])
    beta = _lane_pick(gb, h[1])
    i = lax.broadcasted_iota(jnp.int32, (c, c), 0)
    j = lax.broadcasted_iota(jnp.int32, (c, c), 1)
    tri = i >= j
    gc = _lane_pick(mdot(tri.astype(F32), gb, 1, 0, True), h[0])
    gc_row = jnp.sum(jnp.where(i == j, gc, 0.0), axis=0, keepdims=True)
    decay = jnp.where(tri, jnp.exp(jnp.where(tri, gc - gc_row, 0.0)), 0.0)
    kb = k * beta
    vb = v * beta
    a = jnp.where(i > j, mdot(kb, k, 1, 1) * decay, 0.0)
    t = tri_inv(a)
    e = jnp.exp(gc)
    g_last = jnp.sum(g, axis=0, keepdims=True)
    w_val = mdot(t, vb, 1, 0)
    k_cum = mdot(t, kb * e, 1, 0)
    qk = mdot(q, k, 1, 1) * decay
    v_new = w_val - mdot(k_cum, s, 1, 0)
    o = mdot(q * e, s, 1, 0) + mdot(qk, v_new, 1, 0)
    s_new = s * jnp.exp(g_last) + mdot(k * jnp.exp(g_last - gc), v_new, 0, 0)
    return o, s_new


def scan_fwd(chunk_fn, c, q, k, v, slab, hsel, name):
    t, wid = q.shape
    n, nheads = t // c, wid // HEAD_DIM
    per_token = slab.shape[0] != 1
    row = pl.BlockSpec((c, wid), lambda i: (i, 0))
    slab_spec = pl.BlockSpec((c, LANES), lambda i: (i, 0)) if per_token else _full_spec(slab)
    st_spec = pl.BlockSpec((1, nheads, HEAD_DIM, HEAD_DIM), lambda i: (i, 0, 0, 0))

    def body(q_ref, k_ref, v_ref, slab_ref, o_ref, st_ref, s_scr):
        @pl.when(pl.program_id(0) == 0)
        def _():
            s_scr[...] = jnp.zeros_like(s_scr)

        sb = slab_ref[...]
        for h in range(nheads):
            sl = slice(h * HEAD_DIM, (h + 1) * HEAD_DIM)
            s_h = s_scr[h]
            st_ref[0, h] = s_h
            o, s_new = chunk_fn(q_ref[:, sl], k_ref[:, sl], v_ref[:, sl], s_h, sb, hsel(h))
            o_ref[:, sl] = o
            s_scr[h] = s_new

    return pl.pallas_call(
        body, name=name, grid=(n,),
        in_specs=[row, row, row, slab_spec], out_specs=[row, st_spec],
        out_shape=[jax.ShapeDtypeStruct((t, wid), F32), jax.ShapeDtypeStruct((n, nheads, HEAD_DIM, HEAD_DIM), F32)],
        scratch_shapes=[pltpu.VMEM((nheads, HEAD_DIM, HEAD_DIM), F32)],
        compiler_params=pltpu.CompilerParams(dimension_semantics=("arbitrary",)),
    )(q, k, v, slab)


def scan_bwd(chunk_fn, c, q, k, v, slab, states, do, hsel, name):
    t, wid = q.shape
    n, nheads = t // c, wid // HEAD_DIM
    per_token = slab.shape[0] != 1
    row = pl.BlockSpec((c, wid), lambda i: (n - 1 - i, 0))
    slab_spec = pl.BlockSpec((c, LANES), lambda i: (n - 1 - i, 0)) if per_token else _full_spec(slab)
    st_spec = pl.BlockSpec((1, nheads, HEAD_DIM, HEAD_DIM), lambda i: (n - 1 - i, 0, 0, 0))

    def body(q_ref, k_ref, v_ref, slab_ref, st_ref, do_ref, dq_ref, dk_ref, dv_ref, dslab_ref, ds_scr):
        @pl.when(pl.program_id(0) == 0)
        def _():
            ds_scr[...] = jnp.zeros_like(ds_scr)
            if not per_token:
                dslab_ref[...] = jnp.zeros_like(dslab_ref)

        sb = slab_ref[...]
        dslab = None
        for h in range(nheads):
            sl = slice(h * HEAD_DIM, (h + 1) * HEAD_DIM)
            _, vjp = jax.vjp(lambda a, b, cc, s, z, h=h: chunk_fn(a, b, cc, s, z, hsel(h)),
                             q_ref[:, sl], k_ref[:, sl], v_ref[:, sl], st_ref[0, h], sb)
            dq, dk, dv, ds, dz = vjp((do_ref[:, sl], ds_scr[h]))
            dq_ref[:, sl] = dq
            dk_ref[:, sl] = dk
            dv_ref[:, sl] = dv
            ds_scr[h] = ds
            dslab = dz if dslab is None else dslab + dz
        if per_token:
            dslab_ref[...] = dslab
        else:
            dslab_ref[...] += dslab

    return pl.pallas_call(
        body, name=name, grid=(n,),
        in_specs=[row, row, row, slab_spec, st_spec, row], out_specs=[row, row, row, slab_spec],
        out_shape=[jax.ShapeDtypeStruct((t, wid), F32)] * 3 + [jax.ShapeDtypeStruct(slab.shape, F32)],
        scratch_shapes=[pltpu.VMEM((nheads, HEAD_DIM, HEAD_DIM), F32)],
        compiler_params=pltpu.CompilerParams(dimension_semantics=("arbitrary",)),
    )(q, k, v, slab, states, do)


def ret_chunk(q, k, v, s, logit_slab, sel):
    lane, rev = sel
    c = q.shape[0]
    lg = -_softplus(-_lane_pick(logit_slab, lane))
    i = lax.broadcasted_iota(jnp.int32, (c, c), 0)
    j = lax.broadcasted_iota(jnp.int32, (c, c), 1)
    rel = ((j - i) if rev else (i - j)).astype(F32)
    decay = jnp.where(rel >= 0, jnp.exp(jnp.maximum(rel, 0.0) * lg), 0.0)
    pos = lax.broadcasted_iota(jnp.int32, (c, 1), 0).astype(F32)
    pos = (c - 1.0 - pos) if rev else pos
    q_decay = jnp.exp((pos + 1.0) * lg)
    k_decay = jnp.exp((c - 1.0 - pos) * lg)
    intra = mdot(q, k, 1, 1) * decay
    o = mdot(intra, v, 1, 0) + mdot(q * q_decay, s, 1, 0)
    s_new = s * jnp.exp(c * lg) + mdot(k * k_decay, v, 0, 0)
    return o, s_new


def dn_chunk(q, k, v, s, gb, sel):
    g_lane, b_lane, rev = sel
    c = q.shape[0]
    g = _lane_pick(gb, g_lane)
    beta = _lane_pick(gb, b_lane)
    i = lax.broadcasted_iota(jnp.int32, (c, c), 0)
    j = lax.broadcasted_iota(jnp.int32, (c, c), 1)
    tri = (i <= j) if rev else (i >= j)
    strict = (i < j) if rev else (i > j)
    gc = _lane_pick(mdot(tri.astype(F32), gb, 1, 0, True), g_lane)
    gc_row = jnp.sum(jnp.where(i == j, gc, 0.0), axis=0, keepdims=True)
    decay = jnp.where(tri, jnp.exp(jnp.where(tri, gc - gc_row, 0.0)), 0.0)
    kb = k * beta
    vb = v * beta
    a = jnp.where(strict, mdot(kb, k, 1, 1) * decay, 0.0)
    t = tri_inv(a)
    e = jnp.exp(gc)
    g_last = jnp.sum(g, axis=0, keepdims=True)
    w_val = mdot(t, vb, 1, 0)
    k_cum = mdot(t, kb * e, 1, 0)
    qk = mdot(q, k, 1, 1) * decay
    v_new = w_val - mdot(k_cum, s, 1, 0)
    o = mdot(q * e, s, 1, 0) + mdot(qk, v_new, 1, 0)
    s_new = s * jnp.exp(g_last) + mdot(k * jnp.exp(g_last - gc), v_new, 0, 0)
    return o, s_new


def _scan_maps(n, ncc):
    return (lambda s: s), (lambda s: jnp.where(s < ncc, ncc - 1 - s, n - 1 - (s - ncc)))


def scan_fwd(chunk_fn, c, q, k, v, slab, sels, lc, name):
    t, wid = q.shape
    n, nheads, ncc = t // c, wid // HEAD_DIM, lc // c
    per_token = slab.shape[0] != 1
    maps = _scan_maps(n, ncc)
    rows = [pl.BlockSpec((c, wid), lambda s, m=m: (m(s), 0)) for m in maps]
    slabs = [pl.BlockSpec((c, LANES), lambda s, m=m: (m(s), 0)) if per_token else _full_spec(slab) for m in maps]
    st_spec = pl.BlockSpec((1, 2 * nheads, HEAD_DIM, HEAD_DIM), lambda s: (s, 0, 0, 0))

    def body(qf, kf, vf, sbf, qb, kb, vb, sbb, of_ref, ob_ref, st_ref, s_scr):
        @pl.when(pl.program_id(0) == 0)
        def _():
            s_scr[...] = jnp.zeros_like(s_scr)

        work = []
        for d, (qr, kr, vr, sr) in enumerate(((qf, kf, vf, sbf), (qb, kb, vb, sbb))):
            sb = sr[...]
            for h in range(nheads):
                sl = slice(h * HEAD_DIM, (h + 1) * HEAD_DIM)
                work.append((d, h, sl, qr[:, sl], kr[:, sl], vr[:, sl], s_scr[d * nheads + h], sb))
        res = [chunk_fn(qq, kk, vv, ss, sb, sels[d][h]) for d, h, sl, qq, kk, vv, ss, sb in work]
        for (d, h, sl, _, _, _, ss, _), (o, s_new) in zip(work, res):
            st_ref[0, d * nheads + h] = ss
            (of_ref, ob_ref)[d][:, sl] = o
            s_scr[d * nheads + h] = s_new

    return pl.pallas_call(
        body, name=name, grid=(n,),
        in_specs=[rows[0]] * 3 + [slabs[0]] + [rows[1]] * 3 + [slabs[1]], out_specs=[rows[0], rows[1], st_spec],
        out_shape=[jax.ShapeDtypeStruct((t, wid), F32)] * 2 + [jax.ShapeDtypeStruct((n, 2 * nheads, HEAD_DIM, HEAD_DIM), F32)],
        scratch_shapes=[pltpu.VMEM((2 * nheads, HEAD_DIM, HEAD_DIM), F32)],
        compiler_params=pltpu.CompilerParams(dimension_semantics=("arbitrary",)),
    )(q, k, v, slab, q, k, v, slab)


def scan_bwd(chunk_fn, c, q, k, v, slab, states, do, sels, lc, name):
    t, wid = q.shape
    n, nheads, ncc = t // c, wid // HEAD_DIM, lc // c
    per_token = slab.shape[0] != 1
    maps = [lambda s, m=m: m(n - 1 - s) for m in _scan_maps(n, ncc)]
    rows = [pl.BlockSpec((c, wid), lambda s, m=m: (m(s), 0)) for m in maps]
    slabs = [pl.BlockSpec((c, LANES), lambda s, m=m: (m(s), 0)) if per_token else _full_spec(slab) for m in maps]
    st_spec = pl.BlockSpec((1, 2 * nheads, HEAD_DIM, HEAD_DIM), lambda s: (n - 1 - s, 0, 0, 0))
    n_slab_out = 2 if per_token else 1

    def body(*refs):
        ins = (refs[0:4] + (refs[9],), refs[4:8] + (refs[10],))
        st_ref = refs[8]
        outs = refs[11:11 + 6 + n_slab_out]
        ds_scr = refs[-1]

        @pl.when(pl.program_id(0) == 0)
        def _():
            ds_scr[...] = jnp.zeros_like(ds_scr)
            if not per_token:
                outs[6][...] = jnp.zeros_like(outs[6])

        work = []
        for d, (qr, kr, vr, sr, dor) in enumerate(ins):
            sb = sr[...]
            for h in range(nheads):
                sl = slice(h * HEAD_DIM, (h + 1) * HEAD_DIM)
                idx = d * nheads + h
                work.append((d, h, sl, idx, (qr[:, sl], kr[:, sl], vr[:, sl], st_ref[0, idx], sb), (dor[:, sl], ds_scr[idx])))
        res = []
        for d, h, sl, idx, prim, cot in work:
            _, vjp = jax.vjp(lambda a, b, cc, s, z, d=d, h=h: chunk_fn(a, b, cc, s, z, sels[d][h]), *prim)
            res.append(vjp(cot))
        dslab = [None, None]
        for (d, h, sl, idx, _, _), (dq, dk, dv, ds, dz) in zip(work, res):
            outs[3 * d][:, sl] = dq
            outs[3 * d + 1][:, sl] = dk
            outs[3 * d + 2][:, sl] = dv
            ds_scr[idx] = ds
            dslab[d] = dz if dslab[d] is None else dslab[d] + dz
        if per_token:
            outs[6][...] = dslab[0]
            outs[7][...] = dslab[1]
        else:
            outs[6][...] += dslab[0] + dslab[1]

    return pl.pallas_call(
        body, name=name, grid=(n,),
        in_specs=[rows[0]] * 3 + [slabs[0]] + [rows[1]] * 3 + [slabs[1]] + [st_spec, rows[0], rows[1]],
        out_specs=[rows[0]] * 3 + [rows[1]] * 3 + ([slabs[0], slabs[1]] if per_token else [slabs[0]]),
        out_shape=[jax.ShapeDtypeStruct((t, wid), F32)] * 6 + [jax.ShapeDtypeStruct(slab.shape, F32)] * n_slab_out,
        scratch_shapes=[pltpu.VMEM((2 * nheads, HEAD_DIM, HEAD_DIM), F32)],
        compiler_params=pltpu.CompilerParams(dimension_semantics=("arbitrary",)),
    )(q, k, v, slab, q, k, v, slab, states, do, do)


@jax.custom_vjp
def tri_inv_all(mats):
    n = mats[0].shape[0]
    r = lax.broadcasted_iota(jnp.int32, (n, n), 0)
    c = lax.broadcasted_iota(jnp.int32, (n, n), 1)
    eye = (r == c).astype(F32)
    ps = [-a for a in mats]
    ts = [eye + p for p in ps]
    k = 2
    while k < n:
        ps = [_contract(p, p, 1, 0, True) for p in ps]
        ts = [t + _contract(t, p, 1, 0, True) for t, p in zip(ts, ps)]
        k *= 2
    return tuple(ts)


def _tri_inv_all_fwd(mats):
    ts = tri_inv_all(mats)
    return ts, ts


def _tri_inv_all_bwd(ts, gs):
    left = [_contract(t, g, 0, 0, True) for t, g in zip(ts, gs)]
    return (tuple(-_contract(l, t, 1, 1, True) for l, t in zip(left, ts)),)


tri_inv_all.defvjp(_tri_inv_all_fwd, _tri_inv_all_bwd)


def ret_chunk(qs, ks, vs, ss, slabs, sels):
    c = qs[0].shape[0]
    i = lax.broadcasted_iota(jnp.int32, (c, c), 0)
    j = lax.broadcasted_iota(jnp.int32, (c, c), 1)
    pos0 = lax.broadcasted_iota(jnp.int32, (c, 1), 0).astype(F32)
    lgs = [-_softplus(-_lane_pick(slabs[d], lane)) for d, lane, _ in sels]
    rels = [((j - i) if rev else (i - j)).astype(F32) for _, _, rev in sels]
    decays = [jnp.where(rel >= 0, jnp.exp(jnp.maximum(rel, 0.0) * lg), 0.0) for rel, lg in zip(rels, lgs)]
    poss = [(c - 1.0 - pos0) if rev else pos0 for _, _, rev in sels]
    intra = [mdot(q, k, 1, 1) * dec for q, k, dec in zip(qs, ks, decays)]
    kv = [mdot(k * jnp.exp((c - 1.0 - pos) * lg), v, 0, 0) for k, v, pos, lg in zip(ks, vs, poss, lgs)]
    o1 = [mdot(a, v, 1, 0) for a, v in zip(intra, vs)]
    o2 = [mdot(q * jnp.exp((pos + 1.0) * lg), s, 1, 0) for q, s, pos, lg in zip(qs, ss, poss, lgs)]
    outs = [a + b for a, b in zip(o1, o2)]
    s_new = [s * jnp.exp(c * lg) + u for s, lg, u in zip(ss, lgs, kv)]
    return outs, s_new


def dn_chunk(qs, ks, vs, ss, slabs, sels):
    c = qs[0].shape[0]
    i = lax.broadcasted_iota(jnp.int32, (c, c), 0)
    j = lax.broadcasted_iota(jnp.int32, (c, c), 1)
    tris = [(i <= j) if rev else (i >= j) for _, _, _, rev in sels]
    stricts = [(i < j) if rev else (i > j) for _, _, _, rev in sels]
    gs = [_lane_pick(slabs[d], gl) for d, gl, _, _ in sels]
    betas = [_lane_pick(slabs[d], bl) for d, _, bl, _ in sels]
    sums = [mdot(tri.astype(F32), slabs[sel[0]], 1, 0, True) for tri, sel in zip(tris, sels)]
    gcs = [_lane_pick(cs, sel[1]) for cs, sel in zip(sums, sels)]
    gc_rows = [jnp.sum(jnp.where(i == j, gc, 0.0), axis=0, keepdims=True) for gc in gcs]
    decays = [jnp.where(tri, jnp.exp(jnp.where(tri, gc - gr, 0.0)), 0.0) for tri, gc, gr in zip(tris, gcs, gc_rows)]
    kbs = [k * b for k, b in zip(ks, betas)]
    vbs = [v * b for v, b in zip(vs, betas)]
    kk = [mdot(kb, k, 1, 1) for kb, k in zip(kbs, ks)]
    qk = [mdot(q, k, 1, 1) * dec for q, k, dec in zip(qs, ks, decays)]
    ts = tri_inv_all(tuple(jnp.where(st, a * dec, 0.0) for st, a, dec in zip(stricts, kk, decays)))
    es = [jnp.exp(gc) for gc in gcs]
    g_last = [jnp.sum(g, axis=0, keepdims=True) for g in gs]
    w_val = [mdot(t, vb, 1, 0) for t, vb in zip(ts, vbs)]
    k_cum = [mdot(t, kb * e, 1, 0) for t, kb, e in zip(ts, kbs, es)]
    ks_s = [mdot(kc, s, 1, 0) for kc, s in zip(k_cum, ss)]
    qs_s = [mdot(q * e, s, 1, 0) for q, e, s in zip(qs, es, ss)]
    v_new = [w - u for w, u in zip(w_val, ks_s)]
    o2 = [mdot(a, vn, 1, 0) for a, vn in zip(qk, v_new)]
    upd = [mdot(k * jnp.exp(gl - gc), vn, 0, 0) for k, gl, gc, vn in zip(ks, g_last, gcs, v_new)]
    outs = [a + b for a, b in zip(qs_s, o2)]
    s_new = [s * jnp.exp(gl) + u for s, gl, u in zip(ss, g_last, upd)]
    return outs, s_new


def scan_fwd(chunk_fn, c, q, k, v, slab, sels, lc, name):
    t, wid = q.shape
    n, nheads, ncc = t // c, wid // HEAD_DIM, lc // c
    per_token = slab.shape[0] != 1
    maps = _scan_maps(n, ncc)
    rows = [pl.BlockSpec((c, wid), lambda s, m=m: (m(s), 0)) for m in maps]
    slabs = [pl.BlockSpec((c, LANES), lambda s, m=m: (m(s), 0)) if per_token else _full_spec(slab) for m in maps]
    st_spec = pl.BlockSpec((1, 2 * nheads, HEAD_DIM, HEAD_DIM), lambda s: (s, 0, 0, 0))
    items = [(d, h) for d in range(2) for h in range(nheads)]
    flat_sels = [(d,) + tuple(sels[d][h]) for d, h in items]

    def body(qf, kf, vf, sbf, qb, kb, vb, sbb, of_ref, ob_ref, st_ref, s_scr):
        @pl.when(pl.program_id(0) == 0)
        def _():
            s_scr[...] = jnp.zeros_like(s_scr)

        qr, kr, vr = (qf, qb), (kf, kb), (vf, vb)
        sl = lambda h: slice(h * HEAD_DIM, (h + 1) * HEAD_DIM)
        ss = [s_scr[n_] for n_ in range(len(items))]
        outs, s_new = chunk_fn([qr[d][:, sl(h)] for d, h in items], [kr[d][:, sl(h)] for d, h in items],
                               [vr[d][:, sl(h)] for d, h in items], ss, [sbf[...], sbb[...]], flat_sels)
        for n_, (d, h) in enumerate(items):
            st_ref[0, n_] = ss[n_]
            (of_ref, ob_ref)[d][:, sl(h)] = outs[n_]
            s_scr[n_] = s_new[n_]

    return pl.pallas_call(
        body, name=name, grid=(n,),
        in_specs=[rows[0]] * 3 + [slabs[0]] + [rows[1]] * 3 + [slabs[1]], out_specs=[rows[0], rows[1], st_spec],
        out_shape=[jax.ShapeDtypeStruct((t, wid), F32)] * 2 + [jax.ShapeDtypeStruct((n, 2 * nheads, HEAD_DIM, HEAD_DIM), F32)],
        scratch_shapes=[pltpu.VMEM((2 * nheads, HEAD_DIM, HEAD_DIM), F32)],
        compiler_params=pltpu.CompilerParams(dimension_semantics=("arbitrary",)),
    )(q, k, v, slab, q, k, v, slab)


def scan_bwd(chunk_fn, c, q, k, v, slab, states, do, sels, lc, name):
    t, wid = q.shape
    n, nheads, ncc = t // c, wid // HEAD_DIM, lc // c
    per_token = slab.shape[0] != 1
    maps = [lambda s, m=m: m(n - 1 - s) for m in _scan_maps(n, ncc)]
    rows = [pl.BlockSpec((c, wid), lambda s, m=m: (m(s), 0)) for m in maps]
    slabs = [pl.BlockSpec((c, LANES), lambda s, m=m: (m(s), 0)) if per_token else _full_spec(slab) for m in maps]
    st_spec = pl.BlockSpec((1, 2 * nheads, HEAD_DIM, HEAD_DIM), lambda s: (n - 1 - s, 0, 0, 0))
    n_slab_out = 2 if per_token else 1
    items = [(d, h) for d in range(2) for h in range(nheads)]
    flat_sels = [(d,) + tuple(sels[d][h]) for d, h in items]

    def body(*refs):
        qr, kr, vr, sr = (refs[0], refs[4]), (refs[1], refs[5]), (refs[2], refs[6]), (refs[3], refs[7])
        st_ref, dor = refs[8], (refs[9], refs[10])
        outs = refs[11:11 + 6 + n_slab_out]
        ds_scr = refs[-1]

        @pl.when(pl.program_id(0) == 0)
        def _():
            ds_scr[...] = jnp.zeros_like(ds_scr)
            if not per_token:
                outs[6][...] = jnp.zeros_like(outs[6])

        sl = lambda h: slice(h * HEAD_DIM, (h + 1) * HEAD_DIM)
        prim = ([qr[d][:, sl(h)] for d, h in items], [kr[d][:, sl(h)] for d, h in items],
                [vr[d][:, sl(h)] for d, h in items], [st_ref[0, n_] for n_ in range(len(items))], [sr[0][...], sr[1][...]])
        cot = ([dor[d][:, sl(h)] for d, h in items], [ds_scr[n_] for n_ in range(len(items))])
        _, vjp = jax.vjp(lambda *a: chunk_fn(*a, flat_sels), *prim)
        dqs, dks, dvs, dss, dslabs = vjp(cot)
        for n_, (d, h) in enumerate(items):
            outs[3 * d][:, sl(h)] = dqs[n_]
            outs[3 * d + 1][:, sl(h)] = dks[n_]
            outs[3 * d + 2][:, sl(h)] = dvs[n_]
            ds_scr[n_] = dss[n_]
        if per_token:
            outs[6][...] = dslabs[0]
            outs[7][...] = dslabs[1]
        else:
            outs[6][...] += dslabs[0] + dslabs[1]

    return pl.pallas_call(
        body, name=name, grid=(n,),
        in_specs=[rows[0]] * 3 + [slabs[0]] + [rows[1]] * 3 + [slabs[1]] + [st_spec, rows[0], rows[1]],
        out_specs=[rows[0]] * 3 + [rows[1]] * 3 + ([slabs[0], slabs[1]] if per_token else [slabs[0]]),
        out_shape=[jax.ShapeDtypeStruct((t, wid), F32)] * 6 + [jax.ShapeDtypeStruct(slab.shape, F32)] * n_slab_out,
        scratch_shapes=[pltpu.VMEM((2 * nheads, HEAD_DIM, HEAD_DIM), F32)],
        compiler_params=pltpu.CompilerParams(dimension_semantics=("arbitrary",)),
    )(q, k, v, slab, q, k, v, slab, states, do, do)


def _att_tiles(t, lc):
    return _pick(lc, (256, 128)), _pick(t, (768, 384, 256, 128))


def _att_probs(q, k, lse, is_ctx, j, tk, lc):
    s = _contract(q, k, 1, 1, False) * QK_SCALE
    kidx = j * tk + lax.broadcasted_iota(jnp.int32, (1, tk), 1)
    masked = jnp.logical_and(is_ctx, kidx >= lc)
    return s, masked, (None if lse is None else jnp.where(masked, 0.0, jnp.exp(s - lse)))


def attn_fwd(qn, kn, v, lc, name):
    t = qn.shape[0]
    nh, nkv = qn.shape[1] // HEAD_DIM, kn.shape[1] // HEAD_DIM
    grp = nh // nkv
    tq, tk = _att_tiles(t, lc)
    nq, nk, nctq = t // tq, t // tk, lc // tq

    def body(q_ref, k_ref, v_ref, o_ref, lse_ref, m_scr, l_scr, acc):
        i, j = pl.program_id(1), pl.program_id(2)

        @pl.when(j == 0)
        def _():
            m_scr[...] = jnp.full_like(m_scr, NEG_BIG)
            l_scr[...] = jnp.zeros_like(l_scr)
            acc[...] = jnp.zeros_like(acc)

        is_ctx = i < nctq

        @pl.when(jnp.logical_or(jnp.logical_not(is_ctx), j * tk < lc))
        def _():
            s, masked, _ = _att_probs(q_ref[...], k_ref[...], None, is_ctx, j, tk, lc)
            s = jnp.where(masked, NEG_BIG, s)
            m_old = m_scr[...]
            m_new = jnp.maximum(m_old, jnp.max(s, axis=1, keepdims=True))
            alpha = jnp.exp(m_old - m_new)
            p = jnp.where(masked, 0.0, jnp.exp(s - m_new))
            l_scr[...] = alpha * l_scr[...] + jnp.sum(p, axis=1, keepdims=True)
            acc[...] = alpha * acc[...] + _contract(p, v_ref[...], 1, 0, False)
            m_scr[...] = m_new

        @pl.when(j == nk - 1)
        def _():
            o_ref[...] = acc[...] / l_scr[...]
            lse_ref[0] = m_scr[...] + jnp.log(l_scr[...])

    return pl.pallas_call(
        body, name=name, grid=(nh, nq, nk),
        in_specs=[pl.BlockSpec((tq, HEAD_DIM), lambda h, i, j: (i, h)),
                  pl.BlockSpec((tk, HEAD_DIM), lambda h, i, j: (j, h // grp)),
                  pl.BlockSpec((tk, HEAD_DIM), lambda h, i, j: (j, h // grp))],
        out_specs=[pl.BlockSpec((tq, HEAD_DIM), lambda h, i, j: (i, h)),
                   pl.BlockSpec((1, tq, 1), lambda h, i, j: (h, i, 0))],
        out_shape=[jax.ShapeDtypeStruct((t, nh * HEAD_DIM), F32), jax.ShapeDtypeStruct((nh, t, 1), F32)],
        scratch_shapes=[pltpu.VMEM((tq, 1), F32), pltpu.VMEM((tq, 1), F32), pltpu.VMEM((tq, HEAD_DIM), F32)],
        compiler_params=pltpu.CompilerParams(dimension_semantics=("parallel", "parallel", "arbitrary")),
    )(qn, kn, v)


def attn_bwd_dq(qn, kn, v, o, do, lse, lc, name):
    t = qn.shape[0]
    nh, nkv = qn.shape[1] // HEAD_DIM, kn.shape[1] // HEAD_DIM
    grp = nh // nkv
    tq, tk = _att_tiles(t, lc)
    nq, nk, nctq = t // tq, t // tk, lc // tq

    def body(q_ref, k_ref, v_ref, o_ref, do_ref, lse_ref, dq_ref, dl_ref, acc, dl_scr):
        i, j = pl.program_id(1), pl.program_id(2)

        @pl.when(j == 0)
        def _():
            acc[...] = jnp.zeros_like(acc)
            dl_scr[...] = jnp.sum(do_ref[...] * o_ref[...], axis=1, keepdims=True)

        is_ctx = i < nctq

        @pl.when(jnp.logical_or(jnp.logical_not(is_ctx), j * tk < lc))
        def _():
            _, _, p = _att_probs(q_ref[...], k_ref[...], lse_ref[0], is_ctx, j, tk, lc)
            dp = _contract(do_ref[...], v_ref[...], 1, 1, False)
            ds = p * (dp - dl_scr[...]) * QK_SCALE
            acc[...] += _contract(ds, k_ref[...], 1, 0, False)

        @pl.when(j == nk - 1)
        def _():
            dq_ref[...] = acc[...]
            dl_ref[0] = dl_scr[...]

    qspec = pl.BlockSpec((tq, HEAD_DIM), lambda h, i, j: (i, h))
    kspec = pl.BlockSpec((tk, HEAD_DIM), lambda h, i, j: (j, h // grp))
    rspec = pl.BlockSpec((1, tq, 1), lambda h, i, j: (h, i, 0))
    return pl.pallas_call(
        body, name=name, grid=(nh, nq, nk),
        in_specs=[qspec, kspec, kspec, qspec, qspec, rspec],
        out_specs=[qspec, rspec],
        out_shape=[jax.ShapeDtypeStruct((t, nh * HEAD_DIM), F32), jax.ShapeDtypeStruct((nh, t, 1), F32)],
        scratch_shapes=[pltpu.VMEM((tq, HEAD_DIM), F32), pltpu.VMEM((tq, 1), F32)],
        compiler_params=pltpu.CompilerParams(dimension_semantics=("parallel", "parallel", "arbitrary")),
    )(qn, kn, v, o, do, lse)


def attn_bwd_dkv(qn, kn, v, do, lse, delta, lc, name):
    t = qn.shape[0]
    nh, nkv = qn.shape[1] // HEAD_DIM, kn.shape[1] // HEAD_DIM
    grp = nh // nkv
    tq, tk = _att_tiles(t, lc)
    nq, nk, nctq = t // tq, t // tk, lc // tq
    nr = grp * nq

    def body(q_ref, k_ref, v_ref, do_ref, lse_ref, dl_ref, dk_ref, dv_ref, dk_acc, dv_acc):
        j, r = pl.program_id(1), pl.program_id(2)

        @pl.when(r == 0)
        def _():
            dk_acc[...] = jnp.zeros_like(dk_acc)
            dv_acc[...] = jnp.zeros_like(dv_acc)

        is_ctx = (r % nq) < nctq

        @pl.when(jnp.logical_or(jnp.logical_not(is_ctx), j * tk < lc))
        def _():
            _, _, p = _att_probs(q_ref[...], k_ref[...], lse_ref[0], is_ctx, j, tk, lc)
            dv_acc[...] += _contract(p, do_ref[...], 0, 0, False)
            dp = _contract(do_ref[...], v_ref[...], 1, 1, False)
            ds = p * (dp - dl_ref[0]) * QK_SCALE
            dk_acc[...] += _contract(ds, q_ref[...], 0, 0, False)

        @pl.when(r == nr - 1)
        def _():
            dk_ref[...] = dk_acc[...]
            dv_ref[...] = dv_acc[...]

    qspec = pl.BlockSpec((tq, HEAD_DIM), lambda g, j, r: (r % nq, g * grp + r // nq))
    kspec = pl.BlockSpec((tk, HEAD_DIM), lambda g, j, r: (j, g))
    rspec = pl.BlockSpec((1, tq, 1), lambda g, j, r: (g * grp + r // nq, r % nq, 0))
    return pl.pallas_call(
        body, name=name, grid=(nkv, nk, nr),
        in_specs=[qspec, kspec, kspec, qspec, rspec, rspec],
        out_specs=[kspec, kspec],
        out_shape=[jax.ShapeDtypeStruct((t, nkv * HEAD_DIM), F32), jax.ShapeDtypeStruct((t, nkv * HEAD_DIM), F32)],
        scratch_shapes=[pltpu.VMEM((tk, HEAD_DIM), F32), pltpu.VMEM((tk, HEAD_DIM), F32)],
        compiler_params=pltpu.CompilerParams(dimension_semantics=("parallel", "parallel", "arbitrary")),
    )(qn, kn, v, do, lse, delta)


LOG2E = 1.4426950408889634
LN2 = 0.6931471805599453


def _att_grid(qn, kn, lc):
    t = qn.shape[0]
    nkv = kn.shape[1] // HEAD_DIM
    grp = qn.shape[1] // HEAD_DIM // nkv
    tq, tk = _att_tiles(t, lc)
    return t, nkv, grp, tq, tk, t // tq, t // tk, lc // tq


def _att_paths(i, j, nctq, tk, lc, step):
    is_ctx = i < nctq

    @pl.when(jnp.logical_and(is_ctx, j * tk < lc))
    def _():
        kidx = j * tk + lax.broadcasted_iota(jnp.int32, (1, tk), 1)
        step(kidx >= lc)

    @pl.when(jnp.logical_not(is_ctx))
    def _():
        step(None)


def _att_scores2(q, k, hidden):
    s = _contract(q, k, 1, 1, False)
    return s if hidden is None else jnp.where(hidden, NEG_BIG, s)


def attn_fwd(qn, kn, v, lc, name):
    t, nkv, grp, tq, tk, nq, nk, nctq = _att_grid(qn, kn, lc)

    def body(q_ref, k_ref, v_ref, o_ref, lse_ref, m_scr, l_scr, acc):
        i, j = pl.program_id(1), pl.program_id(2)

        @pl.when(j == 0)
        def _():
            m_scr[...] = jnp.full_like(m_scr, NEG_BIG)
            l_scr[...] = jnp.zeros_like(l_scr)
            acc[...] = jnp.zeros_like(acc)

        def step(hidden):
            k, vv = k_ref[...], v_ref[...]
            qs = [q_ref[:, h * HEAD_DIM:(h + 1) * HEAD_DIM] for h in range(grp)]
            ms = [m_scr[h] for h in range(grp)]
            ls = [l_scr[h] for h in range(grp)]
            accs = [acc[h] for h in range(grp)]
            ss = [_att_scores2(q, k, hidden) for q in qs]
            m_new = [jnp.maximum(m, jnp.max(s, axis=1, keepdims=True)) for m, s in zip(ms, ss)]
            alpha = [jnp.exp2(m - mn) for m, mn in zip(ms, m_new)]
            ps = [jnp.exp2(s - mn) for s, mn in zip(ss, m_new)]
            pv = [_contract(p, vv, 1, 0, False) for p in ps]
            for h in range(grp):
                m_scr[h] = m_new[h]
                l_scr[h] = alpha[h] * ls[h] + jnp.sum(ps[h], axis=1, keepdims=True)
                acc[h] = alpha[h] * accs[h] + pv[h]

        _att_paths(i, j, nctq, tk, lc, step)

        @pl.when(j == nk - 1)
        def _():
            for h in range(grp):
                o_ref[:, h * HEAD_DIM:(h + 1) * HEAD_DIM] = acc[h] / l_scr[h]
                lse_ref[h] = m_scr[h] * LN2 + jnp.log(l_scr[h])

    wid = grp * HEAD_DIM
    return pl.pallas_call(
        body, name=name, grid=(nkv, nq, nk),
        in_specs=[pl.BlockSpec((tq, wid), lambda g, i, j: (i, g)),
                  pl.BlockSpec((tk, HEAD_DIM), lambda g, i, j: (j, g)),
                  pl.BlockSpec((tk, HEAD_DIM), lambda g, i, j: (j, g))],
        out_specs=[pl.BlockSpec((tq, wid), lambda g, i, j: (i, g)),
                   pl.BlockSpec((grp, tq, 1), lambda g, i, j: (g, i, 0))],
        out_shape=[jax.ShapeDtypeStruct((t, nkv * wid), F32), jax.ShapeDtypeStruct((nkv * grp, t, 1), F32)],
        scratch_shapes=[pltpu.VMEM((grp, tq, 1), F32), pltpu.VMEM((grp, tq, 1), F32), pltpu.VMEM((grp, tq, HEAD_DIM), F32)],
        compiler_params=pltpu.CompilerParams(dimension_semantics=("parallel", "parallel", "arbitrary")),
    )(qn, kn, v)


def attn_bwd_dq(qn, kn, v, o, do, lse, lc, name):
    t, nkv, grp, tq, tk, nq, nk, nctq = _att_grid(qn, kn, lc)

    def body(q_ref, k_ref, v_ref, o_ref, do_ref, lse_ref, dq_ref, dl_ref, acc, dl_scr):
        i, j = pl.program_id(1), pl.program_id(2)

        @pl.when(j == 0)
        def _():
            acc[...] = jnp.zeros_like(acc)
            for h in range(grp):
                sl = slice(h * HEAD_DIM, (h + 1) * HEAD_DIM)
                dl_scr[h] = jnp.sum(do_ref[:, sl] * o_ref[:, sl], axis=1, keepdims=True)

        def step(hidden):
            k, vv = k_ref[...], v_ref[...]
            sls = [slice(h * HEAD_DIM, (h + 1) * HEAD_DIM) for h in range(grp)]
            ss = [_att_scores2(q_ref[:, sl], k, hidden) for sl in sls]
            dps = [_contract(do_ref[:, sl], vv, 1, 1, False) for sl in sls]
            ps = [jnp.exp2(s - lse_ref[h] * LOG2E) for h, s in enumerate(ss)]
            dss = [p * (dp - dl_scr[h]) for h, (p, dp) in enumerate(zip(ps, dps))]
            upd = [_contract(ds, k, 1, 0, False) for ds in dss]
            for h in range(grp):
                acc[h] += upd[h]

        _att_paths(i, j, nctq, tk, lc, step)

        @pl.when(j == nk - 1)
        def _():
            for h in range(grp):
                dq_ref[:, h * HEAD_DIM:(h + 1) * HEAD_DIM] = acc[h] * QK_SCALE
                dl_ref[h] = dl_scr[h]

    wid = grp * HEAD_DIM
    qspec = pl.BlockSpec((tq, wid), lambda g, i, j: (i, g))
    kspec = pl.BlockSpec((tk, HEAD_DIM), lambda g, i, j: (j, g))
    rspec = pl.BlockSpec((grp, tq, 1), lambda g, i, j: (g, i, 0))
    return pl.pallas_call(
        body, name=name, grid=(nkv, nq, nk),
        in_specs=[qspec, kspec, kspec, qspec, qspec, rspec],
        out_specs=[qspec, rspec],
        out_shape=[jax.ShapeDtypeStruct((t, nkv * wid), F32), jax.ShapeDtypeStruct((nkv * grp, t, 1), F32)],
        scratch_shapes=[pltpu.VMEM((grp, tq, HEAD_DIM), F32), pltpu.VMEM((grp, tq, 1), F32)],
        compiler_params=pltpu.CompilerParams(dimension_semantics=("parallel", "parallel", "arbitrary")),
    )(qn, kn, v, o, do, lse)


def attn_bwd_dkv(qn, kn, v, do, lse, delta, lc, name):
    t, nkv, grp, tq, tk, nq, nk, nctq = _att_grid(qn, kn, lc)

    def body(q_ref, k_ref, v_ref, do_ref, lse_ref, dl_ref, dk_ref, dv_ref, dk_acc, dv_acc):
        j, i = pl.program_id(1), pl.program_id(2)

        @pl.when(i == 0)
        def _():
            dk_acc[...] = jnp.zeros_like(dk_acc)
            dv_acc[...] = jnp.zeros_like(dv_acc)

        def step(hidden):
            k, vv = k_ref[...], v_ref[...]
            sls = [slice(h * HEAD_DIM, (h + 1) * HEAD_DIM) for h in range(grp)]
            qs = [q_ref[:, sl] for sl in sls]
            dos = [do_ref[:, sl] for sl in sls]
            ss = [_att_scores2(q, k, hidden) for q in qs]
            dps = [_contract(do, vv, 1, 1, False) for do in dos]
            ps = [jnp.exp2(s - lse_ref[h] * LOG2E) for h, s in enumerate(ss)]
            dss = [p * (dp - dl_ref[h]) for h, (p, dp) in enumerate(zip(ps, dps))]
            dv_new = [_contract(p, do, 0, 0, False) for p, do in zip(ps, dos)]
            dk_new = [_contract(ds, q, 0, 0, False) for ds, q in zip(dss, qs)]
            dv_acc[...] += (dv_new[0] + dv_new[1]) + (dv_new[2] + dv_new[3]) if grp == 4 else sum(dv_new)
            dk_acc[...] += (dk_new[0] + dk_new[1]) + (dk_new[2] + dk_new[3]) if grp == 4 else sum(dk_new)

        _att_paths(i, j, nctq, tk, lc, step)

        @pl.when(i == nq - 1)
        def _():
            dk_ref[...] = dk_acc[...] * LN2
            dv_ref[...] = dv_acc[...]

    wid = grp * HEAD_DIM
    qspec = pl.BlockSpec((tq, wid), lambda g, j, i: (i, g))
    kspec = pl.BlockSpec((tk, HEAD_DIM), lambda g, j, i: (j, g))
    rspec = pl.BlockSpec((grp, tq, 1), lambda g, j, i: (g, i, 0))
    return pl.pallas_call(
        body, name=name, grid=(nkv, nk, nq),
        in_specs=[qspec, kspec, kspec, qspec, rspec, rspec],
        out_specs=[kspec, kspec],
        out_shape=[jax.ShapeDtypeStruct((t, nkv * HEAD_DIM), F32), jax.ShapeDtypeStruct((t, nkv * HEAD_DIM), F32)],
        scratch_shapes=[pltpu.VMEM((tk, HEAD_DIM), F32), pltpu.VMEM((tk, HEAD_DIM), F32)],
        compiler_params=pltpu.CompilerParams(dimension_semantics=("parallel", "parallel", "arbitrary")),
    )(qn, kn, v, do, lse, delta)


def loss_and_grad(y, target, tm, name):
    t, d = y.shape

    def body(y_ref, t_ref, l_ref, g_ref):
        @pl.when(pl.program_id(0) == 0)
        def _():
            l_ref[...] = jnp.zeros_like(l_ref)

        e = y_ref[...] - t_ref[...]
        g_ref[...] = e * (1.0 / d)
        l_ref[...] += 0.5 * jnp.sum(jnp.mean(e * e, axis=1, keepdims=True), axis=0, keepdims=True)

    return pl.pallas_call(
        body, name=name, grid=(t // tm,),
        in_specs=[pl.BlockSpec((tm, d), lambda i: (i, 0))] * 2,
        out_specs=[pl.BlockSpec((1, 1), lambda i: (0, 0)), pl.BlockSpec((tm, d), lambda i: (i, 0))],
        out_shape=[jax.ShapeDtypeStruct((1, 1), F32), jax.ShapeDtypeStruct((t, d), F32)],
        compiler_params=pltpu.CompilerParams(dimension_semantics=("arbitrary",)),
    )(y, target)


def _row_tile(rows, width):
    budget = max(8, (2 * 1024 * 1024) // (4 * width))
    for cand in (1024, 512, 256, 128, 64, 32, 16, 8):
        if cand <= budget and rows % cand == 0:
            return cand
    return rows


def ew_sum(arrs, name, out_dtype=F32):
    rows, width = arrs[0].shape
    tr = _row_tile(rows, width)

    def body(*refs):
        acc = refs[0][...].astype(F32)
        for r in refs[1:-1]:
            acc = acc + r[...].astype(F32)
        refs[-1][...] = acc.astype(refs[-1].dtype)

    spec = pl.BlockSpec((tr, width), lambda i: (i, 0))
    return pl.pallas_call(
        body, name=name, grid=(rows // tr,), in_specs=[spec] * len(arrs), out_specs=spec,
        out_shape=jax.ShapeDtypeStruct((rows, width), out_dtype),
        compiler_params=pltpu.CompilerParams(dimension_semantics=("parallel",)),
    )(*arrs)


def adamw(g, w, m, v, name):
    rows, width = g.shape
    tr = _row_tile(rows, width)

    def body(g_ref, w_ref, m_ref, v_ref, d_ref, mo_ref, vo_ref):
        gg = g_ref[...]
        m_new = ADAM_B1 * m_ref[...] + (1.0 - ADAM_B1) * gg
        v_new = ADAM_B2 * v_ref[...] + (1.0 - ADAM_B2) * jnp.square(gg)
        m_hat = m_new / (1.0 - ADAM_B1 ** ADAM_STEP)
        v_hat = v_new / (1.0 - ADAM_B2 ** ADAM_STEP)
        d_ref[...] = -ADAM_LR * (m_hat / (jnp.sqrt(v_hat) + ADAM_EPS) + ADAM_WD * w_ref[...])
        mo_ref[...] = m_new
        vo_ref[...] = v_new

    spec = pl.BlockSpec((tr, width), lambda i: (i, 0))
    return pl.pallas_call(
        body, name=name, grid=(rows // tr,), in_specs=[spec] * 4, out_specs=[spec] * 3,
        out_shape=[jax.ShapeDtypeStruct((rows, width), F32)] * 3,
        compiler_params=pltpu.CompilerParams(dimension_semantics=("parallel",)),
    )(g, w, m, v)


def _place():
    return lax.axis_index("x"), lax.axis_index("y"), lax.axis_index("c")


def _other_chips(x, y):
    return [(1 - x, y), (x, 1 - y), (1 - x, 1 - y)]


def allgather8(x_shard, name):
    m_per, n = x_shard.shape

    def body(x_ref, out_ref, send_sems, recv_sems, local_sem):
        x, y, c = _place()
        me, sibling = (x, y, c), (x, y, 1 - c)
        chips = _other_chips(x, y)

        def rows(px, py, pc):
            return out_ref.at[pl.ds((4 * px + 2 * py + pc) * m_per, m_per), :]

        def copy(k, block, to, src=None):
            return pltpu.make_async_remote_copy(
                src_ref=rows(*block) if src is None else src, dst_ref=rows(*block),
                send_sem=send_sems.at[k], recv_sem=recv_sems.at[k], device_id=to, device_id_type=MESH_ID)

        mine = pltpu.make_async_copy(x_ref, rows(*me), local_sem)
        mine.start()
        first = [copy(0, me, sibling, src=x_ref)]
        first += [copy(1 + j, me, (*chip, c), src=x_ref) for j, chip in enumerate(chips)]
        for cp in first:
            cp.start()
        passed = [copy(4 + j, (*chip, c), sibling) for j, chip in enumerate(chips)]
        for j, chip in enumerate(chips):
            copy(1 + j, (*chip, c), me).wait_recv()
            passed[j].start()
        copy(0, sibling, me).wait_recv()
        for j, chip in enumerate(chips):
            copy(4 + j, (*chip, 1 - c), me).wait_recv()
        for cp in first + passed:
            cp.wait_send()
        mine.wait()

    return pl.pallas_call(
        body, name=name,
        out_shape=jax.ShapeDtypeStruct((8 * m_per, n), x_shard.dtype),
        in_specs=[pl.BlockSpec(memory_space=pltpu.VMEM)],
        out_specs=pl.BlockSpec(memory_space=pltpu.VMEM),
        scratch_shapes=[pltpu.SemaphoreType.DMA((7,)), pltpu.SemaphoreType.DMA((7,)), pltpu.SemaphoreType.DMA],
    )(x_shard)


_ANY = pl.BlockSpec(memory_space=pl.ANY)


def gather_chips(shards, name):
    n = len(shards)

    def body(*refs):
        ins, outs = refs[:n], refs[n:2 * n]
        send_sems, recv_sems, local_sems = refs[2 * n:]
        x, y, c = _place()
        chips = _other_chips(x, y)
        started = []
        for a in range(n):
            loc = pltpu.make_async_copy(ins[a], outs[a].at[2 * x + y], local_sems.at[a])
            loc.start()
            started.append(loc)
        sends = []
        for a in range(n):
            for j, chip in enumerate(chips):
                cp = pltpu.make_async_remote_copy(
                    src_ref=ins[a], dst_ref=outs[a].at[2 * x + y], send_sem=send_sems.at[3 * a + j],
                    recv_sem=recv_sems.at[3 * a + j], device_id=(*chip, c), device_id_type=MESH_ID)
                cp.start()
                sends.append(cp)
        for a in range(n):
            for j, chip in enumerate(chips):
                pltpu.make_async_remote_copy(
                    src_ref=ins[a], dst_ref=outs[a].at[2 * chip[0] + chip[1]], send_sem=send_sems.at[3 * a + j],
                    recv_sem=recv_sems.at[3 * a + j], device_id=(*chip, c), device_id_type=MESH_ID).wait_recv()
        for cp in sends:
            cp.wait_send()
        for loc in started:
            loc.wait()

    return pl.pallas_call(
        body, name=name,
        out_shape=[jax.ShapeDtypeStruct((4,) + s.shape, s.dtype) for s in shards],
        in_specs=[_ANY] * n, out_specs=[_ANY] * n,
        scratch_shapes=[pltpu.SemaphoreType.DMA((3 * n,)), pltpu.SemaphoreType.DMA((3 * n,)), pltpu.SemaphoreType.DMA((n,))],
    )(*shards)


def rs_sibling(grads, name):
    n = len(grads)

    def body(*refs):
        ins, mine, got = refs[:n], refs[n:2 * n], refs[2 * n:3 * n]
        send_sems, recv_sems, local_sems = refs[3 * n:]
        x, y, c = _place()
        pend = []
        for a in range(n):
            h = ins[a].shape[1] // 2
            loc = pltpu.make_async_copy(ins[a].at[:, pl.ds(pl.multiple_of(c * h, 8), h), :], mine[a], local_sems.at[a])
            cp = pltpu.make_async_remote_copy(
                src_ref=ins[a].at[:, pl.ds(pl.multiple_of((1 - c) * h, 8), h), :], dst_ref=got[a],
                send_sem=send_sems.at[a], recv_sem=recv_sems.at[a], device_id=(x, y, 1 - c), device_id_type=MESH_ID)
            loc.start()
            cp.start()
            pend.append((loc, cp))
        for loc, cp in pend:
            cp.wait()
            loc.wait()

    half = [jax.ShapeDtypeStruct((g.shape[0], g.shape[1] // 2, g.shape[2]), g.dtype) for g in grads]
    return pl.pallas_call(
        body, name=name, out_shape=half + half, in_specs=[_ANY] * n, out_specs=[_ANY] * (2 * n),
        scratch_shapes=[pltpu.SemaphoreType.DMA((n,)), pltpu.SemaphoreType.DMA((n,)), pltpu.SemaphoreType.DMA((n,))],
    )(*grads)


def rs_chips(parts, name):
    n = len(parts)

    def body(*refs):
        ins, mine, got = refs[:n], refs[n:2 * n], refs[2 * n:3 * n]
        send_sems, recv_sems, local_sems = refs[3 * n:]
        x, y, c = _place()
        chips = _other_chips(x, y)
        pend = []
        for a in range(n):
            loc = pltpu.make_async_copy(ins[a].at[2 * x + y], mine[a], local_sems.at[a])
            loc.start()
            pend.append(loc)
            for j, chip in enumerate(chips):
                cp = pltpu.make_async_remote_copy(
                    src_ref=ins[a].at[2 * chip[0] + chip[1]], dst_ref=got[a].at[j],
                    send_sem=send_sems.at[3 * a + j], recv_sem=recv_sems.at[3 * a + j],
                    device_id=(*chip, c), device_id_type=MESH_ID)
                cp.start()
                pend.append(cp)
        for p in pend:
            p.wait()

    return pl.pallas_call(
        body, name=name,
        out_shape=[jax.ShapeDtypeStruct(p.shape[1:], p.dtype) for p in parts]
        + [jax.ShapeDtypeStruct((3,) + p.shape[1:], p.dtype) for p in parts],
        in_specs=[_ANY] * n, out_specs=[_ANY] * (2 * n),
        scratch_shapes=[pltpu.SemaphoreType.DMA((3 * n,)), pltpu.SemaphoreType.DMA((3 * n,)), pltpu.SemaphoreType.DMA((n,))],
    )(*parts)


def share_sibling(halves, name):
    n = len(halves)

    def body(*refs):
        ins, outs = refs[:n], refs[n:2 * n]
        send_sems, recv_sems, local_sems = refs[2 * n:]
        x, y, c = _place()
        pend = []
        for a in range(n):
            loc = pltpu.make_async_copy(ins[a], outs[a].at[c], local_sems.at[a])
            cp = pltpu.make_async_remote_copy(
                src_ref=ins[a], dst_ref=outs[a].at[c], send_sem=send_sems.at[a], recv_sem=recv_sems.at[a],
                device_id=(x, y, 1 - c), device_id_type=MESH_ID)
            loc.start()
            cp.start()
            pend.append((loc, cp))
        for a, (loc, cp) in enumerate(pend):
            cp.wait_send()
            pltpu.make_async_remote_copy(
                src_ref=ins[a], dst_ref=outs[a].at[1 - c], send_sem=send_sems.at[a], recv_sem=recv_sems.at[a],
                device_id=(x, y, 1 - c), device_id_type=MESH_ID).wait_recv()
            loc.wait()

    return pl.pallas_call(
        body, name=name, out_shape=[jax.ShapeDtypeStruct((2,) + h.shape, h.dtype) for h in halves],
        in_specs=[_ANY] * n, out_specs=[_ANY] * n,
        scratch_shapes=[pltpu.SemaphoreType.DMA((n,)), pltpu.SemaphoreType.DMA((n,)), pltpu.SemaphoreType.DMA((n,))],
    )(*halves)


def reduce_scatter(grads, tag):
    mine, got = _split(rs_sibling(grads, name=f"rs_sibling_{tag}"))
    pair = [ew_sum([a.reshape(-1, a.shape[2]), b.reshape(-1, b.shape[2])], name=f"rs_pair_{tag}_{i}").reshape(a.shape)
            for i, (a, b) in enumerate(zip(mine, got))]
    own, recv = _split(rs_chips(pair, name=f"rs_chips_{tag}"))
    tot = [ew_sum([a, b[0], b[1], b[2]], name=f"rs_quad_{tag}_{i}") for i, (a, b) in enumerate(zip(own, recv))]
    both = share_sibling(tot, name=f"rs_share_{tag}")
    return [b.reshape(-1, b.shape[2]) for b in both]


def _split(lst):
    n = len(lst) // 2
    return lst[:n], lst[n:]


def _sibling():
    x, y, c = _place()
    return (x, y, 1 - c)


def send_rows(src, name):
    r, c = src.shape
    tr = _row_tile(r, c)
    n = r // tr

    def body(x_ref, out_ref, send_sem, recv_sem):
        i = pl.program_id(0)
        cp = pltpu.make_async_remote_copy(
            src_ref=x_ref, dst_ref=out_ref.at[pl.ds(pl.multiple_of(i * tr, 8), tr), :], send_sem=send_sem,
            recv_sem=recv_sem, device_id=_sibling(), device_id_type=MESH_ID)
        cp.start()
        cp.wait_send()

        @pl.when(i == n - 1)
        def _():
            pltpu.make_async_remote_copy(src_ref=out_ref, dst_ref=out_ref, send_sem=send_sem, recv_sem=recv_sem,
                                         device_id=_sibling(), device_id_type=MESH_ID).wait_recv()

    return pl.pallas_call(
        body, name=name, grid=(n,), in_specs=[pl.BlockSpec((tr, c), lambda i: (i, 0))], out_specs=_ANY,
        out_shape=jax.ShapeDtypeStruct((r, c), src.dtype),
        scratch_shapes=[pltpu.SemaphoreType.DMA, pltpu.SemaphoreType.DMA],
        compiler_params=pltpu.CompilerParams(dimension_semantics=("arbitrary",)),
    )(src)


def gather_chips(shards, name):
    n = len(shards)

    def body(*refs):
        ins, outs = refs[:n], refs[n:2 * n]
        send_sems, recv_sems = refs[2 * n:]
        x, y, c = _place()
        chips = _other_chips(x, y)
        sends = []
        for a in range(n):
            for j, chip in enumerate(chips):
                cp = pltpu.make_async_remote_copy(
                    src_ref=ins[a], dst_ref=outs[a].at[2 * x + y], send_sem=send_sems.at[3 * a + j],
                    recv_sem=recv_sems.at[3 * a + j], device_id=(*chip, c), device_id_type=MESH_ID)
                cp.start()
                sends.append(cp)
        for a in range(n):
            for j, chip in enumerate(chips):
                pltpu.make_async_remote_copy(
                    src_ref=ins[a], dst_ref=outs[a].at[2 * chip[0] + chip[1]], send_sem=send_sems.at[3 * a + j],
                    recv_sem=recv_sems.at[3 * a + j], device_id=(*chip, c), device_id_type=MESH_ID).wait_recv()
        for cp in sends:
            cp.wait_send()

    res = pl.pallas_call(
        body, name=name,
        out_shape=[jax.ShapeDtypeStruct((4,) + s.shape, s.dtype) for s in shards],
        in_specs=[_ANY] * n, out_specs=[_ANY] * n,
        scratch_shapes=[pltpu.SemaphoreType.DMA((3 * n,)), pltpu.SemaphoreType.DMA((3 * n,))],
    )(*shards)
    x, y, _ = _place()
    return [lax.dynamic_update_slice(g, s[None], (2 * x + y,) + (0,) * s.ndim) for g, s in zip(res, shards)]


def gather_chips(shards, name):
    n = len(shards)

    def body(*refs):
        ins, outs = refs[:n], refs[n:2 * n]
        send_sems, recv_sems = refs[2 * n:]
        x, y, c = _place()
        chips = _other_chips(x, y)
        sends = []
        for a in range(n):
            h = ins[a].shape[0] // 2
            src = ins[a].at[pl.ds(pl.multiple_of(c * h, 16), h), :]
            for j, chip in enumerate(chips):
                cp = pltpu.make_async_remote_copy(
                    src_ref=src, dst_ref=outs[a].at[2 * x + y], send_sem=send_sems.at[3 * a + j],
                    recv_sem=recv_sems.at[3 * a + j], device_id=(*chip, c), device_id_type=MESH_ID)
                cp.start()
                sends.append(cp)
        for a in range(n):
            h = ins[a].shape[0] // 2
            src = ins[a].at[pl.ds(pl.multiple_of(c * h, 16), h), :]
            for j, chip in enumerate(chips):
                pltpu.make_async_remote_copy(
                    src_ref=src, dst_ref=outs[a].at[2 * chip[0] + chip[1]], send_sem=send_sems.at[3 * a + j],
                    recv_sem=recv_sems.at[3 * a + j], device_id=(*chip, c), device_id_type=MESH_ID).wait_recv()
        for cp in sends:
            cp.wait_send()

    halves = pl.pallas_call(
        body, name=name,
        out_shape=[jax.ShapeDtypeStruct((4, s.shape[0] // 2, s.shape[1]), s.dtype) for s in shards],
        in_specs=[_ANY] * n, out_specs=[_ANY] * n,
        scratch_shapes=[pltpu.SemaphoreType.DMA((3 * n,)), pltpu.SemaphoreType.DMA((3 * n,))],
    )(*shards)
    x, y, ci = _place()
    res = []
    for a, (g, s) in enumerate(zip(halves, shards)):
        h = s.shape[0] // 2
        g = lax.dynamic_update_slice(g, lax.dynamic_slice_in_dim(s, ci * h, h, 0)[None], (2 * x + y, 0, 0))
        other = send_rows(g.reshape(4 * h, s.shape[1]), f"{name}_sib_{a}").reshape(g.shape)
        res.append(jnp.concatenate([jnp.where(ci == 0, g, other), jnp.where(ci == 0, other, g)], 1))
    return res


def rs_chips(parts, name):
    n = len(parts)

    def body(*refs):
        ins, got = refs[:n], refs[n:2 * n]
        send_sems, recv_sems = refs[2 * n:]
        x, y, c = _place()
        chips = _other_chips(x, y)
        pend = []
        for a in range(n):
            for j, chip in enumerate(chips):
                cp = pltpu.make_async_remote_copy(
                    src_ref=ins[a].at[2 * chip[0] + chip[1]], dst_ref=got[a].at[j],
                    send_sem=send_sems.at[3 * a + j], recv_sem=recv_sems.at[3 * a + j],
                    device_id=(*chip, c), device_id_type=MESH_ID)
                cp.start()
                pend.append(cp)
        for p in pend:
            p.wait()

    return pl.pallas_call(
        body, name=name,
        out_shape=[jax.ShapeDtypeStruct((3,) + p.shape[1:], p.dtype) for p in parts],
        in_specs=[_ANY] * n, out_specs=[_ANY] * n,
        scratch_shapes=[pltpu.SemaphoreType.DMA((3 * n,)), pltpu.SemaphoreType.DMA((3 * n,))],
    )(*parts)


def reduce_scatter(grads, tag):
    x, y, ci = _place()
    chip = 2 * x + y
    out = []
    pairs = []
    for i, g in enumerate(grads):
        _, r, c = g.shape
        h = r // 2
        keep = lax.dynamic_slice_in_dim(g, ci * h, h, 1).reshape(4 * h, c)
        give = lax.dynamic_slice_in_dim(g, (1 - ci) * h, h, 1).reshape(4 * h, c).astype(MXU_DTYPE)
        got = send_rows(give, f"rs_sibling_{tag}_{i}")
        pairs.append(ew_sum([keep, got], name=f"rs_pair_{tag}_{i}").reshape(4, h, c))
    recv = rs_chips([p.astype(MXU_DTYPE) for p in pairs], f"rs_chips_{tag}")
    for i, (p, b) in enumerate(zip(pairs, recv)):
        own = lax.dynamic_index_in_dim(p, chip, 0, keepdims=False)
        tot = ew_sum([own, b[0], b[1], b[2]], name=f"rs_quad_{tag}_{i}")
        other = send_rows(tot, f"rs_share_{tag}_{i}")
        h = tot.shape[0]
        out.append(lax.dynamic_update_slice(jnp.concatenate([other, other], 0), tot, (ci * h, 0)))
    return out


def sum_entries(g, idxs, name):
    _, rows, width = g.shape
    tr = _row_tile(rows, width)

    def body(g_ref, o_ref):
        acc = g_ref[idxs[0]]
        for d in idxs[1:]:
            acc = acc + g_ref[d]
        o_ref[...] = acc

    return pl.pallas_call(
        body, name=name, grid=(rows // tr,),
        in_specs=[pl.BlockSpec((8, tr, width), lambda i: (0, i, 0))], out_specs=pl.BlockSpec((tr, width), lambda i: (i, 0)),
        out_shape=jax.ShapeDtypeStruct((rows, width), F32),
        compiler_params=pltpu.CompilerParams(dimension_semantics=("parallel",)),
    )(g)


def _pack(arrs, rows_multiple=8):
    parts, offs, r = [], [], 0
    for a in arrs:
        flat = a.reshape(-1).astype(F32)
        nrow = -(-flat.shape[0] // LANES)
        parts.append(jnp.pad(flat, (0, nrow * LANES - flat.shape[0])).reshape(nrow, LANES))
        offs.append((r, nrow, a.shape))
        r += nrow
    pad = (-r) % rows_multiple
    if pad:
        parts.append(jnp.zeros((pad, LANES), F32))
    return jnp.concatenate(parts, 0), offs


def _unpack(slab, offs):
    outs = []
    for r, nrow, shape in offs:
        size = 1
        for s in shape:
            size *= s
        outs.append(slab[r:r + nrow].reshape(-1)[:size].reshape(shape))
    return outs


def _seqflip(a, lc):
    return jnp.concatenate([jnp.flip(a[:lc], 0), jnp.flip(a[lc:], 0)], 0)


def _slab(vec8):
    return jnp.pad(vec8.reshape(1, -1).astype(F32), ((0, 0), (0, LANES - vec8.size)))


def _rope_tables(n_lat, lc):
    rows = n_lat // GRID_W
    row = jnp.repeat(jnp.arange(rows, dtype=F32), GRID_W)
    col = jnp.tile(jnp.arange(GRID_W, dtype=F32), rows)
    n_freq = HEAD_DIM // 4
    inv = ROPE_THETA ** (-jnp.arange(n_freq, dtype=F32) / n_freq)
    ang = jnp.concatenate([row[:, None] * inv, col[:, None] * inv], -1)
    cos, sin = jnp.cos(ang), jnp.sin(ang)
    cos_t = jnp.concatenate([jnp.ones((lc, HEAD_DIM), F32), jnp.concatenate([cos, cos], -1)], 0)
    sin_t = jnp.concatenate([jnp.zeros((lc, HEAD_DIM), F32), jnp.concatenate([-sin, sin], -1)], 0)
    return cos_t, sin_t


def _permute_w_in(w):
    return jnp.concatenate([w[:, :4096], w[:, 4112:], w[:, 4096:4112], jnp.zeros((w.shape[0], PROJ_PAD - PROJ_W), w.dtype)], 1)


def _unpermute_dw_in(dw):
    return jnp.concatenate([dw[:, :4096], dw[:, 5632:5648], dw[:, 4096:5632]], 1)


RET_SELS = [[(d * RET_HEADS + h, d == 1) for h in range(RET_HEADS)] for d in range(2)]
DN_SELS = [[(d * DN_HEADS + h, (2 + d) * DN_HEADS + h, d == 1) for h in range(DN_HEADS)] for d in range(2)]


def _layer_fwd(xin, mods, wts, prm, tabs, lc, tm, i):
    nct = lc // tm
    seg = lambda j: mods[:, j:j + 1, :]
    cos_t, sin_t = tabs
    sv = {}
    (h1,) = rowwise(f"mod1_{i}", modulate_fn, [xin], [seg(0), seg(1)], [], [(D_MODEL, MXU_DTYPE)], tm, nct)
    p = mm(h1, wts["w_in"], "nn", name=f"proj_in_{i}")
    rq, rk = rowwise(f"ret_prep_{i}", ret_prep_fn, [(p, 512, 0), (p, 512, 1), cos_t, sin_t], [], [],
                     [(512, F32), (512, F32)], tm, nct)
    rv = p[:, 1024:1536]
    r_of, r_ob, r_st = scan_fwd(ret_chunk, RET_CHUNK, rq, rk, rv, prm["ret_logit"], RET_SELS, lc, f"ret_scan_{i}")
    (y_ret,) = rowwise(f"ret_out_{i}", ret_out_fn, [r_of, r_ob, (p, 512, 3)], [], [], [(512, F32)], tm, nct)
    qkvc = conv_fwd(p, 2048, 3 * 512, prm["conv_w"], lc, f"dn_conv_{i}")
    dq, dk, gb = rowwise(f"dn_prep_{i}", dn_prep_fn, [(qkvc, 512, 0), (qkvc, 512, 1), (p, LANES, 44)], [],
                         [prm["a_log"], prm["dt_b"]], [(512, F32), (512, F32), (LANES, F32)], tm, nct)
    dv = qkvc[:, 1024:1536]
    d_of, d_ob, d_st = scan_fwd(dn_chunk, DN_CHUNK, dq, dk, dv, gb, DN_SELS, lc, f"dn_scan_{i}")
    (y_dn,) = rowwise(f"dn_out_{i}", dn_out_fn, [d_of, d_ob, (p, 512, 7)], [], [prm["dn_norm_w"]], [(512, F32)], tm, nct)
    aq, ak, av = rowwise(f"att_prep_{i}", lambda q, k, v, *rest: (lambda qk: (qk[0] * (QK_SCALE * LOG2E), qk[1], v))(att_prep_fn(q, k, *rest)),
                         [(p, 1024, 4), (p, 256, 20), (p, 256, 21), cos_t, sin_t], [], [prm["qn_w"], prm["kn_w"]],
                         [(1024, MXU_DTYPE), (256, MXU_DTYPE), (256, MXU_DTYPE)], tm, nct)
    ao, lse = attn_fwd(aq, ak, av, lc, f"attn_fwd_{i}")
    y = jnp.concatenate([y_ret, y_dn, ao], 1)
    a1 = mm(y, wts["w_o"], "nn", name=f"proj_out_{i}")
    (x1,) = rowwise(f"postnorm1_{i}", postnorm_fn, [xin, a1], [seg(2)], [prm["ln1_w"], prm["ln1_b"]], [(D_MODEL, F32)], tm, nct)
    (h2,) = rowwise(f"mod2_{i}", modulate_fn, [x1], [seg(3), seg(4)], [], [(D_MODEL, MXU_DTYPE)], tm, nct)
    u = mm(h2, wts["w_ffn_in"], "nn", name=f"ffn_in_{i}")
    act = swiglu_fwd(u, tm, f"swiglu_{i}")
    a2 = mm(act, wts["w_ffn_out"], "nn", name=f"ffn_out_{i}")
    (x2,) = rowwise(f"postnorm2_{i}", postnorm_fn, [x1, a2], [seg(5)], [prm["ln2_w"], prm["ln2_b"]], [(D_MODEL, F32)], tm, nct)
    sv.update(xin=xin, h1=h1, p=p, rq=rq, rk=rk, rv=rv, r_st=r_st, r_of=r_of, r_ob=r_ob, qkvc=qkvc, dq=dq, dk=dk, dv=dv,
              gb=gb, d_st=d_st, d_of=d_of, d_ob=d_ob, aq=aq, ak=ak, av=av, ao=ao, lse=lse, y=y, a1=a1, x1=x1, h2=h2,
              u=u, act=act, a2=a2)
    return x2, sv


def _layer_bwd(dx2, sv, mods, wts, prm, tabs, lc, tm, i):
    nct = lc // tm
    seg = lambda j: mods[:, j:j + 1, :]
    cos_t, sin_t = tabs
    p = sv["p"]
    both = lambda g: g[0] + g[1]
    (dx1a, da2), (dgate2,), (dln2w, dln2b) = rowwise_bwd(
        f"postnorm2_b_{i}", postnorm_fn, [sv["x1"], sv["a2"]], [seg(5)], [prm["ln2_w"], prm["ln2_b"]], [dx2], [True, True], tm, nct)
    dact = mm(da2, wts["w_ffn_out"], "nt", name=f"ffn_out_dx_{i}")
    dw_ffn_out = mm(sv["act"], da2, "tn", name=f"ffn_out_dw_{i}")
    du = swiglu_bwd(sv["u"], dact, tm, f"swiglu_b_{i}")
    dh2 = mm(du, wts["w_ffn_in"], "nt", name=f"ffn_in_dx_{i}")
    dw_ffn_in = mm(sv["h2"], du, "tn", name=f"ffn_in_dw_{i}")
    (dx1b,), (dshift2, dscale2), _ = rowwise_bwd(
        f"mod2_b_{i}", modulate_fn, [sv["x1"]], [seg(3), seg(4)], [], [dh2], [True], tm, nct)
    dx1 = ew_sum([dx1a, dx1b], name=f"dx1_{i}")
    (dxa, da1), (dgate1,), (dln1w, dln1b) = rowwise_bwd(
        f"postnorm1_b_{i}", postnorm_fn, [sv["xin"], sv["a1"]], [seg(2)], [prm["ln1_w"], prm["ln1_b"]], [dx1], [True, True], tm, nct)
    dy = mm(da1, wts["w_o"], "nt", name=f"proj_out_dx_{i}")
    dw_o = mm(sv["y"], da1, "tn", name=f"proj_out_dw_{i}")
    dy_ret, dy_dn, dao = dy[:, :512], dy[:, 512:1024], dy[:, 1024:]
    daq, delta = attn_bwd_dq(sv["aq"], sv["ak"], sv["av"], sv["ao"], dao, sv["lse"], lc, f"attn_dq_{i}")
    dak, dav = attn_bwd_dkv(sv["aq"], sv["ak"], sv["av"], dao, sv["lse"], delta, lc, f"attn_dkv_{i}")
    (dp_aq, dp_ak), _, (dqn_w, dkn_w) = rowwise_bwd(
        f"att_prep_b_{i}", att_prep_fn, [(p, 1024, 4), (p, 256, 20), cos_t, sin_t], [], [prm["qn_w"], prm["kn_w"]],
        [daq, dak], [True, True, False, False], tm, nct)
    (dd_o, dp_z), _, (ddn_norm_w,) = rowwise_bwd(
        f"dn_out_b_{i}", dn_out_fn, [sv["d_of"], sv["d_ob"], (p, 512, 7)], [], [prm["dn_norm_w"]], [dy_dn], [True, False, True], tm, nct)
    dqf, dkf, dvf, dqb, dkb, dvb, dgbf, dgbb = scan_bwd(
        dn_chunk, DN_CHUNK, sv["dq"], sv["dk"], sv["dv"], sv["gb"], sv["d_st"], dd_o, DN_SELS, lc, f"dn_scan_b_{i}")
    ddv = ew_sum([dvf, dvb], name=f"dn_dv_{i}")
    (dqc, dkc, dp_ab), _, (da_log, ddt_b) = rowwise_bwd(
        f"dn_prep_b_{i}", dn_prep_fn, [(sv["qkvc"], 512, 0), (sv["qkvc"], 512, 1), (p, LANES, 44)], [],
        [prm["a_log"], prm["dt_b"]], [[dqf, dqb], [dkf, dkb], [dgbf, dgbb]], [True, True, True], tm, nct)
    dqkvc = jnp.concatenate([dqc, dkc, ddv], 1)
    dp_qkv, dconv_w = conv_bwd(p, 2048, 3 * 512, prm["conv_w"], dqkvc, lc, f"dn_conv_b_{i}")
    (dr_o, dp_g), _, _ = rowwise_bwd(
        f"ret_out_b_{i}", ret_out_fn, [sv["r_of"], sv["r_ob"], (p, 512, 3)], [], [], [dy_ret], [True, False, True], tm, nct)
    drqf, drkf, drvf, drqb, drkb, drvb, dlogit = scan_bwd(
        ret_chunk, RET_CHUNK, sv["rq"], sv["rk"], sv["rv"], prm["ret_logit"], sv["r_st"], dr_o, RET_SELS, lc, f"ret_scan_b_{i}")
    drv = ew_sum([drvf, drvb], name=f"ret_dv_{i}")
    dret_logit = dlogit[0, :2 * RET_HEADS].reshape(2, RET_HEADS)
    (dp_rq, dp_rk), _, _ = rowwise_bwd(
        f"ret_prep_b_{i}", ret_prep_fn, [(p, 512, 0), (p, 512, 1), cos_t, sin_t], [], [], [[drqf, drqb], [drkf, drkb]],
        [True, True, False, False], tm, nct)
    dp = jnp.concatenate([dp_rq, dp_rk, drv, dp_g, dp_qkv, dp_z, dp_aq, dp_ak, dav, dp_ab], 1)
    dh1 = mm(dp, wts["w_in"], "nt", name=f"proj_in_dx_{i}")
    dw_in = mm(sv["h1"], dp, "tn", name=f"proj_in_dw_{i}")
    (dxb,), (dshift1, dscale1), _ = rowwise_bwd(
        f"mod1_b_{i}", modulate_fn, [sv["xin"]], [seg(0), seg(1)], [], [dh1], [True], tm, nct)
    dxin = ew_sum([dxa, dxb], name=f"dxin_{i}")
    dmods = jnp.concatenate([dshift1, dscale1, dgate1, dshift2, dscale2, dgate2], 1)
    big = dict(w_in=dw_in, w_o=dw_o, w_ffn_in=dw_ffn_in, w_ffn_out=dw_ffn_out)
    small = dict(ln1_w=both(dln1w)[0], ln1_b=both(dln1b)[0], ln2_w=both(dln2w)[0], ln2_b=both(dln2b)[0],
                 dn_norm_w=both(ddn_norm_w)[0], att_qn_w=both(dqn_w)[0], att_kn_w=both(dkn_w)[0],
                 dn_conv_w=dconv_w[:DN_CONV_K], ret_decay_logit=dret_logit,
                 dn_a_log=both(da_log)[0, :2 * DN_HEADS].reshape(2, DN_HEADS),
                 dn_dt_bias=both(ddt_b)[0, :2 * DN_HEADS].reshape(2, DN_HEADS))
    return dxin, dmods, big, small


BIG = ("w_in", "w_o", "w_ffn_in", "w_ffn_out")
SMALL = ("c_ctx", "b_ada", "ret_decay_logit", "dn_conv_w", "dn_a_log", "dn_dt_bias", "dn_norm_w", "att_qn_w", "att_kn_w",
         "ln1_w", "ln1_b", "ln2_w", "ln2_b")
WEIGHTS = ("c_ctx", "w_ada", "b_ada", "w_in", "ret_decay_logit", "dn_conv_w", "dn_a_log", "dn_dt_bias", "dn_norm_w",
           "att_qn_w", "att_kn_w", "w_o", "ln1_w", "ln1_b", "w_ffn_in", "w_ffn_out", "ln2_w", "ln2_b")


def _chip_major(g, name):
    if name in ("w_in", "w_ffn_in"):
        r, cols = g.shape
        return g.reshape(r, 4, cols // 4).transpose(1, 0, 2)
    return g.reshape(4, g.shape[0] // 4, g.shape[1])


def kernel(x, c, ctx, c_ctx, w_ada, b_ada, w_in, ret_decay_logit, dn_conv_w, dn_a_log, dn_dt_bias, dn_norm_w, att_qn_w, att_kn_w, w_o, ln1_w, ln1_b, w_ffn_in, w_ffn_out, ln2_w, ln2_b, loss_target, m_c_ctx, m_w_ada, m_b_ada, m_w_in, m_ret_decay_logit, m_dn_conv_w, m_dn_a_log, m_dn_dt_bias, m_dn_norm_w, m_att_qn_w, m_att_kn_w, m_w_o, m_ln1_w, m_ln1_b, m_w_ffn_in, m_w_ffn_out, m_ln2_w, m_ln2_b, v_c_ctx, v_w_ada, v_b_ada, v_w_in, v_ret_decay_logit, v_dn_conv_w, v_dn_a_log, v_dn_dt_bias, v_dn_norm_w, v_att_qn_w, v_att_kn_w, v_w_o, v_ln1_w, v_ln1_b, v_w_ffn_in, v_w_ffn_out, v_ln2_w, v_ln2_b):
    wv = dict(c_ctx=c_ctx, w_ada=w_ada, b_ada=b_ada, w_in=w_in, ret_decay_logit=ret_decay_logit, dn_conv_w=dn_conv_w,
              dn_a_log=dn_a_log, dn_dt_bias=dn_dt_bias, dn_norm_w=dn_norm_w, att_qn_w=att_qn_w, att_kn_w=att_kn_w, w_o=w_o,
              ln1_w=ln1_w, ln1_b=ln1_b, w_ffn_in=w_ffn_in, w_ffn_out=w_ffn_out, ln2_w=ln2_w, ln2_b=ln2_b)
    mv = dict(c_ctx=m_c_ctx, w_ada=m_w_ada, b_ada=m_b_ada, w_in=m_w_in, ret_decay_logit=m_ret_decay_logit,
              dn_conv_w=m_dn_conv_w, dn_a_log=m_dn_a_log, dn_dt_bias=m_dn_dt_bias, dn_norm_w=m_dn_norm_w,
              att_qn_w=m_att_qn_w, att_kn_w=m_att_kn_w, w_o=m_w_o, ln1_w=m_ln1_w, ln1_b=m_ln1_b, w_ffn_in=m_w_ffn_in,
              w_ffn_out=m_w_ffn_out, ln2_w=m_ln2_w, ln2_b=m_ln2_b)
    vv = dict(c_ctx=v_c_ctx, w_ada=v_w_ada, b_ada=v_b_ada, w_in=v_w_in, ret_decay_logit=v_ret_decay_logit,
              dn_conv_w=v_dn_conv_w, dn_a_log=v_dn_a_log, dn_dt_bias=v_dn_dt_bias, dn_norm_w=v_dn_norm_w,
              att_qn_w=v_att_qn_w, att_kn_w=v_att_kn_w, w_o=v_w_o, ln1_w=v_ln1_w, ln1_b=v_ln1_b, w_ffn_in=v_w_ffn_in,
              w_ffn_out=v_w_ffn_out, ln2_w=v_ln2_w, ln2_b=v_ln2_b)
    depth = w_in.shape[0]
    n_lat, lc = x.shape[1], ctx.shape[1]
    t = lc + n_lat
    tm = _pick(lc, (256, 128))
    xi, yi, ci = _place()
    bidx = 4 * xi + 2 * yi + ci
    chip = 2 * xi + yi
    ada_w = w_ada.shape[2]
    conv_sh = dn_conv_w.shape[2]

    slab0, offs0 = _pack([c, dn_conv_w])
    g0 = allgather8(slab0, "gather_cond").reshape(8, -1, LANES)
    c_all = jnp.concatenate([_unpack(g0[d], offs0)[0] for d in range(8)], 0)
    conv_full = jnp.concatenate([_unpack(g0[2 * k], offs0)[1] for k in range(4)], 2)
    c_raw = jnp.concatenate([c_all, c_ctx[None], jnp.zeros((LANES - 9, D_MODEL), F32)], 0)
    (cond,) = rowwise("cond_silu", lambda a: (_silu(a),), [c_raw], [], [], [(D_MODEL, F32)], LANES, 0)
    b_sh = lax.dynamic_slice(b_ada, (0, chip * ada_w), (depth, ada_w))
    mods_sh = []
    for i in range(depth):
        mi = mm(cond, w_ada[i], "nn", name=f"ada_{i}")
        (mi,) = rowwise(f"ada_bias_{i}", lambda a, b: (a + b,), [mi], [], [b_sh[i:i + 1]], [(ada_w, F32)], LANES, 0)
        mods_sh.append(mi[:16])
    slab1, offs1 = _pack([jnp.stack(mods_sh)])
    g1 = allgather8(slab1, "gather_mods").reshape(8, -1, LANES)
    mods_all = jnp.concatenate([_unpack(g1[2 * k], offs1)[0] for k in range(4)], 2)
    mod_lat = lax.dynamic_index_in_dim(mods_all, bidx, 1, keepdims=False)
    mod_ctx = mods_all[:, 8]
    mods = jnp.stack([mod_ctx, mod_lat], 1).reshape(depth, 2, 6, D_MODEL)

    wbf = {n: ew_sum([wv[n].reshape(-1, wv[n].shape[2])], name=f"cast_{n}", out_dtype=MXU_DTYPE).reshape(wv[n].shape) for n in BIG}
    layers_w = []
    for i in range(depth):
        g = gather_chips([wbf[n][i] for n in BIG], f"gather_w_{i}")
        gw = dict(zip(BIG, g))
        layers_w.append(dict(
            w_in=_permute_w_in(gw["w_in"].transpose(1, 0, 2).reshape(D_MODEL, PROJ_W)),
            w_o=gw["w_o"].reshape(D_MODEL, D_MODEL),
            w_ffn_in=gw["w_ffn_in"].transpose(1, 0, 2).reshape(D_MODEL, 2 * D_FF),
            w_ffn_out=gw["w_ffn_out"].reshape(D_FF, D_MODEL)))

    tabs = _rope_tables(n_lat, lc)
    prms = []
    for i in range(depth):
        prms.append(dict(
            ret_logit=_slab(ret_decay_logit[i].reshape(-1)), conv_w=conv_full[i],
            a_log=_slab(dn_a_log[i].reshape(-1)), dt_b=_slab(dn_dt_bias[i].reshape(-1)), dn_norm_w=dn_norm_w[i:i + 1],
            qn_w=att_qn_w[i:i + 1], kn_w=att_kn_w[i:i + 1], ln1_w=ln1_w[i:i + 1], ln1_b=ln1_b[i:i + 1],
            ln2_w=ln2_w[i:i + 1], ln2_b=ln2_b[i:i + 1]))

    rows = jnp.concatenate([ctx[0], x[0]], 0)
    saved = []
    for i in range(depth):
        rows, sv = _layer_fwd(rows, mods[i], layers_w[i], prms[i], tabs, lc, tm, i)
        saved.append(sv)
    loss_local, dy = loss_and_grad(rows[lc:], loss_target[0], tm, "loss")
    loss = lax.psum(loss_local[0, 0], ("x", "y", "c"))

    drows = jnp.concatenate([jnp.zeros((lc, D_MODEL), F32), dy], 0)
    dmods, big_g, small_g = [None] * depth, [None] * depth, [None] * depth
    for i in reversed(range(depth)):
        drows, dmods[i], big, small_g[i] = _layer_bwd(drows, saved[i], mods[i], layers_w[i], prms[i], tabs, lc, tm, i)
        big["w_in"] = _unpermute_dw_in(big["w_in"])
        big_g[i] = dict(zip(BIG, reduce_scatter([_chip_major(big[n], n) for n in BIG], str(i))))
    grad_x = drows[lc:][None]

    names = ("ln1_w", "ln1_b", "ln2_w", "ln2_b", "dn_norm_w", "att_qn_w", "att_kn_w", "dn_conv_w", "ret_decay_logit",
             "dn_a_log", "dn_dt_bias")
    loc = [jnp.stack(dmods)] + [jnp.stack([small_g[i][n] for i in range(depth)]) for n in names]
    slab2, offs2 = _pack(loc)
    g2 = allgather8(slab2, "gather_small").reshape(8, -1, LANES)
    tot = _unpack(sum_entries(g2, tuple(range(8)), "sum_small"), offs2)
    dmods_sum = tot[0]
    gsm = dict(zip(names, tot[1:]))
    dmod_lat = jnp.stack([_unpack(g2[d], offs2)[0][:, 1].reshape(depth, 6 * D_MODEL) for d in range(8)], 1)
    dmod_ctx = dmods_sum[:, 0].reshape(depth, 1, 6 * D_MODEL)
    dm = jnp.concatenate([dmod_lat, dmod_ctx, jnp.zeros((depth, LANES - 9, 6 * D_MODEL), F32)], 1)
    dm_sh = lax.dynamic_slice(dm, (0, 0, chip * ada_w), (depth, LANES, ada_w))
    g_w_ada = jnp.stack([mm(cond, dm_sh[i], "tn", name=f"ada_dw_{i}") for i in range(depth)])
    gsm["b_ada"] = ew_sum([dmods_sum[:, 0].reshape(-1, LANES), dmods_sum[:, 1].reshape(-1, LANES)], name="b_ada_sum").reshape(b_ada.shape)
    dctx_rows = jnp.concatenate([dm_sh[:, 8:9], jnp.zeros((depth, 15, ada_w), F32)], 1)
    part = ew_sum([mm(dctx_rows[i], w_ada[i], "nt", name=f"ada_dcond_{i}") for i in range(depth)], name="ada_dcond_sum")[0]
    slab3, offs3 = _pack([part])
    g3 = allgather8(slab3, "gather_dcond").reshape(8, -1, LANES)
    dcond_ctx = _unpack(sum_entries(g3, (0, 2, 4, 6), "sum_dcond"), offs3)[0]
    (dc_ctx,), _, _ = rowwise_bwd("c_ctx_silu_b", lambda a: (_silu(a),), [c_ctx.reshape(16, LANES)], [], [],
                                  [dcond_ctx.reshape(16, LANES)], [True], 16, 0)
    gsm["c_ctx"] = dc_ctx.reshape(c_ctx.shape)
    gsm["dn_conv_w"] = lax.dynamic_slice(gsm["dn_conv_w"], (0, 0, chip * conv_sh), (depth, DN_CONV_K, conv_sh))

    grads, delta, new_m, new_v = {}, {}, {}, {}
    gs, offs = _pack([gsm[n] for n in SMALL])
    ws, _ = _pack([wv[n] for n in SMALL])
    ms, _ = _pack([mv[n] for n in SMALL])
    vs, _ = _pack([vv[n] for n in SMALL])
    res = [_unpack(o, offs) for o in adamw(gs, ws, ms, vs, "adamw_small")]
    for j, n in enumerate(SMALL):
        grads[n], delta[n], new_m[n], new_v[n] = gsm[n], res[0][j], res[1][j], res[2][j]
    bigs = {n: jnp.stack([big_g[i][n] for i in range(depth)]) for n in BIG}
    bigs["w_ada"] = g_w_ada
    for n, g in bigs.items():
        shp = wv[n].shape
        flat = lambda a: a.reshape(-1, shp[2])
        d_, m_, v_ = adamw(flat(g), flat(wv[n]), flat(mv[n]), flat(vv[n]), f"adamw_{n}")
        grads[n], delta[n], new_m[n], new_v[n] = g.reshape(shp), d_.reshape(shp), m_.reshape(shp), v_.reshape(shp)
    return (loss, grad_x, *[grads[n] for n in WEIGHTS], *[delta[n] for n in WEIGHTS], *[new_m[n] for n in WEIGHTS],
            *[new_v[n] for n in WEIGHTS])
```

```python
import functools

import jax
import jax.numpy as jnp
from jax import lax
from jax.experimental import pallas as pl
from jax.experimental.pallas import tpu as pltpu

F32 = jnp.float32
MXU_DTYPE = jnp.bfloat16

D_MODEL = 2048
DEPTH = 4
GRID_W = 64
HEAD_DIM = 128
RET_HEADS = 4
DN_HEADS = 4
ATT_HEADS = 8
ATT_KV_HEADS = 2
RET_CHUNK = 128
DN_CHUNK = 64
DN_CONV_K = 5
ROPE_THETA = 10000.0
D_FF = 5632
PROJ_W = 5648
PROJ_PAD = 5760
DEEPNORM_ALPHA = (2 * DEPTH) ** 0.25
EPS = 1e-6
QK_SCALE = HEAD_DIM ** -0.5
ADAM_LR = 0.001
ADAM_B1 = 0.9
ADAM_B2 = 0.999
ADAM_EPS = 1e-08
ADAM_WD = 0.01
ADAM_STEP = 10
LANES = 128
MM_VMEM_BUDGET = 44 * 1024 * 1024
MESH_ID = pl.DeviceIdType.MESH
NEG_BIG = -1e30


def _pick(n, prefs):
    for p in prefs:
        if n % p == 0:
            return p
    return n


def _contract(a, b, ca, cb, hp):
    dims = (((ca,), (cb,)), ((), ()))
    if hp and MXU_DTYPE == F32:
        return lax.dot_general(a, b, dims, precision=lax.Precision.HIGHEST, preferred_element_type=F32)
    if hp:
        ah, bh = a.astype(MXU_DTYPE), b.astype(MXU_DTYPE)
        al = (a - ah.astype(F32)).astype(MXU_DTYPE)
        bl = (b - bh.astype(F32)).astype(MXU_DTYPE)
        dot = lambda u, v: lax.dot_general(u, v, dims, preferred_element_type=F32)
        return dot(ah, bh) + (dot(ah, bl) + dot(al, bh))
    return lax.dot_general(a.astype(MXU_DTYPE), b.astype(MXU_DTYPE), dims, preferred_element_type=F32)


@functools.partial(jax.custom_vjp, nondiff_argnums=(2, 3, 4))
def mdot(a, b, ca, cb, hp=False):
    return _contract(a, b, ca, cb, hp)


def _mdot_fwd(a, b, ca, cb, hp):
    return _contract(a, b, ca, cb, hp), (a, b)


def _mdot_bwd(ca, cb, hp, res, g):
    a, b = res
    da = mdot(g, b, 1, 1 - cb, hp) if ca == 1 else mdot(b, g, 1 - cb, 1, hp)
    db = mdot(a, g, 1 - ca, 0, hp) if cb == 0 else mdot(g, a, 0, 1 - ca, hp)
    return da, db


mdot.defvjp(_mdot_fwd, _mdot_bwd)


@jax.custom_vjp
def swap_halves(x):
    return pltpu.roll(x, HEAD_DIM // 2, 1)


def _swap_fwd(x):
    return swap_halves(x), None


def _swap_bwd(_, g):
    return (swap_halves(g),)


swap_halves.defvjp(_swap_fwd, _swap_bwd)


@jax.custom_vjp
def tri_inv(a):
    n = a.shape[0]
    r = lax.broadcasted_iota(jnp.int32, (n, n), 0)
    c = lax.broadcasted_iota(jnp.int32, (n, n), 1)
    eye = (r == c).astype(F32)
    p = -a
    t = eye + p
    k = 2
    while k < n:
        p = _contract(p, p, 1, 0, True)
        t = t + _contract(t, p, 1, 0, True)
        k *= 2
    return t


def _tri_inv_fwd(a):
    t = tri_inv(a)
    return t, t


def _tri_inv_bwd(t, g):
    return (-_contract(_contract(t, g, 0, 0, True), t, 1, 1, True),)


tri_inv.defvjp(_tri_inv_fwd, _tri_inv_bwd)


def _sigmoid(x):
    return 1.0 / (1.0 + jnp.exp(-x))


def _silu(x):
    return x * _sigmoid(x)


def _softplus(x):
    return jnp.maximum(x, 0.0) + jnp.log(1.0 + jnp.exp(-jnp.abs(x)))


def _lane_pick(slab, idx):
    lane = lax.broadcasted_iota(jnp.int32, slab.shape, 1)
    return jnp.sum(jnp.where(lane == idx, slab, 0.0), axis=1, keepdims=True)


def _heads(x):
    return [x[:, h * HEAD_DIM:(h + 1) * HEAD_DIM] for h in range(x.shape[1] // HEAD_DIM)]


def _rope(x, cos, sin):
    return x * cos + swap_halves(x) * sin


def _rms(x):
    return x * lax.rsqrt(jnp.mean(x * x, -1, keepdims=True) + EPS)


def _l2n(x):
    return x * lax.rsqrt(jnp.sum(x * x, -1, keepdims=True) + EPS)


def mm(a, b, mode, out_dtype=F32, name="mm"):
    if mode == "nn":
        (m, k), (k2, n) = a.shape, b.shape
        ca, cb = 1, 0
    elif mode == "nt":
        (m, k), (n, k2) = a.shape, b.shape
        ca, cb = 1, 1
    else:
        (k, m), (k2, n) = a.shape, b.shape
        ca, cb = 0, 0
    assert k == k2, (a.shape, b.shape, mode)
    tm = _pick(m, (768, 512, 384, 256, 128))
    tn = _pick(n, (1024, 768, 640, 512, 384, 256, 128))
    sa, sb, so = a.dtype.itemsize, b.dtype.itemsize, jnp.dtype(out_dtype).itemsize
    for tk in [k] + [d for d in (4224, 2816, 2048, 1920, 1408, 1152, 1024, 768, 640, 512, 384, 256, 128) if d < k and k % d == 0]:
        if 2 * (tm * tk * sa + tk * tn * sb) + 2 * tm * tn * so + (tm * tn * 4 if tk < k else 0) <= MM_VMEM_BUDGET:
            break
    nk = k // tk
    a_spec = pl.BlockSpec((tm, tk), lambda i, j, q: (i, q)) if ca == 1 else pl.BlockSpec((tk, tm), lambda i, j, q: (q, i))
    b_spec = pl.BlockSpec((tk, tn), lambda i, j, q: (q, j)) if cb == 0 else pl.BlockSpec((tn, tk), lambda i, j, q: (j, q))

    def body(a_ref, b_ref, o_ref, *scratch):
        d = _contract(a_ref[...], b_ref[...], ca, cb, False)
        if nk == 1:
            o_ref[...] = d.astype(o_ref.dtype)
            return
        acc = scratch[0]
        q = pl.program_id(2)

        @pl.when(q == 0)
        def _():
            acc[...] = d

        @pl.when(jnp.logical_and(q > 0, q < nk - 1))
        def _():
            acc[...] += d

        @pl.when(q == nk - 1)
        def _():
            o_ref[...] = (acc[...] + d).astype(o_ref.dtype)

    return pl.pallas_call(
        body, name=name, grid=(m // tm, n // tn, nk),
        in_specs=[a_spec, b_spec], out_specs=pl.BlockSpec((tm, tn), lambda i, j, q: (i, j)),
        out_shape=jax.ShapeDtypeStruct((m, n), out_dtype),
        scratch_shapes=[pltpu.VMEM((tm, tn), F32)] if nk > 1 else [],
        compiler_params=pltpu.CompilerParams(dimension_semantics=("parallel", "parallel", "arbitrary")),
    )(a, b)


def _row_spec(r, tm):
    if isinstance(r, tuple):
        arr, width, blk = r
        return arr, pl.BlockSpec((tm, width), lambda i, blk=blk: (i, blk))
    return r, pl.BlockSpec((tm, r.shape[1]), lambda i: (i, 0))


def _seg_spec(s, nct):
    return pl.BlockSpec((1, 1, s.shape[2]), lambda i: (jnp.where(i < nct, 0, 1), 0, 0))


def _full_spec(s):
    return pl.BlockSpec(s.shape, lambda i: (0,) * s.ndim)


def rowwise(name, fn, rows, segs, shared, outs, tm, nct):
    arrs, specs = zip(*[_row_spec(r, tm) for r in rows])
    t = arrs[0].shape[0]
    nr, ns = len(rows), len(segs)

    def body(*refs):
        vals = [r[...] for r in refs[:nr]] + [r[0] for r in refs[nr:nr + ns]] + [r[...] for r in refs[nr + ns:nr + ns + len(shared)]]
        res = fn(*vals)
        for o_ref, v in zip(refs[nr + ns + len(shared):], res):
            o_ref[...] = v.astype(o_ref.dtype)

    res = pl.pallas_call(
        body, name=name, grid=(t // tm,),
        in_specs=list(specs) + [_seg_spec(s, nct) for s in segs] + [_full_spec(s) for s in shared],
        out_specs=[pl.BlockSpec((tm, w), lambda i: (i, 0)) for w, _ in outs],
        out_shape=[jax.ShapeDtypeStruct((t, w), d) for w, d in outs],
        compiler_params=pltpu.CompilerParams(dimension_semantics=("parallel",)),
    )(*arrs, *segs, *shared)
    return res


def rowwise_bwd(name, fn, rows, segs, shared, cts, want, tm, nct):
    arrs, specs = zip(*[_row_spec(r, tm) for r in rows])
    t = arrs[0].shape[0]
    groups = [list(c) if isinstance(c, (list, tuple)) else [c] for c in cts]
    cts = [c for g in groups for c in g]
    nr, ns, nsh, nc = len(rows), len(segs), len(shared), len(cts)
    widths = [sp.block_shape[1] for sp in specs]
    wanted = [i for i in range(nr) if want[i]]

    def body(*refs):
        i = pl.program_id(0)
        ins = refs[:nr + ns + nsh]
        ct_refs = refs[nr + ns + nsh:nr + ns + nsh + nc]
        o_refs = refs[nr + ns + nsh + nc:]
        vals = [r[...] for r in ins[:nr]] + [r[0] for r in ins[nr:nr + ns]] + [r[...] for r in ins[nr + ns:]]
        _, vjp = jax.vjp(fn, *vals)
        ct_vals, pos = [], 0
        for g in groups:
            tot = ct_refs[pos][...]
            for c_ref in ct_refs[pos + 1:pos + len(g)]:
                tot = tot + c_ref[...]
            ct_vals.append(tot)
            pos += len(g)
        grads = vjp(tuple(ct_vals))
        for o_ref, idx in zip(o_refs[:len(wanted)], wanted):
            o_ref[...] = grads[idx].astype(o_ref.dtype)
        first = jnp.logical_or(i == 0, i == nct)
        for o_ref, g in zip(o_refs[len(wanted):], grads[nr:]):
            @pl.when(first)
            def _(o_ref=o_ref):
                o_ref[...] = jnp.zeros_like(o_ref)

            o_ref[0] += g

    seg_idx = lambda i: jnp.where(i < nct, 0, 1)
    out_specs = [pl.BlockSpec((tm, widths[idx]), lambda i: (i, 0)) for idx in wanted]
    out_shape = [jax.ShapeDtypeStruct((t, widths[idx]), F32) for idx in wanted]
    for s in segs:
        out_specs.append(pl.BlockSpec((1, 1, s.shape[2]), lambda i: (seg_idx(i), 0, 0)))
        out_shape.append(jax.ShapeDtypeStruct((2, 1, s.shape[2]), F32))
    for s in shared:
        out_specs.append(pl.BlockSpec((1,) + s.shape, lambda i, nd=s.ndim: (seg_idx(i),) + (0,) * nd))
        out_shape.append(jax.ShapeDtypeStruct((2,) + s.shape, F32))
    res = pl.pallas_call(
        body, name=name, grid=(t // tm,),
        in_specs=list(specs) + [_seg_spec(s, nct) for s in segs] + [_full_spec(s) for s in shared]
        + [pl.BlockSpec((tm, c.shape[1]), lambda i: (i, 0)) for c in cts],
        out_specs=out_specs, out_shape=out_shape,
        compiler_params=pltpu.CompilerParams(dimension_semantics=("arbitrary",)),
    )(*arrs, *segs, *shared, *cts)
    nw = len(wanted)
    return res[:nw], res[nw:nw + ns], res[nw + ns:]


def modulate_fn(x, shift, scale):
    return (x * (1.0 + scale) + shift,)


def postnorm_fn(x, a, gate, w, b):
    z = DEEPNORM_ALPHA * x + gate * a
    mu = jnp.mean(z, -1, keepdims=True)
    zc = z - mu
    var = jnp.mean(zc * zc, -1, keepdims=True)
    return (zc * lax.rsqrt(var + EPS) * w + b,)


def ret_prep_fn(q, k, cos, sin):
    qs = [_rope(x, cos, sin) for x in _heads(q)]
    ks = [_rope(x, cos, sin) * QK_SCALE for x in _heads(k)]
    return jnp.concatenate(qs, 1), jnp.concatenate(ks, 1)


def ret_out_fn(of, ob, gate):
    ys = [_rms(o) * _silu(g) for o, g in zip(_heads(of + ob), _heads(gate))]
    return (jnp.concatenate(ys, 1),)


def dn_prep_fn(q, k, ab, a_log, dt_b):
    qs = [_l2n(x) * QK_SCALE for x in _heads(q)]
    ks = [_l2n(x) for x in _heads(k)]
    lane = lax.broadcasted_iota(jnp.int32, ab.shape, 1)
    g = -jnp.exp(a_log) * _softplus(ab + dt_b)
    beta = _sigmoid(ab)
    gb = jnp.where(lane < 2 * DN_HEADS, g, jnp.where(lane < 4 * DN_HEADS, beta, 0.0))
    return jnp.concatenate(qs, 1), jnp.concatenate(ks, 1), gb


def dn_out_fn(of, ob, z, w):
    ys = [_rms(o) * w * _silu(g) for o, g in zip(_heads(of + ob), _heads(z))]
    return (jnp.concatenate(ys, 1),)


def att_prep_fn(q, k, cos, sin, qw, kw):
    qs = [_rope(_rms(x) * qw, cos, sin) for x in _heads(q)]
    ks = [_rope(_rms(x) * kw, cos, sin) for x in _heads(k)]
    return jnp.concatenate(qs, 1), jnp.concatenate(ks, 1)


def swiglu_fwd(u, tm, name):
    t, w2 = u.shape
    w = w2 // 2
    cw = _pick(w, (512, 256, 128))
    ncb = w // cw

    def body(g_ref, u_ref, o_ref):
        o_ref[...] = (_silu(g_ref[...].astype(F32)) * u_ref[...].astype(F32)).astype(o_ref.dtype)

    return pl.pallas_call(
        body, name=name, grid=(t // tm, ncb),
        in_specs=[pl.BlockSpec((tm, cw), lambda i, j: (i, j)), pl.BlockSpec((tm, cw), lambda i, j: (i, j + ncb))],
        out_specs=pl.BlockSpec((tm, cw), lambda i, j: (i, j)),
        out_shape=jax.ShapeDtypeStruct((t, w), MXU_DTYPE),
        compiler_params=pltpu.CompilerParams(dimension_semantics=("parallel", "parallel")),
    )(u, u)


def swiglu_bwd(u, dact, tm, name):
    t, w2 = u.shape
    w = w2 // 2
    cw = _pick(w, (512, 256, 128))
    ncb = w // cw

    def body(g_ref, u_ref, d_ref, o_ref):
        j = pl.program_id(1)
        g = g_ref[...].astype(F32)
        s = _sigmoid(g)
        d = d_ref[...]
        dg = d * u_ref[...].astype(F32) * (s * (1.0 + g * (1.0 - s)))
        du = d * g * s
        o_ref[...] = jnp.where(j < ncb, dg, du).astype(o_ref.dtype)

    return pl.pallas_call(
        body, name=name, grid=(t // tm, 2 * ncb),
        in_specs=[pl.BlockSpec((tm, cw), lambda i, j: (i, j % ncb)), pl.BlockSpec((tm, cw), lambda i, j: (i, j % ncb + ncb)),
                  pl.BlockSpec((tm, cw), lambda i, j: (i, j % ncb))],
        out_specs=pl.BlockSpec((tm, cw), lambda i, j: (i, j)),
        out_shape=jax.ShapeDtypeStruct((t, w2), MXU_DTYPE),
        compiler_params=pltpu.CompilerParams(dimension_semantics=("parallel", "parallel")),
    )(u, u, dact)


CONV_HALO = 8


def _conv_mask(t, lc, s):
    r = lax.broadcasted_iota(jnp.int32, (t, 1), 0)
    src = r + s
    return jnp.logical_and(jnp.logical_and(src >= 0, src < t), (r < lc) == (src < lc))


def _conv_taps(pad_ref, w_ref, t, lc, flip):
    acc = None
    for k in range(DN_CONV_K):
        s = k - DN_CONV_K // 2
        off = -s if flip else s
        tap = pad_ref[pl.ds(CONV_HALO + off, t), :]
        term = jnp.where(_conv_mask(t, lc, off), tap, 0.0) * w_ref[k:k + 1, :]
        acc = term if acc is None else acc + term
    return acc


def _fill_pad(pad_ref, val, t):
    pad_ref[pl.ds(0, CONV_HALO), :] = jnp.zeros((CONV_HALO, LANES), F32)
    pad_ref[pl.ds(CONV_HALO + t, CONV_HALO), :] = jnp.zeros((CONV_HALO, LANES), F32)
    pad_ref[pl.ds(CONV_HALO, t), :] = val


def conv_fwd(p, col0, width, w, lc, name):
    t = p.shape[0]
    b0 = col0 // LANES

    def body(x_ref, w_ref, o_ref, pad):
        _fill_pad(pad, x_ref[...], t)
        o_ref[...] = _silu(_conv_taps(pad, w_ref, t, lc, False))

    return pl.pallas_call(
        body, name=name, grid=(width // LANES,),
        in_specs=[pl.BlockSpec((t, LANES), lambda j: (0, j + b0)), pl.BlockSpec((DN_CONV_K, LANES), lambda j: (0, j))],
        out_specs=pl.BlockSpec((t, LANES), lambda j: (0, j)),
        out_shape=jax.ShapeDtypeStruct((t, width), F32),
        scratch_shapes=[pltpu.VMEM((t + 2 * CONV_HALO, LANES), F32)],
        compiler_params=pltpu.CompilerParams(dimension_semantics=("parallel",)),
    )(p, w)


def conv_bwd(p, col0, width, w, dout, lc, name):
    t = p.shape[0]
    b0 = col0 // LANES

    def body(x_ref, w_ref, d_ref, dx_ref, dw_ref, pad):
        _fill_pad(pad, x_ref[...], t)
        y = _conv_taps(pad, w_ref, t, lc, False)
        sg = _sigmoid(y)
        dy = d_ref[...] * (sg * (1.0 + y * (1.0 - sg)))
        krow = lax.broadcasted_iota(jnp.int32, (8, LANES), 0)
        dw = jnp.zeros((8, LANES), F32)
        for k in range(DN_CONV_K):
            s = k - DN_CONV_K // 2
            tap = pad[pl.ds(CONV_HALO + s, t), :]
            dw_k = jnp.sum(jnp.where(_conv_mask(t, lc, s), tap, 0.0) * dy, axis=0, keepdims=True)
            dw = dw + jnp.where(krow == k, dw_k, 0.0)
        dw_ref[...] = dw
        _fill_pad(pad, dy, t)
        dx_ref[...] = _conv_taps(pad, w_ref, t, lc, True)

    return pl.pallas_call(
        body, name=name, grid=(width // LANES,),
        in_specs=[pl.BlockSpec((t, LANES), lambda j: (0, j + b0)), pl.BlockSpec((DN_CONV_K, LANES), lambda j: (0, j)),
                  pl.BlockSpec((t, LANES), lambda j: (0, j))],
        out_specs=[pl.BlockSpec((t, LANES), lambda j: (0, j)), pl.BlockSpec((8, LANES), lambda j: (0, j))],
        out_shape=[jax.ShapeDtypeStruct((t, width), F32), jax.ShapeDtypeStruct((8, width), F32)],
        scratch_shapes=[pltpu.VMEM((t + 2 * CONV_HALO, LANES), F32)],
        compiler_params=pltpu.CompilerParams(dimension_semantics=("parallel",)),
    )(p, w, dout)


def ret_chunk(q, k, v, s, logit_slab, h):
    c = q.shape[0]
    lg = -_softplus(-_lane_pick(logit_slab, h))
    i = lax.broadcasted_iota(jnp.int32, (c, c), 0)
    j = lax.broadcasted_iota(jnp.int32, (c, c), 1)
    rel = (i - j).astype(F32)
    decay = jnp.where(i >= j, jnp.exp(jnp.maximum(rel, 0.0) * lg), 0.0)
    pos = lax.broadcasted_iota(jnp.int32, (c, 1), 0).astype(F32)
    q_decay = jnp.exp((pos + 1.0) * lg)
    k_decay = jnp.exp((c - 1.0 - pos) * lg)
    intra = mdot(q, k, 1, 1) * decay
    o = mdot(intra, v, 1, 0) + mdot(q * q_decay, s, 1, 0)
    s_new = s * jnp.exp(c * lg) + mdot(k * k_decay, v, 0, 0)
    return o, s_new


def dn_chunk(q, k, v, s, gb, h):
    c = q.shape[0]
    g = _lane_pick(gb, h[0])
    beta = _lane_pick(gb, h[1])
    i = lax.broadcasted_iota(jnp.int32, (c, c), 0)
    j = lax.broadcasted_iota(jnp.int32, (c, c), 1)
    tri = i >= j
    gc = _lane_pick(mdot(tri.astype(F32), gb, 1, 0, True), h[0])
    gc_row = jnp.sum(jnp.where(i == j, gc, 0.0), axis=0, keepdims=True)
    decay = jnp.where(tri, jnp.exp(jnp.where(tri, gc - gc_row, 0.0)), 0.0)
    kb = k * beta
    vb = v * beta
    a = jnp.where(i > j, mdot(kb, k, 1, 1) * decay, 0.0)
    t = tri_inv(a)
    e = jnp.exp(gc)
    g_last = jnp.sum(g, axis=0, keepdims=True)
    w_val = mdot(t, vb, 1, 0)
    k_cum = mdot(t, kb * e, 1, 0)
    qk = mdot(q, k, 1, 1) * decay
    v_new = w_val - mdot(k_cum, s, 1, 0)
    o = mdot(q * e, s, 1, 0) + mdot(qk, v_new, 1, 0)
    s_new = s * jnp.exp(g_last) + mdot(k * jnp.exp(g_last - gc), v_new, 0, 0)
    return o, s_new


def scan_fwd(chunk_fn, c, q, k, v, slab, hsel, name):
    t, wid = q.shape
    n, nheads = t // c, wid // HEAD_DIM
    per_token = slab.shape[0] != 1
    row = pl.BlockSpec((c, wid), lambda i: (i, 0))
    slab_spec = pl.BlockSpec((c, LANES), lambda i: (i, 0)) if per_token else _full_spec(slab)
    st_spec = pl.BlockSpec((1, nheads, HEAD_DIM, HEAD_DIM), lambda i: (i, 0, 0, 0))

    def body(q_ref, k_ref, v_ref, slab_ref, o_ref, st_ref, s_scr):
        @pl.when(pl.program_id(0) == 0)
        def _():
            s_scr[...] = jnp.zeros_like(s_scr)

        sb = slab_ref[...]
        for h in range(nheads):
            sl = slice(h * HEAD_DIM, (h + 1) * HEAD_DIM)
            s_h = s_scr[h]
            st_ref[0, h] = s_h
            o, s_new = chunk_fn(q_ref[:, sl], k_ref[:, sl], v_ref[:, sl], s_h, sb, hsel(h))
            o_ref[:, sl] = o
            s_scr[h] = s_new

    return pl.pallas_call(
        body, name=name, grid=(n,),
        in_specs=[row, row, row, slab_spec], out_specs=[row, st_spec],
        out_shape=[jax.ShapeDtypeStruct((t, wid), F32), jax.ShapeDtypeStruct((n, nheads, HEAD_DIM, HEAD_DIM), F32)],
        scratch_shapes=[pltpu.VMEM((nheads, HEAD_DIM, HEAD_DIM), F32)],
        compiler_params=pltpu.CompilerParams(dimension_semantics=("arbitrary",)),
    )(q, k, v, slab)


def scan_bwd(chunk_fn, c, q, k, v, slab, states, do, hsel, name):
    t, wid = q.shape
    n, nheads = t // c, wid // HEAD_DIM
    per_token = slab.shape[0] != 1
    row = pl.BlockSpec((c, wid), lambda i: (n - 1 - i, 0))
    slab_spec = pl.BlockSpec((c, LANES), lambda i: (n - 1 - i, 0)) if per_token else _full_spec(slab)
    st_spec = pl.BlockSpec((1, nheads, HEAD_DIM, HEAD_DIM), lambda i: (n - 1 - i, 0, 0, 0))

    def body(q_ref, k_ref, v_ref, slab_ref, st_ref, do_ref, dq_ref, dk_ref, dv_ref, dslab_ref, ds_scr):
        @pl.when(pl.program_id(0) == 0)
        def _():
            ds_scr[...] = jnp.zeros_like(ds_scr)
            if not per_token:
                dslab_ref[...] = jnp.zeros_like(dslab_ref)

        sb = slab_ref[...]
        dslab = None
        for h in range(nheads):
            sl = slice(h * HEAD_DIM, (h + 1) * HEAD_DIM)
            _, vjp = jax.vjp(lambda a, b, cc, s, z, h=h: chunk_fn(a, b, cc, s, z, hsel(h)),
                             q_ref[:, sl], k_ref[:, sl], v_ref[:, sl], st_ref[0, h], sb)
            dq, dk, dv, ds, dz = vjp((do_ref[:, sl], ds_scr[h]))
            dq_ref[:, sl] = dq
            dk_ref[:, sl] = dk
            dv_ref[:, sl] = dv
            ds_scr[h] = ds
            dslab = dz if dslab is None else dslab + dz
        if per_token:
            dslab_ref[...] = dslab
        else:
            dslab_ref[...] += dslab

    return pl.pallas_call(
        body, name=name, grid=(n,),
        in_specs=[row, row, row, slab_spec, st_spec, row], out_specs=[row, row, row, slab_spec],
        out_shape=[jax.ShapeDtypeStruct((t, wid), F32)] * 3 + [jax.ShapeDtypeStruct(slab.shape, F32)],
        scratch_shapes=[pltpu.VMEM((nheads, HEAD_DIM, HEAD_DIM), F32)],
        compiler_params=pltpu.CompilerParams(dimension_semantics=("arbitrary",)),
    )(q, k, v, slab, states, do)


def ret_chunk(q, k, v, s, logit_slab, sel):
    lane, rev = sel
    c = q.shape[0]
    lg = -_softplus(-_lane_pick(logit_slab, lane))
    i = lax.broadcasted_iota(jnp.int32, (c, c), 0)
    j = lax.broadcasted_iota(jnp.int32, (c, c), 1)
    rel = ((j - i) if rev else (i - j)).astype(F32)
    decay = jnp.where(rel >= 0, jnp.exp(jnp.maximum(rel, 0.0) * lg), 0.0)
    pos = lax.broadcasted_iota(jnp.int32, (c, 1), 0).astype(F32)
    pos = (c - 1.0 - pos) if rev else pos
    q_decay = jnp.exp((pos + 1.0) * lg)
    k_decay = jnp.exp((c - 1.0 - pos) * lg)
    intra = mdot(q, k, 1, 1) * decay
    o = mdot(intra, v, 1, 0) + mdot(q * q_decay, s, 1, 0)
    s_new = s * jnp.exp(c * lg) + mdot(k * k_decay, v, 0, 0)
    return o, s_new


def dn_chunk(q, k, v, s, gb, sel):
    g_lane, b_lane, rev = sel
    c = q.shape[0]
    g = _lane_pick(gb, g_lane)
    beta = _lane_pick(gb, b_lane)
    i = lax.broadcasted_iota(jnp.int32, (c, c), 0)
    j = lax.broadcasted_iota(jnp.int32, (c, c), 1)
    tri = (i <= j) if rev else (i >= j)
    strict = (i < j) if rev else (i > j)
    gc = _lane_pick(mdot(tri.astype(F32), gb, 1, 0, True), g_lane)
    gc_row = jnp.sum(jnp.where(i == j, gc, 0.0), axis=0, keepdims=True)
    decay = jnp.where(tri, jnp.exp(jnp.where(tri, gc - gc_row, 0.0)), 0.0)
    kb = k * beta
    vb = v * beta
    a = jnp.where(strict, mdot(kb, k, 1, 1) * decay, 0.0)
    t = tri_inv(a)
    e = jnp.exp(gc)
    g_last = jnp.sum(g, axis=0, keepdims=True)
    w_val = mdot(t, vb, 1, 0)
    k_cum = mdot(t, kb * e, 1, 0)
    qk = mdot(q, k, 1, 1) * decay
    v_new = w_val - mdot(k_cum, s, 1, 0)
    o = mdot(q * e, s, 1, 0) + mdot(qk, v_new, 1, 0)
    s_new = s * jnp.exp(g_last) + mdot(k * jnp.exp(g_last - gc), v_new, 0, 0)
    return o, s_new


def _scan_maps(n, ncc):
    return (lambda s: s), (lambda s: jnp.where(s < ncc, ncc - 1 - s, n - 1 - (s - ncc)))


def scan_fwd(chunk_fn, c, q, k, v, slab, sels, lc, name):
    t, wid = q.shape
    n, nheads, ncc = t // c, wid // HEAD_DIM, lc // c
    per_token = slab.shape[0] != 1
    maps = _scan_maps(n, ncc)
    rows = [pl.BlockSpec((c, wid), lambda s, m=m: (m(s), 0)) for m in maps]
    slabs = [pl.BlockSpec((c, LANES), lambda s, m=m: (m(s), 0)) if per_token else _full_spec(slab) for m in maps]
    st_spec = pl.BlockSpec((1, 2 * nheads, HEAD_DIM, HEAD_DIM), lambda s: (s, 0, 0, 0))

    def body(qf, kf, vf, sbf, qb, kb, vb, sbb, of_ref, ob_ref, st_ref, s_scr):
        @pl.when(pl.program_id(0) == 0)
        def _():
            s_scr[...] = jnp.zeros_like(s_scr)

        work = []
        for d, (qr, kr, vr, sr) in enumerate(((qf, kf, vf, sbf), (qb, kb, vb, sbb))):
            sb = sr[...]
            for h in range(nheads):
                sl = slice(h * HEAD_DIM, (h + 1) * HEAD_DIM)
                work.append((d, h, sl, qr[:, sl], kr[:, sl], vr[:, sl], s_scr[d * nheads + h], sb))
        res = [chunk_fn(qq, kk, vv, ss, sb, sels[d][h]) for d, h, sl, qq, kk, vv, ss, sb in work]
        for (d, h, sl, _, _, _, ss, _), (o, s_new) in zip(work, res):
            st_ref[0, d * nheads + h] = ss
            (of_ref, ob_ref)[d][:, sl] = o
            s_scr[d * nheads + h] = s_new

    return pl.pallas_call(
        body, name=name, grid=(n,),
        in_specs=[rows[0]] * 3 + [slabs[0]] + [rows[1]] * 3 + [slabs[1]], out_specs=[rows[0], rows[1], st_spec],
        out_shape=[jax.ShapeDtypeStruct((t, wid), F32)] * 2 + [jax.ShapeDtypeStruct((n, 2 * nheads, HEAD_DIM, HEAD_DIM), F32)],
        scratch_shapes=[pltpu.VMEM((2 * nheads, HEAD_DIM, HEAD_DIM), F32)],
        compiler_params=pltpu.CompilerParams(dimension_semantics=("arbitrary",)),
    )(q, k, v, slab, q, k, v, slab)


def scan_bwd(chunk_fn, c, q, k, v, slab, states, do, sels, lc, name):
    t, wid = q.shape
    n, nheads, ncc = t // c, wid // HEAD_DIM, lc // c
    per_token = slab.shape[0] != 1
    maps = [lambda s, m=m: m(n - 1 - s) for m in _scan_maps(n, ncc)]
    rows = [pl.BlockSpec((c, wid), lambda s, m=m: (m(s), 0)) for m in maps]
    slabs = [pl.BlockSpec((c, LANES), lambda s, m=m: (m(s), 0)) if per_token else _full_spec(slab) for m in maps]
    st_spec = pl.BlockSpec((1, 2 * nheads, HEAD_DIM, HEAD_DIM), lambda s: (n - 1 - s, 0, 0, 0))
    n_slab_out = 2 if per_token else 1

    def body(*refs):
        ins = (refs[0:4] + (refs[9],), refs[4:8] + (refs[10],))
        st_ref = refs[8]
        outs = refs[11:11 + 6 + n_slab_out]
        ds_scr = refs[-1]

        @pl.when(pl.program_id(0) == 0)
        def _():
            ds_scr[...] = jnp.zeros_like(ds_scr)
            if not per_token:
                outs[6][...] = jnp.zeros_like(outs[6])

        work = []
        for d, (qr, kr, vr, sr, dor) in enumerate(ins):
            sb = sr[...]
            for h in range(nheads):
                sl = slice(h * HEAD_DIM, (h + 1) * HEAD_DIM)
                idx = d * nheads + h
                work.append((d, h, sl, idx, (qr[:, sl], kr[:, sl], vr[:, sl], st_ref[0, idx], sb), (dor[:, sl], ds_scr[idx])))
        res = []
        for d, h, sl, idx, prim, cot in work:
            _, vjp = jax.vjp(lambda a, b, cc, s, z, d=d, h=h: chunk_fn(a, b, cc, s, z, sels[d][h]), *prim)
            res.append(vjp(cot))
        dslab = [None, None]
        for (d, h, sl, idx, _, _), (dq, dk, dv, ds, dz) in zip(work, res):
            outs[3 * d][:, sl] = dq
            outs[3 * d + 1][:, sl] = dk
            outs[3 * d + 2][:, sl] = dv
            ds_scr[idx] = ds
            dslab[d] = dz if dslab[d] is None else dslab[d] + dz
        if per_token:
            outs[6][...] = dslab[0]
            outs[7][...] = dslab[1]
        else:
            outs[6][...] += dslab[0] + dslab[1]

    return pl.pallas_call(
        body, name=name, grid=(n,),
        in_specs=[rows[0]] * 3 + [slabs[0]] + [rows[1]] * 3 + [slabs[1]] + [st_spec, rows[0], rows[1]],
        out_specs=[rows[0]] * 3 + [rows[1]] * 3 + ([slabs[0], slabs[1]] if per_token else [slabs[0]]),
        out_shape=[jax.ShapeDtypeStruct((t, wid), F32)] * 6 + [jax.ShapeDtypeStruct(slab.shape, F32)] * n_slab_out,
        scratch_shapes=[pltpu.VMEM((2 * nheads, HEAD_DIM, HEAD_DIM), F32)],
        compiler_params=pltpu.CompilerParams(dimension_semantics=("arbitrary",)),
    )(q, k, v, slab, q, k, v, slab, states, do, do)


@jax.custom_vjp
def tri_inv_all(mats):
    n = mats[0].shape[0]
    r = lax.broadcasted_iota(jnp.int32, (n, n), 0)
    c = lax.broadcasted_iota(jnp.int32, (n, n), 1)
    eye = (r == c).astype(F32)
    ps = [-a for a in mats]
    ts = [eye + p for p in ps]
    k = 2
    while k < n:
        ps = [_contract(p, p, 1, 0, True) for p in ps]
        ts = [t + _contract(t, p, 1, 0, True) for t, p in zip(ts, ps)]
        k *= 2
    return tuple(ts)


def _tri_inv_all_fwd(mats):
    ts = tri_inv_all(mats)
    return ts, ts


def _tri_inv_all_bwd(ts, gs):
    left = [_contract(t, g, 0, 0, True) for t, g in zip(ts, gs)]
    return (tuple(-_contract(l, t, 1, 1, True) for l, t in zip(left, ts)),)


tri_inv_all.defvjp(_tri_inv_all_fwd, _tri_inv_all_bwd)


def ret_chunk(qs, ks, vs, ss, slabs, sels):
    c = qs[0].shape[0]
    i = lax.broadcasted_iota(jnp.int32, (c, c), 0)
    j = lax.broadcasted_iota(jnp.int32, (c, c), 1)
    pos0 = lax.broadcasted_iota(jnp.int32, (c, 1), 0).astype(F32)
    lgs = [-_softplus(-_lane_pick(slabs[d], lane)) for d, lane, _ in sels]
    rels = [((j - i) if rev else (i - j)).astype(F32) for _, _, rev in sels]
    decays = [jnp.where(rel >= 0, jnp.exp(jnp.maximum(rel, 0.0) * lg), 0.0) for rel, lg in zip(rels, lgs)]
    poss = [(c - 1.0 - pos0) if rev else pos0 for _, _, rev in sels]
    intra = [mdot(q, k, 1, 1) * dec for q, k, dec in zip(qs, ks, decays)]
    kv = [mdot(k * jnp.exp((c - 1.0 - pos) * lg), v, 0, 0) for k, v, pos, lg in zip(ks, vs, poss, lgs)]
    o1 = [mdot(a, v, 1, 0) for a, v in zip(intra, vs)]
    o2 = [mdot(q * jnp.exp((pos + 1.0) * lg), s, 1, 0) for q, s, pos, lg in zip(qs, ss, poss, lgs)]
    outs = [a + b for a, b in zip(o1, o2)]
    s_new = [s * jnp.exp(c * lg) + u for s, lg, u in zip(ss, lgs, kv)]
    return outs, s_new


def dn_chunk(qs, ks, vs, ss, slabs, sels):
    c = qs[0].shape[0]
    i = lax.broadcasted_iota(jnp.int32, (c, c), 0)
    j = lax.broadcasted_iota(jnp.int32, (c, c), 1)
    tris = [(i <= j) if rev else (i >= j) for _, _, _, rev in sels]
    stricts = [(i < j) if rev else (i > j) for _, _, _, rev in sels]
    gs = [_lane_pick(slabs[d], gl) for d, gl, _, _ in sels]
    betas = [_lane_pick(slabs[d], bl) for d, _, bl, _ in sels]
    sums = [mdot(tri.astype(F32), slabs[sel[0]], 1, 0, True) for tri, sel in zip(tris, sels)]
    gcs = [_lane_pick(cs, sel[1]) for cs, sel in zip(sums, sels)]
    gc_rows = [jnp.sum(jnp.where(i == j, gc, 0.0), axis=0, keepdims=True) for gc in gcs]
    decays = [jnp.where(tri, jnp.exp(jnp.where(tri, gc - gr, 0.0)), 0.0) for tri, gc, gr in zip(tris, gcs, gc_rows)]
    kbs = [k * b for k, b in zip(ks, betas)]
    vbs = [v * b for v, b in zip(vs, betas)]
    kk = [mdot(kb, k, 1, 1) for kb, k in zip(kbs, ks)]
    qk = [mdot(q, k, 1, 1) * dec for q, k, dec in zip(qs, ks, decays)]
    ts = tri_inv_all(tuple(jnp.where(st, a * dec, 0.0) for st, a, dec in zip(stricts, kk, decays)))
    es = [jnp.exp(gc) for gc in gcs]
    g_last = [jnp.sum(g, axis=0, keepdims=True) for g in gs]
    w_val = [mdot(t, vb, 1, 0) for t, vb in zip(ts, vbs)]
    k_cum = [mdot(t, kb * e, 1, 0) for t, kb, e in zip(ts, kbs, es)]
    ks_s = [mdot(kc, s, 1, 0) for kc, s in zip(k_cum, ss)]
    qs_s = [mdot(q * e, s, 1, 0) for q, e, s in zip(qs, es, ss)]
    v_new = [w - u for w, u in zip(w_val, ks_s)]
    o2 = [mdot(a, vn, 1, 0) for a, vn in zip(qk, v_new)]
    upd = [mdot(k * jnp.exp(gl - gc), vn, 0, 0) for k, gl, gc, vn in zip(ks, g_last, gcs, v_new)]
    outs = [a + b for a, b in zip(qs_s, o2)]
    s_new = [s * jnp.exp(gl) + u for s, gl, u in zip(ss, g_last, upd)]
    return outs, s_new


def scan_fwd(chunk_fn, c, q, k, v, slab, sels, lc, name):
    t, wid = q.shape
    n, nheads, ncc = t // c, wid // HEAD_DIM, lc // c
    per_token = slab.shape[0] != 1
    maps = _scan_maps(n, ncc)
    rows = [pl.BlockSpec((c, wid), lambda s, m=m: (m(s), 0)) for m in maps]
    slabs = [pl.BlockSpec((c, LANES), lambda s, m=m: (m(s), 0)) if per_token else _full_spec(slab) for m in maps]
    st_spec = pl.BlockSpec((1, 2 * nheads, HEAD_DIM, HEAD_DIM), lambda s: (s, 0, 0, 0))
    items = [(d, h) for d in range(2) for h in range(nheads)]
    flat_sels = [(d,) + tuple(sels[d][h]) for d, h in items]

    def body(qf, kf, vf, sbf, qb, kb, vb, sbb, of_ref, ob_ref, st_ref, s_scr):
        @pl.when(pl.program_id(0) == 0)
        def _():
            s_scr[...] = jnp.zeros_like(s_scr)

        qr, kr, vr = (qf, qb), (kf, kb), (vf, vb)
        sl = lambda h: slice(h * HEAD_DIM, (h + 1) * HEAD_DIM)
        ss = [s_scr[n_] for n_ in range(len(items))]
        outs, s_new = chunk_fn([qr[d][:, sl(h)] for d, h in items], [kr[d][:, sl(h)] for d, h in items],
                               [vr[d][:, sl(h)] for d, h in items], ss, [sbf[...], sbb[...]], flat_sels)
        for n_, (d, h) in enumerate(items):
            st_ref[0, n_] = ss[n_]
            (of_ref, ob_ref)[d][:, sl(h)] = outs[n_]
            s_scr[n_] = s_new[n_]

    return pl.pallas_call(
        body, name=name, grid=(n,),
        in_specs=[rows[0]] * 3 + [slabs[0]] + [rows[1]] * 3 + [slabs[1]], out_specs=[rows[0], rows[1], st_spec],
        out_shape=[jax.ShapeDtypeStruct((t, wid), F32)] * 2 + [jax.ShapeDtypeStruct((n, 2 * nheads, HEAD_DIM, HEAD_DIM), F32)],
        scratch_shapes=[pltpu.VMEM((2 * nheads, HEAD_DIM, HEAD_DIM), F32)],
        compiler_params=pltpu.CompilerParams(dimension_semantics=("arbitrary",)),
    )(q, k, v, slab, q, k, v, slab)


def scan_bwd(chunk_fn, c, q, k, v, slab, states, do, sels, lc, name):
    t, wid = q.shape
    n, nheads, ncc = t // c, wid // HEAD_DIM, lc // c
    per_token = slab.shape[0] != 1
    maps = [lambda s, m=m: m(n - 1 - s) for m in _scan_maps(n, ncc)]
    rows = [pl.BlockSpec((c, wid), lambda s, m=m: (m(s), 0)) for m in maps]
    slabs = [pl.BlockSpec((c, LANES), lambda s, m=m: (m(s), 0)) if per_token else _full_spec(slab) for m in maps]
    st_spec = pl.BlockSpec((1, 2 * nheads, HEAD_DIM, HEAD_DIM), lambda s: (n - 1 - s, 0, 0, 0))
    n_slab_out = 2 if per_token else 1
    items = [(d, h) for d in range(2) for h in range(nheads)]
    flat_sels = [(d,) + tuple(sels[d][h]) for d, h in items]

    def body(*refs):
        qr, kr, vr, sr = (refs[0], refs[4]), (refs[1], refs[5]), (refs[2], refs[6]), (refs[3], refs[7])
        st_ref, dor = refs[8], (refs[9], refs[10])
        outs = refs[11:11 + 6 + n_slab_out]
        ds_scr = refs[-1]

        @pl.when(pl.program_id(0) == 0)
        def _():
            ds_scr[...] = jnp.zeros_like(ds_scr)
            if not per_token:
                outs[6][...] = jnp.zeros_like(outs[6])

        sl = lambda h: slice(h * HEAD_DIM, (h + 1) * HEAD_DIM)
        prim = ([qr[d][:, sl(h)] for d, h in items], [kr[d][:, sl(h)] for d, h in items],
                [vr[d][:, sl(h)] for d, h in items], [st_ref[0, n_] for n_ in range(len(items))], [sr[0][...], sr[1][...]])
        cot = ([dor[d][:, sl(h)] for d, h in items], [ds_scr[n_] for n_ in range(len(items))])
        _, vjp = jax.vjp(lambda *a: chunk_fn(*a, flat_sels), *prim)
        dqs, dks, dvs, dss, dslabs = vjp(cot)
        for n_, (d, h) in enumerate(items):
            outs[3 * d][:, sl(h)] = dqs[n_]
            outs[3 * d + 1][:, sl(h)] = dks[n_]
            outs[3 * d + 2][:, sl(h)] = dvs[n_]
            ds_scr[n_] = dss[n_]
        if per_token:
            outs[6][...] = dslabs[0]
            outs[7][...] = dslabs[1]
        else:
            outs[6][...] += dslabs[0] + dslabs[1]

    return pl.pallas_call(
        body, name=name, grid=(n,),
        in_specs=[rows[0]] * 3 + [slabs[0]] + [rows[1]] * 3 + [slabs[1]] + [st_spec, rows[0], rows[1]],
        out_specs=[rows[0]] * 3 + [rows[1]] * 3 + ([slabs[0], slabs[1]] if per_token else [slabs[0]]),
        out_shape=[jax.ShapeDtypeStruct((t, wid), F32)] * 6 + [jax.ShapeDtypeStruct(slab.shape, F32)] * n_slab_out,
        scratch_shapes=[pltpu.VMEM((2 * nheads, HEAD_DIM, HEAD_DIM), F32)],
        compiler_params=pltpu.CompilerParams(dimension_semantics=("arbitrary",)),
    )(q, k, v, slab, q, k, v, slab, states, do, do)


def _att_tiles(t, lc):
    return _pick(lc, (256, 128)), _pick(t, (768, 384, 256, 128))


def _att_probs(q, k, lse, is_ctx, j, tk, lc):
    s = _contract(q, k, 1, 1, False) * QK_SCALE
    kidx = j * tk + lax.broadcasted_iota(jnp.int32, (1, tk), 1)
    masked = jnp.logical_and(is_ctx, kidx >= lc)
    return s, masked, (None if lse is None else jnp.where(masked, 0.0, jnp.exp(s - lse)))


def attn_fwd(qn, kn, v, lc, name):
    t = qn.shape[0]
    nh, nkv = qn.shape[1] // HEAD_DIM, kn.shape[1] // HEAD_DIM
    grp = nh // nkv
    tq, tk = _att_tiles(t, lc)
    nq, nk, nctq = t // tq, t // tk, lc // tq

    def body(q_ref, k_ref, v_ref, o_ref, lse_ref, m_scr, l_scr, acc):
        i, j = pl.program_id(1), pl.program_id(2)

        @pl.when(j == 0)
        def _():
            m_scr[...] = jnp.full_like(m_scr, NEG_BIG)
            l_scr[...] = jnp.zeros_like(l_scr)
            acc[...] = jnp.zeros_like(acc)

        is_ctx = i < nctq

        @pl.when(jnp.logical_or(jnp.logical_not(is_ctx), j * tk < lc))
        def _():
            s, masked, _ = _att_probs(q_ref[...], k_ref[...], None, is_ctx, j, tk, lc)
            s = jnp.where(masked, NEG_BIG, s)
            m_old = m_scr[...]
            m_new = jnp.maximum(m_old, jnp.max(s, axis=1, keepdims=True))
            alpha = jnp.exp(m_old - m_new)
            p = jnp.where(masked, 0.0, jnp.exp(s - m_new))
            l_scr[...] = alpha * l_scr[...] + jnp.sum(p, axis=1, keepdims=True)
            acc[...] = alpha * acc[...] + _contract(p, v_ref[...], 1, 0, False)
            m_scr[...] = m_new

        @pl.when(j == nk - 1)
        def _():
            o_ref[...] = acc[...] / l_scr[...]
            lse_ref[0] = m_scr[...] + jnp.log(l_scr[...])

    return pl.pallas_call(
        body, name=name, grid=(nh, nq, nk),
        in_specs=[pl.BlockSpec((tq, HEAD_DIM), lambda h, i, j: (i, h)),
                  pl.BlockSpec((tk, HEAD_DIM), lambda h, i, j: (j, h // grp)),
                  pl.BlockSpec((tk, HEAD_DIM), lambda h, i, j: (j, h // grp))],
        out_specs=[pl.BlockSpec((tq, HEAD_DIM), lambda h, i, j: (i, h)),
                   pl.BlockSpec((1, tq, 1), lambda h, i, j: (h, i, 0))],
        out_shape=[jax.ShapeDtypeStruct((t, nh * HEAD_DIM), F32), jax.ShapeDtypeStruct((nh, t, 1), F32)],
        scratch_shapes=[pltpu.VMEM((tq, 1), F32), pltpu.VMEM((tq, 1), F32), pltpu.VMEM((tq, HEAD_DIM), F32)],
        compiler_params=pltpu.CompilerParams(dimension_semantics=("parallel", "parallel", "arbitrary")),
    )(qn, kn, v)


def attn_bwd_dq(qn, kn, v, o, do, lse, lc, name):
    t = qn.shape[0]
    nh, nkv = qn.shape[1] // HEAD_DIM, kn.shape[1] // HEAD_DIM
    grp = nh // nkv
    tq, tk = _att_tiles(t, lc)
    nq, nk, nctq = t // tq, t // tk, lc // tq

    def body(q_ref, k_ref, v_ref, o_ref, do_ref, lse_ref, dq_ref, dl_ref, acc, dl_scr):
        i, j = pl.program_id(1), pl.program_id(2)

        @pl.when(j == 0)
        def _():
            acc[...] = jnp.zeros_like(acc)
            dl_scr[...] = jnp.sum(do_ref[...] * o_ref[...], axis=1, keepdims=True)

        is_ctx = i < nctq

        @pl.when(jnp.logical_or(jnp.logical_not(is_ctx), j * tk < lc))
        def _():
            _, _, p = _att_probs(q_ref[...], k_ref[...], lse_ref[0], is_ctx, j, tk, lc)
            dp = _contract(do_ref[...], v_ref[...], 1, 1, False)
            ds = p * (dp - dl_scr[...]) * QK_SCALE
            acc[...] += _contract(ds, k_ref[...], 1, 0, False)

        @pl.when(j == nk - 1)
        def _():
            dq_ref[...] = acc[...]
            dl_ref[0] = dl_scr[...]

    qspec = pl.BlockSpec((tq, HEAD_DIM), lambda h, i, j: (i, h))
    kspec = pl.BlockSpec((tk, HEAD_DIM), lambda h, i, j: (j, h // grp))
    rspec = pl.BlockSpec((1, tq, 1), lambda h, i, j: (h, i, 0))
    return pl.pallas_call(
        body, name=name, grid=(nh, nq, nk),
        in_specs=[qspec, kspec, kspec, qspec, qspec, rspec],
        out_specs=[qspec, rspec],
        out_shape=[jax.ShapeDtypeStruct((t, nh * HEAD_DIM), F32), jax.ShapeDtypeStruct((nh, t, 1), F32)],
        scratch_shapes=[pltpu.VMEM((tq, HEAD_DIM), F32), pltpu.VMEM((tq, 1), F32)],
        compiler_params=pltpu.CompilerParams(dimension_semantics=("parallel", "parallel", "arbitrary")),
    )(qn, kn, v, o, do, lse)


def attn_bwd_dkv(qn, kn, v, do, lse, delta, lc, name):
    t = qn.shape[0]
    nh, nkv = qn.shape[1] // HEAD_DIM, kn.shape[1] // HEAD_DIM
    grp = nh // nkv
    tq, tk = _att_tiles(t, lc)
    nq, nk, nctq = t // tq, t // tk, lc // tq
    nr = grp * nq

    def body(q_ref, k_ref, v_ref, do_ref, lse_ref, dl_ref, dk_ref, dv_ref, dk_acc, dv_acc):
        j, r = pl.program_id(1), pl.program_id(2)

        @pl.when(r == 0)
        def _():
            dk_acc[...] = jnp.zeros_like(dk_acc)
            dv_acc[...] = jnp.zeros_like(dv_acc)

        is_ctx = (r % nq) < nctq

        @pl.when(jnp.logical_or(jnp.logical_not(is_ctx), j * tk < lc))
        def _():
            _, _, p = _att_probs(q_ref[...], k_ref[...], lse_ref[0], is_ctx, j, tk, lc)
            dv_acc[...] += _contract(p, do_ref[...], 0, 0, False)
            dp = _contract(do_ref[...], v_ref[...], 1, 1, False)
            ds = p * (dp - dl_ref[0]) * QK_SCALE
            dk_acc[...] += _contract(ds, q_ref[...], 0, 0, False)

        @pl.when(r == nr - 1)
        def _():
            dk_ref[...] = dk_acc[...]
            dv_ref[...] = dv_acc[...]

    qspec = pl.BlockSpec((tq, HEAD_DIM), lambda g, j, r: (r % nq, g * grp + r // nq))
    kspec = pl.BlockSpec((tk, HEAD_DIM), lambda g, j, r: (j, g))
    rspec = pl.BlockSpec((1, tq, 1), lambda g, j, r: (g * grp + r // nq, r % nq, 0))
    return pl.pallas_call(
        body, name=name, grid=(nkv, nk, nr),
        in_specs=[qspec, kspec, kspec, qspec, rspec, rspec],
        out_specs=[kspec, kspec],
        out_shape=[jax.ShapeDtypeStruct((t, nkv * HEAD_DIM), F32), jax.ShapeDtypeStruct((t, nkv * HEAD_DIM), F32)],
        scratch_shapes=[pltpu.VMEM((tk, HEAD_DIM), F32), pltpu.VMEM((tk, HEAD_DIM), F32)],
        compiler_params=pltpu.CompilerParams(dimension_semantics=("parallel", "parallel", "arbitrary")),
    )(qn, kn, v, do, lse, delta)


ATT_FWD_TK = (2816, 1408, 768, 384, 256, 128)
LOG2E = 1.4426950408889634
LN2 = 0.6931471805599453


def _att_grid(qn, kn, lc):
    t = qn.shape[0]
    nkv = kn.shape[1] // HEAD_DIM
    grp = qn.shape[1] // HEAD_DIM // nkv
    tq, tk = _att_tiles(t, lc)
    return t, nkv, grp, tq, tk, t // tq, t // tk, lc // tq


def _att_paths(i, j, nctq, tk, lc, step):
    is_ctx = i < nctq

    @pl.when(jnp.logical_and(is_ctx, j * tk < lc))
    def _():
        kidx = j * tk + lax.broadcasted_iota(jnp.int32, (1, tk), 1)
        step(kidx >= lc)

    @pl.when(jnp.logical_not(is_ctx))
    def _():
        step(None)


def _att_scores2(q, k, hidden):
    s = _contract(q, k, 1, 1, False)
    return s if hidden is None else jnp.where(hidden, NEG_BIG, s)


def attn_fwd(qn, kn, v, lc, name):
    t, nkv, grp, tq, tk, nq, nk, nctq = _att_grid(qn, kn, lc)
    tk = _pick(t, ATT_FWD_TK)
    nk = t // tk

    def body(q_ref, k_ref, v_ref, o_ref, lse_ref, m_scr, acc):
        i, j = pl.program_id(1), pl.program_id(2)

        @pl.when(j == 0)
        def _():
            m_scr[...] = jnp.full_like(m_scr, NEG_BIG)
            acc[...] = jnp.zeros_like(acc)

        def step(hidden):
            k = k_ref[...]
            vext = jnp.concatenate([v_ref[...], jnp.ones((tk, HEAD_DIM), v_ref.dtype)], 1)
            qs = [q_ref[:, h * HEAD_DIM:(h + 1) * HEAD_DIM] for h in range(grp)]
            ms = [m_scr[h] for h in range(grp)]
            accs = [acc[h] for h in range(grp)]
            ss = [_att_scores2(q, k, hidden) for q in qs]
            m_new = [jnp.maximum(m, jnp.max(s, axis=1, keepdims=True)) for m, s in zip(ms, ss)]
            alpha = [jnp.exp2(m - mn) for m, mn in zip(ms, m_new)]
            ps = [jnp.exp2(s - mn) for s, mn in zip(ss, m_new)]
            pv = [_contract(p, vext, 1, 0, False) for p in ps]
            for h in range(grp):
                m_scr[h] = m_new[h]
                acc[h] = alpha[h] * accs[h] + pv[h]

        _att_paths(i, j, nctq, tk, lc, step)

        @pl.when(j == nk - 1)
        def _():
            for h in range(grp):
                a = acc[h]
                o_ref[:, h * HEAD_DIM:(h + 1) * HEAD_DIM] = a[:, :HEAD_DIM] / a[:, HEAD_DIM:]
                lse_ref[h] = m_scr[h] * LN2 + jnp.log(a[:, HEAD_DIM:HEAD_DIM + 1])

    wid = grp * HEAD_DIM
    return pl.pallas_call(
        body, name=name, grid=(nkv, nq, nk),
        in_specs=[pl.BlockSpec((tq, wid), lambda g, i, j: (i, g)),
                  pl.BlockSpec((tk, HEAD_DIM), lambda g, i, j: (j, g)),
                  pl.BlockSpec((tk, HEAD_DIM), lambda g, i, j: (j, g))],
        out_specs=[pl.BlockSpec((tq, wid), lambda g, i, j: (i, g)),
                   pl.BlockSpec((grp, tq, 1), lambda g, i, j: (g, i, 0))],
        out_shape=[jax.ShapeDtypeStruct((t, nkv * wid), F32), jax.ShapeDtypeStruct((nkv * grp, t, 1), F32)],
        scratch_shapes=[pltpu.VMEM((grp, tq, 1), F32), pltpu.VMEM((grp, tq, 2 * HEAD_DIM), F32)],
        compiler_params=pltpu.CompilerParams(dimension_semantics=("parallel", "parallel", "arbitrary")),
    )(qn, kn, v)


def attn_bwd_dq(qn, kn, v, o, do, lse, lc, name):
    t, nkv, grp, tq, tk, nq, nk, nctq = _att_grid(qn, kn, lc)

    def body(q_ref, k_ref, v_ref, o_ref, do_ref, lse_ref, dq_ref, dl_ref, acc, dl_scr):
        i, j = pl.program_id(1), pl.program_id(2)

        @pl.when(j == 0)
        def _():
            acc[...] = jnp.zeros_like(acc)
            for h in range(grp):
                sl = slice(h * HEAD_DIM, (h + 1) * HEAD_DIM)
                dl_scr[h] = jnp.sum(do_ref[:, sl] * o_ref[:, sl], axis=1, keepdims=True)

        def step(hidden):
            k, vv = k_ref[...], v_ref[...]
            sls = [slice(h * HEAD_DIM, (h + 1) * HEAD_DIM) for h in range(grp)]
            ss = [_att_scores2(q_ref[:, sl], k, hidden) for sl in sls]
            dps = [_contract(do_ref[:, sl], vv, 1, 1, False) for sl in sls]
            ps = [jnp.exp2(s - lse_ref[h] * LOG2E) for h, s in enumerate(ss)]
            dss = [p * (dp - dl_scr[h]) for h, (p, dp) in enumerate(zip(ps, dps))]
            upd = [_contract(ds, k, 1, 0, False) for ds in dss]
            for h in range(grp):
                acc[h] += upd[h]

        _att_paths(i, j, nctq, tk, lc, step)

        @pl.when(j == nk - 1)
        def _():
            for h in range(grp):
                dq_ref[:, h * HEAD_DIM:(h + 1) * HEAD_DIM] = acc[h] * QK_SCALE
                dl_ref[h] = dl_scr[h]

    wid = grp * HEAD_DIM
    qspec = pl.BlockSpec((tq, wid), lambda g, i, j: (i, g))
    kspec = pl.BlockSpec((tk, HEAD_DIM), lambda g, i, j: (j, g))
    rspec = pl.BlockSpec((grp, tq, 1), lambda g, i, j: (g, i, 0))
    return pl.pallas_call(
        body, name=name, grid=(nkv, nq, nk),
        in_specs=[qspec, kspec, kspec, qspec, qspec, rspec],
        out_specs=[qspec, rspec],
        out_shape=[jax.ShapeDtypeStruct((t, nkv * wid), F32), jax.ShapeDtypeStruct((nkv * grp, t, 1), F32)],
        scratch_shapes=[pltpu.VMEM((grp, tq, HEAD_DIM), F32), pltpu.VMEM((grp, tq, 1), F32)],
        compiler_params=pltpu.CompilerParams(dimension_semantics=("parallel", "parallel", "arbitrary")),
    )(qn, kn, v, o, do, lse)


def attn_bwd_dkv(qn, kn, v, do, lse, delta, lc, name):
    t, nkv, grp, tq, tk, nq, nk, nctq = _att_grid(qn, kn, lc)

    def body(q_ref, k_ref, v_ref, do_ref, lse_ref, dl_ref, dk_ref, dv_ref, dk_acc, dv_acc):
        j, i = pl.program_id(1), pl.program_id(2)

        @pl.when(i == 0)
        def _():
            dk_acc[...] = jnp.zeros_like(dk_acc)
            dv_acc[...] = jnp.zeros_like(dv_acc)

        def step(hidden):
            k, vv = k_ref[...], v_ref[...]
            sls = [slice(h * HEAD_DIM, (h + 1) * HEAD_DIM) for h in range(grp)]
            qs = [q_ref[:, sl] for sl in sls]
            dos = [do_ref[:, sl] for sl in sls]
            ss = [_att_scores2(q, k, hidden) for q in qs]
            dps = [_contract(do, vv, 1, 1, False) for do in dos]
            ps = [jnp.exp2(s - lse_ref[h] * LOG2E) for h, s in enumerate(ss)]
            dss = [p * (dp - dl_ref[h]) for h, (p, dp) in enumerate(zip(ps, dps))]
            dv_new = [_contract(p, do, 0, 0, False) for p, do in zip(ps, dos)]
            dk_new = [_contract(ds, q, 0, 0, False) for ds, q in zip(dss, qs)]
            dv_acc[...] += (dv_new[0] + dv_new[1]) + (dv_new[2] + dv_new[3]) if grp == 4 else sum(dv_new)
            dk_acc[...] += (dk_new[0] + dk_new[1]) + (dk_new[2] + dk_new[3]) if grp == 4 else sum(dk_new)

        _att_paths(i, j, nctq, tk, lc, step)

        @pl.when(i == nq - 1)
        def _():
            dk_ref[...] = dk_acc[...] * LN2
            dv_ref[...] = dv_acc[...]

    wid = grp * HEAD_DIM
    qspec = pl.BlockSpec((tq, wid), lambda g, j, i: (i, g))
    kspec = pl.BlockSpec((tk, HEAD_DIM), lambda g, j, i: (j, g))
    rspec = pl.BlockSpec((grp, tq, 1), lambda g, j, i: (g, i, 0))
    return pl.pallas_call(
        body, name=name, grid=(nkv, nk, nq),
        in_specs=[qspec, kspec, kspec, qspec, rspec, rspec],
        out_specs=[kspec, kspec],
        out_shape=[jax.ShapeDtypeStruct((t, nkv * HEAD_DIM), F32), jax.ShapeDtypeStruct((t, nkv * HEAD_DIM), F32)],
        scratch_shapes=[pltpu.VMEM((tk, HEAD_DIM), F32), pltpu.VMEM((tk, HEAD_DIM), F32)],
        compiler_params=pltpu.CompilerParams(dimension_semantics=("parallel", "parallel", "arbitrary")),
    )(qn, kn, v, do, lse, delta)


def loss_and_grad(y, target, tm, name):
    t, d = y.shape

    def body(y_ref, t_ref, l_ref, g_ref):
        @pl.when(pl.program_id(0) == 0)
        def _():
            l_ref[...] = jnp.zeros_like(l_ref)

        e = y_ref[...] - t_ref[...]
        g_ref[...] = e * (1.0 / d)
        l_ref[...] += 0.5 * jnp.sum(jnp.mean(e * e, axis=1, keepdims=True), axis=0, keepdims=True)

    return pl.pallas_call(
        body, name=name, grid=(t // tm,),
        in_specs=[pl.BlockSpec((tm, d), lambda i: (i, 0))] * 2,
        out_specs=[pl.BlockSpec((1, 1), lambda i: (0, 0)), pl.BlockSpec((tm, d), lambda i: (i, 0))],
        out_shape=[jax.ShapeDtypeStruct((1, 1), F32), jax.ShapeDtypeStruct((t, d), F32)],
        compiler_params=pltpu.CompilerParams(dimension_semantics=("arbitrary",)),
    )(y, target)


def _row_tile(rows, width):
    budget = max(8, (2 * 1024 * 1024) // (4 * width))
    for cand in (1024, 512, 256, 128, 64, 32, 16, 8):
        if cand <= budget and rows % cand == 0:
            return cand
    return rows


def ew_sum(arrs, name, out_dtype=F32):
    rows, width = arrs[0].shape
    tr = _row_tile(rows, width)

    def body(*refs):
        acc = refs[0][...].astype(F32)
        for r in refs[1:-1]:
            acc = acc + r[...].astype(F32)
        refs[-1][...] = acc.astype(refs[-1].dtype)

    spec = pl.BlockSpec((tr, width), lambda i: (i, 0))
    return pl.pallas_call(
        body, name=name, grid=(rows // tr,), in_specs=[spec] * len(arrs), out_specs=spec,
        out_shape=jax.ShapeDtypeStruct((rows, width), out_dtype),
        compiler_params=pltpu.CompilerParams(dimension_semantics=("parallel",)),
    )(*arrs)


def adamw(g, w, m, v, name):
    rows, width = g.shape
    tr = _row_tile(rows, width)

    def body(g_ref, w_ref, m_ref, v_ref, d_ref, mo_ref, vo_ref):
        gg = g_ref[...]
        m_new = ADAM_B1 * m_ref[...] + (1.0 - ADAM_B1) * gg
        v_new = ADAM_B2 * v_ref[...] + (1.0 - ADAM_B2) * jnp.square(gg)
        m_hat = m_new / (1.0 - ADAM_B1 ** ADAM_STEP)
        v_hat = v_new / (1.0 - ADAM_B2 ** ADAM_STEP)
        d_ref[...] = -ADAM_LR * (m_hat / (jnp.sqrt(v_hat) + ADAM_EPS) + ADAM_WD * w_ref[...])
        mo_ref[...] = m_new
        vo_ref[...] = v_new

    spec = pl.BlockSpec((tr, width), lambda i: (i, 0))
    return pl.pallas_call(
        body, name=name, grid=(rows // tr,), in_specs=[spec] * 4, out_specs=[spec] * 3,
        out_shape=[jax.ShapeDtypeStruct((rows, width), F32)] * 3,
        compiler_params=pltpu.CompilerParams(dimension_semantics=("parallel",)),
    )(g, w, m, v)


def _place():
    return lax.axis_index("x"), lax.axis_index("y"), lax.axis_index("c")


def _other_chips(x, y):
    return [(1 - x, y), (x, 1 - y), (1 - x, 1 - y)]


def allgather8(x_shard, name):
    m_per, n = x_shard.shape

    def body(x_ref, out_ref, send_sems, recv_sems, local_sem):
        x, y, c = _place()
        me, sibling = (x, y, c), (x, y, 1 - c)
        chips = _other_chips(x, y)

        def rows(px, py, pc):
            return out_ref.at[pl.ds((4 * px + 2 * py + pc) * m_per, m_per), :]

        def copy(k, block, to, src=None):
            return pltpu.make_async_remote_copy(
                src_ref=rows(*block) if src is None else src, dst_ref=rows(*block),
                send_sem=send_sems.at[k], recv_sem=recv_sems.at[k], device_id=to, device_id_type=MESH_ID)

        mine = pltpu.make_async_copy(x_ref, rows(*me), local_sem)
        mine.start()
        first = [copy(0, me, sibling, src=x_ref)]
        first += [copy(1 + j, me, (*chip, c), src=x_ref) for j, chip in enumerate(chips)]
        for cp in first:
            cp.start()
        passed = [copy(4 + j, (*chip, c), sibling) for j, chip in enumerate(chips)]
        for j, chip in enumerate(chips):
            copy(1 + j, (*chip, c), me).wait_recv()
            passed[j].start()
        copy(0, sibling, me).wait_recv()
        for j, chip in enumerate(chips):
            copy(4 + j, (*chip, 1 - c), me).wait_recv()
        for cp in first + passed:
            cp.wait_send()
        mine.wait()

    return pl.pallas_call(
        body, name=name,
        out_shape=jax.ShapeDtypeStruct((8 * m_per, n), x_shard.dtype),
        in_specs=[pl.BlockSpec(memory_space=pltpu.VMEM)],
        out_specs=pl.BlockSpec(memory_space=pltpu.VMEM),
        scratch_shapes=[pltpu.SemaphoreType.DMA((7,)), pltpu.SemaphoreType.DMA((7,)), pltpu.SemaphoreType.DMA],
    )(x_shard)


_ANY = pl.BlockSpec(memory_space=pl.ANY)


def gather_chips(shards, name):
    n = len(shards)

    def body(*refs):
        ins, outs = refs[:n], refs[n:2 * n]
        send_sems, recv_sems, local_sems = refs[2 * n:]
        x, y, c = _place()
        chips = _other_chips(x, y)
        started = []
        for a in range(n):
            loc = pltpu.make_async_copy(ins[a], outs[a].at[2 * x + y], local_sems.at[a])
            loc.start()
            started.append(loc)
        sends = []
        for a in range(n):
            for j, chip in enumerate(chips):
                cp = pltpu.make_async_remote_copy(
                    src_ref=ins[a], dst_ref=outs[a].at[2 * x + y], send_sem=send_sems.at[3 * a + j],
                    recv_sem=recv_sems.at[3 * a + j], device_id=(*chip, c), device_id_type=MESH_ID)
                cp.start()
                sends.append(cp)
        for a in range(n):
            for j, chip in enumerate(chips):
                pltpu.make_async_remote_copy(
                    src_ref=ins[a], dst_ref=outs[a].at[2 * chip[0] + chip[1]], send_sem=send_sems.at[3 * a + j],
                    recv_sem=recv_sems.at[3 * a + j], device_id=(*chip, c), device_id_type=MESH_ID).wait_recv()
        for cp in sends:
            cp.wait_send()
        for loc in started:
            loc.wait()

    return pl.pallas_call(
        body, name=name,
        out_shape=[jax.ShapeDtypeStruct((4,) + s.shape, s.dtype) for s in shards],
        in_specs=[_ANY] * n, out_specs=[_ANY] * n,
        scratch_shapes=[pltpu.SemaphoreType.DMA((3 * n,)), pltpu.SemaphoreType.DMA((3 * n,)), pltpu.SemaphoreType.DMA((n,))],
    )(*shards)


def rs_sibling(grads, name):
    n = len(grads)

    def body(*refs):
        ins, mine, got = refs[:n], refs[n:2 * n], refs[2 * n:3 * n]
        send_sems, recv_sems, local_sems = refs[3 * n:]
        x, y, c = _place()
        pend = []
        for a in range(n):
            h = ins[a].shape[1] // 2
            loc = pltpu.make_async_copy(ins[a].at[:, pl.ds(pl.multiple_of(c * h, 8), h), :], mine[a], local_sems.at[a])
            cp = pltpu.make_async_remote_copy(
                src_ref=ins[a].at[:, pl.ds(pl.multiple_of((1 - c) * h, 8), h), :], dst_ref=got[a],
                send_sem=send_sems.at[a], recv_sem=recv_sems.at[a], device_id=(x, y, 1 - c), device_id_type=MESH_ID)
            loc.start()
            cp.start()
            pend.append((loc, cp))
        for loc, cp in pend:
            cp.wait()
            loc.wait()

    half = [jax.ShapeDtypeStruct((g.shape[0], g.shape[1] // 2, g.shape[2]), g.dtype) for g in grads]
    return pl.pallas_call(
        body, name=name, out_shape=half + half, in_specs=[_ANY] * n, out_specs=[_ANY] * (2 * n),
        scratch_shapes=[pltpu.SemaphoreType.DMA((n,)), pltpu.SemaphoreType.DMA((n,)), pltpu.SemaphoreType.DMA((n,))],
    )(*grads)


def rs_chips(parts, name):
    n = len(parts)

    def body(*refs):
        ins, mine, got = refs[:n], refs[n:2 * n], refs[2 * n:3 * n]
        send_sems, recv_sems, local_sems = refs[3 * n:]
        x, y, c = _place()
        chips = _other_chips(x, y)
        pend = []
        for a in range(n):
            loc = pltpu.make_async_copy(ins[a].at[2 * x + y], mine[a], local_sems.at[a])
            loc.start()
            pend.append(loc)
            for j, chip in enumerate(chips):
                cp = pltpu.make_async_remote_copy(
                    src_ref=ins[a].at[2 * chip[0] + chip[1]], dst_ref=got[a].at[j],
                    send_sem=send_sems.at[3 * a + j], recv_sem=recv_sems.at[3 * a + j],
                    device_id=(*chip, c), device_id_type=MESH_ID)
                cp.start()
                pend.append(cp)
        for p in pend:
            p.wait()

    return pl.pallas_call(
        body, name=name,
        out_shape=[jax.ShapeDtypeStruct(p.shape[1:], p.dtype) for p in parts]
        + [jax.ShapeDtypeStruct((3,) + p.shape[1:], p.dtype) for p in parts],
        in_specs=[_ANY] * n, out_specs=[_ANY] * (2 * n),
        scratch_shapes=[pltpu.SemaphoreType.DMA((3 * n,)), pltpu.SemaphoreType.DMA((3 * n,)), pltpu.SemaphoreType.DMA((n,))],
    )(*parts)


def share_sibling(halves, name):
    n = len(halves)

    def body(*refs):
        ins, outs = refs[:n], refs[n:2 * n]
        send_sems, recv_sems, local_sems = refs[2 * n:]
        x, y, c = _place()
        pend = []
        for a in range(n):
            loc = pltpu.make_async_copy(ins[a], outs[a].at[c], local_sems.at[a])
            cp = pltpu.make_async_remote_copy(
                src_ref=ins[a], dst_ref=outs[a].at[c], send_sem=send_sems.at[a], recv_sem=recv_sems.at[a],
                device_id=(x, y, 1 - c), device_id_type=MESH_ID)
            loc.start()
            cp.start()
            pend.append((loc, cp))
        for a, (loc, cp) in enumerate(pend):
            cp.wait_send()
            pltpu.make_async_remote_copy(
                src_ref=ins[a], dst_ref=outs[a].at[1 - c], send_sem=send_sems.at[a], recv_sem=recv_sems.at[a],
                device_id=(x, y, 1 - c), device_id_type=MESH_ID).wait_recv()
            loc.wait()

    return pl.pallas_call(
        body, name=name, out_shape=[jax.ShapeDtypeStruct((2,) + h.shape, h.dtype) for h in halves],
        in_specs=[_ANY] * n, out_specs=[_ANY] * n,
        scratch_shapes=[pltpu.SemaphoreType.DMA((n,)), pltpu.SemaphoreType.DMA((n,)), pltpu.SemaphoreType.DMA((n,))],
    )(*halves)


def reduce_scatter(grads, tag):
    mine, got = _split(rs_sibling(grads, name=f"rs_sibling_{tag}"))
    pair = [ew_sum([a.reshape(-1, a.shape[2]), b.reshape(-1, b.shape[2])], name=f"rs_pair_{tag}_{i}").reshape(a.shape)
            for i, (a, b) in enumerate(zip(mine, got))]
    own, recv = _split(rs_chips(pair, name=f"rs_chips_{tag}"))
    tot = [ew_sum([a, b[0], b[1], b[2]], name=f"rs_quad_{tag}_{i}") for i, (a, b) in enumerate(zip(own, recv))]
    both = share_sibling(tot, name=f"rs_share_{tag}")
    return [b.reshape(-1, b.shape[2]) for b in both]


def _split(lst):
    n = len(lst) // 2
    return lst[:n], lst[n:]


def _sibling():
    x, y, c = _place()
    return (x, y, 1 - c)


def send_rows(src, name):
    r, c = src.shape
    tr = _row_tile(r, c)
    n = r // tr

    def body(x_ref, out_ref, send_sem, recv_sem):
        i = pl.program_id(0)
        cp = pltpu.make_async_remote_copy(
            src_ref=x_ref, dst_ref=out_ref.at[pl.ds(pl.multiple_of(i * tr, 8), tr), :], send_sem=send_sem,
            recv_sem=recv_sem, device_id=_sibling(), device_id_type=MESH_ID)
        cp.start()
        cp.wait_send()

        @pl.when(i == n - 1)
        def _():
            pltpu.make_async_remote_copy(src_ref=out_ref, dst_ref=out_ref, send_sem=send_sem, recv_sem=recv_sem,
                                         device_id=_sibling(), device_id_type=MESH_ID).wait_recv()

    return pl.pallas_call(
        body, name=name, grid=(n,), in_specs=[pl.BlockSpec((tr, c), lambda i: (i, 0))], out_specs=_ANY,
        out_shape=jax.ShapeDtypeStruct((r, c), src.dtype),
        scratch_shapes=[pltpu.SemaphoreType.DMA, pltpu.SemaphoreType.DMA],
        compiler_params=pltpu.CompilerParams(dimension_semantics=("arbitrary",)),
    )(src)


def gather_chips(shards, name):
    n = len(shards)

    def body(*refs):
        ins, outs = refs[:n], refs[n:2 * n]
        send_sems, recv_sems = refs[2 * n:]
        x, y, c = _place()
        chips = _other_chips(x, y)
        sends = []
        for a in range(n):
            for j, chip in enumerate(chips):
                cp = pltpu.make_async_remote_copy(
                    src_ref=ins[a], dst_ref=outs[a].at[2 * x + y], send_sem=send_sems.at[3 * a + j],
                    recv_sem=recv_sems.at[3 * a + j], device_id=(*chip, c), device_id_type=MESH_ID)
                cp.start()
                sends.append(cp)
        for a in range(n):
            for j, chip in enumerate(chips):
                pltpu.make_async_remote_copy(
                    src_ref=ins[a], dst_ref=outs[a].at[2 * chip[0] + chip[1]], send_sem=send_sems.at[3 * a + j],
                    recv_sem=recv_sems.at[3 * a + j], device_id=(*chip, c), device_id_type=MESH_ID).wait_recv()
        for cp in sends:
            cp.wait_send()

    res = pl.pallas_call(
        body, name=name,
        out_shape=[jax.ShapeDtypeStruct((4,) + s.shape, s.dtype) for s in shards],
        in_specs=[_ANY] * n, out_specs=[_ANY] * n,
        scratch_shapes=[pltpu.SemaphoreType.DMA((3 * n,)), pltpu.SemaphoreType.DMA((3 * n,))],
    )(*shards)
    x, y, _ = _place()
    return [lax.dynamic_update_slice(g, s[None], (2 * x + y,) + (0,) * s.ndim) for g, s in zip(res, shards)]


def gather_chips(shards, name):
    n = len(shards)

    def body(*refs):
        ins, outs = refs[:n], refs[n:2 * n]
        send_sems, recv_sems = refs[2 * n:]
        x, y, c = _place()
        chips = _other_chips(x, y)
        sends = []
        for a in range(n):
            h = ins[a].shape[0] // 2
            src = ins[a].at[pl.ds(pl.multiple_of(c * h, 16), h), :]
            for j, chip in enumerate(chips):
                cp = pltpu.make_async_remote_copy(
                    src_ref=src, dst_ref=outs[a].at[2 * x + y], send_sem=send_sems.at[3 * a + j],
                    recv_sem=recv_sems.at[3 * a + j], device_id=(*chip, c), device_id_type=MESH_ID)
                cp.start()
                sends.append(cp)
        for a in range(n):
            h = ins[a].shape[0] // 2
            src = ins[a].at[pl.ds(pl.multiple_of(c * h, 16), h), :]
            for j, chip in enumerate(chips):
                pltpu.make_async_remote_copy(
                    src_ref=src, dst_ref=outs[a].at[2 * chip[0] + chip[1]], send_sem=send_sems.at[3 * a + j],
                    recv_sem=recv_sems.at[3 * a + j], device_id=(*chip, c), device_id_type=MESH_ID).wait_recv()
        for cp in sends:
            cp.wait_send()

    halves = pl.pallas_call(
        body, name=name,
        out_shape=[jax.ShapeDtypeStruct((4, s.shape[0] // 2, s.shape[1]), s.dtype) for s in shards],
        in_specs=[_ANY] * n, out_specs=[_ANY] * n,
        scratch_shapes=[pltpu.SemaphoreType.DMA((3 * n,)), pltpu.SemaphoreType.DMA((3 * n,))],
    )(*shards)
    x, y, ci = _place()
    res = []
    for a, (g, s) in enumerate(zip(halves, shards)):
        h = s.shape[0] // 2
        g = lax.dynamic_update_slice(g, lax.dynamic_slice_in_dim(s, ci * h, h, 0)[None], (2 * x + y, 0, 0))
        other = send_rows(g.reshape(4 * h, s.shape[1]), f"{name}_sib_{a}").reshape(g.shape)
        res.append(jnp.concatenate([jnp.where(ci == 0, g, other), jnp.where(ci == 0, other, g)], 1))
    return res


def rs_chips(parts, name):
    n = len(parts)

    def body(*refs):
        ins, got = refs[:n], refs[n:2 * n]
        send_sems, recv_sems = refs[2 * n:]
        x, y, c = _place()
        chips = _other_chips(x, y)
        pend = []
        for a in range(n):
            for j, chip in enumerate(chips):
                cp = pltpu.make_async_remote_copy(
                    src_ref=ins[a].at[2 * chip[0] + chip[1]], dst_ref=got[a].at[j],
                    send_sem=send_sems.at[3 * a + j], recv_sem=recv_sems.at[3 * a + j],
                    device_id=(*chip, c), device_id_type=MESH_ID)
                cp.start()
                pend.append(cp)
        for p in pend:
            p.wait()

    return pl.pallas_call(
        body, name=name,
        out_shape=[jax.ShapeDtypeStruct((3,) + p.shape[1:], p.dtype) for p in parts],
        in_specs=[_ANY] * n, out_specs=[_ANY] * n,
        scratch_shapes=[pltpu.SemaphoreType.DMA((3 * n,)), pltpu.SemaphoreType.DMA((3 * n,))],
    )(*parts)


def reduce_scatter(grads, tag):
    x, y, ci = _place()
    chip = 2 * x + y
    out = []
    pairs = []
    for i, g in enumerate(grads):
        _, r, c = g.shape
        h = r // 2
        keep = lax.dynamic_slice_in_dim(g, ci * h, h, 1).reshape(4 * h, c)
        give = lax.dynamic_slice_in_dim(g, (1 - ci) * h, h, 1).reshape(4 * h, c).astype(MXU_DTYPE)
        got = send_rows(give, f"rs_sibling_{tag}_{i}")
        pairs.append(ew_sum([keep, got], name=f"rs_pair_{tag}_{i}").reshape(4, h, c))
    recv = rs_chips([p.astype(MXU_DTYPE) for p in pairs], f"rs_chips_{tag}")
    for i, (p, b) in enumerate(zip(pairs, recv)):
        own = lax.dynamic_index_in_dim(p, chip, 0, keepdims=False)
        tot = ew_sum([own, b[0], b[1], b[2]], name=f"rs_quad_{tag}_{i}")
        other = send_rows(tot, f"rs_share_{tag}_{i}")
        h = tot.shape[0]
        out.append(lax.dynamic_update_slice(jnp.concatenate([other, other], 0), tot, (ci * h, 0)))
    return out


def sum_entries(g, idxs, name):
    _, rows, width = g.shape
    tr = _row_tile(rows, width)

    def body(g_ref, o_ref):
        acc = g_ref[idxs[0]]
        for d in idxs[1:]:
            acc = acc + g_ref[d]
        o_ref[...] = acc

    return pl.pallas_call(
        body, name=name, grid=(rows // tr,),
        in_specs=[pl.BlockSpec((8, tr, width), lambda i: (0, i, 0))], out_specs=pl.BlockSpec((tr, width), lambda i: (i, 0)),
        out_shape=jax.ShapeDtypeStruct((rows, width), F32),
        compiler_params=pltpu.CompilerParams(dimension_semantics=("parallel",)),
    )(g)


def _pack(arrs, rows_multiple=8):
    parts, offs, r = [], [], 0
    for a in arrs:
        flat = a.reshape(-1).astype(F32)
        nrow = -(-flat.shape[0] // LANES)
        parts.append(jnp.pad(flat, (0, nrow * LANES - flat.shape[0])).reshape(nrow, LANES))
        offs.append((r, nrow, a.shape))
        r += nrow
    pad = (-r) % rows_multiple
    if pad:
        parts.append(jnp.zeros((pad, LANES), F32))
    return jnp.concatenate(parts, 0), offs


def _unpack(slab, offs):
    outs = []
    for r, nrow, shape in offs:
        size = 1
        for s in shape:
            size *= s
        outs.append(slab[r:r + nrow].reshape(-1)[:size].reshape(shape))
    return outs


def _seqflip(a, lc):
    return jnp.concatenate([jnp.flip(a[:lc], 0), jnp.flip(a[lc:], 0)], 0)


def _slab(vec8):
    return jnp.pad(vec8.reshape(1, -1).astype(F32), ((0, 0), (0, LANES - vec8.size)))


def _rope_tables(n_lat, lc):
    rows = n_lat // GRID_W
    row = jnp.repeat(jnp.arange(rows, dtype=F32), GRID_W)
    col = jnp.tile(jnp.arange(GRID_W, dtype=F32), rows)
    n_freq = HEAD_DIM // 4
    inv = ROPE_THETA ** (-jnp.arange(n_freq, dtype=F32) / n_freq)
    ang = jnp.concatenate([row[:, None] * inv, col[:, None] * inv], -1)
    cos, sin = jnp.cos(ang), jnp.sin(ang)
    cos_t = jnp.concatenate([jnp.ones((lc, HEAD_DIM), F32), jnp.concatenate([cos, cos], -1)], 0)
    sin_t = jnp.concatenate([jnp.zeros((lc, HEAD_DIM), F32), jnp.concatenate([-sin, sin], -1)], 0)
    return cos_t, sin_t


def _permute_w_in(w):
    return jnp.concatenate([w[:, :4096], w[:, 4112:], w[:, 4096:4112], jnp.zeros((w.shape[0], PROJ_PAD - PROJ_W), w.dtype)], 1)


def _unpermute_dw_in(dw):
    return jnp.concatenate([dw[:, :4096], dw[:, 5632:5648], dw[:, 4096:5632]], 1)


RET_SELS = [[(d * RET_HEADS + h, d == 1) for h in range(RET_HEADS)] for d in range(2)]
DN_SELS = [[(d * DN_HEADS + h, (2 + d) * DN_HEADS + h, d == 1) for h in range(DN_HEADS)] for d in range(2)]


def _layer_fwd(xin, mods, wts, prm, tabs, lc, tm, i):
    nct = lc // tm
    seg = lambda j: mods[:, j:j + 1, :]
    cos_t, sin_t = tabs
    sv = {}
    (h1,) = rowwise(f"mod1_{i}", modulate_fn, [xin], [seg(0), seg(1)], [], [(D_MODEL, MXU_DTYPE)], tm, nct)
    p = mm(h1, wts["w_in"], "nn", name=f"proj_in_{i}")
    rq, rk = rowwise(f"ret_prep_{i}", ret_prep_fn, [(p, 512, 0), (p, 512, 1), cos_t, sin_t], [], [],
                     [(512, F32), (512, F32)], tm, nct)
    rv = p[:, 1024:1536]
    r_of, r_ob, r_st = scan_fwd(ret_chunk, RET_CHUNK, rq, rk, rv, prm["ret_logit"], RET_SELS, lc, f"ret_scan_{i}")
    (y_ret,) = rowwise(f"ret_out_{i}", ret_out_fn, [r_of, r_ob, (p, 512, 3)], [], [], [(512, F32)], tm, nct)
    qkvc = conv_fwd(p, 2048, 3 * 512, prm["conv_w"], lc, f"dn_conv_{i}")
    dq, dk, gb = rowwise(f"dn_prep_{i}", dn_prep_fn, [(qkvc, 512, 0), (qkvc, 512, 1), (p, LANES, 44)], [],
                         [prm["a_log"], prm["dt_b"]], [(512, F32), (512, F32), (LANES, F32)], tm, nct)
    dv = qkvc[:, 1024:1536]
    d_of, d_ob, d_st = scan_fwd(dn_chunk, DN_CHUNK, dq, dk, dv, gb, DN_SELS, lc, f"dn_scan_{i}")
    (y_dn,) = rowwise(f"dn_out_{i}", dn_out_fn, [d_of, d_ob, (p, 512, 7)], [], [prm["dn_norm_w"]], [(512, F32)], tm, nct)
    aq, ak, av = rowwise(f"att_prep_{i}", lambda q, k, v, *rest: (lambda qk: (qk[0] * (QK_SCALE * LOG2E), qk[1], v))(att_prep_fn(q, k, *rest)),
                         [(p, 1024, 4), (p, 256, 20), (p, 256, 21), cos_t, sin_t], [], [prm["qn_w"], prm["kn_w"]],
                         [(1024, MXU_DTYPE), (256, MXU_DTYPE), (256, MXU_DTYPE)], tm, nct)
    ao, lse = attn_fwd(aq, ak, av, lc, f"attn_fwd_{i}")
    y = jnp.concatenate([y_ret, y_dn, ao], 1)
    a1 = mm(y, wts["w_o"], "nn", name=f"proj_out_{i}")
    (x1,) = rowwise(f"postnorm1_{i}", postnorm_fn, [xin, a1], [seg(2)], [prm["ln1_w"], prm["ln1_b"]], [(D_MODEL, F32)], tm, nct)
    (h2,) = rowwise(f"mod2_{i}", modulate_fn, [x1], [seg(3), seg(4)], [], [(D_MODEL, MXU_DTYPE)], tm, nct)
    u = mm(h2, wts["w_ffn_in"], "nn", out_dtype=MXU_DTYPE, name=f"ffn_in_{i}")
    act = swiglu_fwd(u, tm, f"swiglu_{i}")
    a2 = mm(act, wts["w_ffn_out"], "nn", name=f"ffn_out_{i}")
    (x2,) = rowwise(f"postnorm2_{i}", postnorm_fn, [x1, a2], [seg(5)], [prm["ln2_w"], prm["ln2_b"]], [(D_MODEL, F32)], tm, nct)
    sv.update(xin=xin, h1=h1, p=p, rq=rq, rk=rk, rv=rv, r_st=r_st, r_of=r_of, r_ob=r_ob, qkvc=qkvc, dq=dq, dk=dk, dv=dv,
              gb=gb, d_st=d_st, d_of=d_of, d_ob=d_ob, aq=aq, ak=ak, av=av, ao=ao, lse=lse, y=y, a1=a1, x1=x1, h2=h2,
              u=u, act=act, a2=a2)
    return x2, sv


def _layer_bwd(dx2, sv, mods, wts, prm, tabs, lc, tm, i):
    nct = lc // tm
    seg = lambda j: mods[:, j:j + 1, :]
    cos_t, sin_t = tabs
    p = sv["p"]
    both = lambda g: g[0] + g[1]
    (dx1a, da2), (dgate2,), (dln2w, dln2b) = rowwise_bwd(
        f"postnorm2_b_{i}", postnorm_fn, [sv["x1"], sv["a2"]], [seg(5)], [prm["ln2_w"], prm["ln2_b"]], [dx2], [True, True], tm, nct)
    dact = mm(da2, wts["w_ffn_out"], "nt", name=f"ffn_out_dx_{i}")
    dw_ffn_out = mm(sv["act"], da2, "tn", name=f"ffn_out_dw_{i}")
    du = swiglu_bwd(sv["u"], dact, tm, f"swiglu_b_{i}")
    dh2 = mm(du, wts["w_ffn_in"], "nt", name=f"ffn_in_dx_{i}")
    dw_ffn_in = mm(sv["h2"], du, "tn", name=f"ffn_in_dw_{i}")
    (dx1b,), (dshift2, dscale2), _ = rowwise_bwd(
        f"mod2_b_{i}", modulate_fn, [sv["x1"]], [seg(3), seg(4)], [], [dh2], [True], tm, nct)
    dx1 = ew_sum([dx1a, dx1b], name=f"dx1_{i}")
    (dxa, da1), (dgate1,), (dln1w, dln1b) = rowwise_bwd(
        f"postnorm1_b_{i}", postnorm_fn, [sv["xin"], sv["a1"]], [seg(2)], [prm["ln1_w"], prm["ln1_b"]], [dx1], [True, True], tm, nct)
    dy = mm(da1, wts["w_o"], "nt", name=f"proj_out_dx_{i}")
    dw_o = mm(sv["y"], da1, "tn", name=f"proj_out_dw_{i}")
    dy_ret, dy_dn, dao = dy[:, :512], dy[:, 512:1024], dy[:, 1024:]
    daq, delta = attn_bwd_dq(sv["aq"], sv["ak"], sv["av"], sv["ao"], dao, sv["lse"], lc, f"attn_dq_{i}")
    dak, dav = attn_bwd_dkv(sv["aq"], sv["ak"], sv["av"], dao, sv["lse"], delta, lc, f"attn_dkv_{i}")
    (dp_aq, dp_ak), _, (dqn_w, dkn_w) = rowwise_bwd(
        f"att_prep_b_{i}", att_prep_fn, [(p, 1024, 4), (p, 256, 20), cos_t, sin_t], [], [prm["qn_w"], prm["kn_w"]],
        [daq, dak], [True, True, False, False], tm, nct)
    (dd_o, dp_z), _, (ddn_norm_w,) = rowwise_bwd(
        f"dn_out_b_{i}", dn_out_fn, [sv["d_of"], sv["d_ob"], (p, 512, 7)], [], [prm["dn_norm_w"]], [dy_dn], [True, False, True], tm, nct)
    dqf, dkf, dvf, dqb, dkb, dvb, dgbf, dgbb = scan_bwd(
        dn_chunk, DN_CHUNK, sv["dq"], sv["dk"], sv["dv"], sv["gb"], sv["d_st"], dd_o, DN_SELS, lc, f"dn_scan_b_{i}")
    ddv = ew_sum([dvf, dvb], name=f"dn_dv_{i}")
    (dqc, dkc, dp_ab), _, (da_log, ddt_b) = rowwise_bwd(
        f"dn_prep_b_{i}", dn_prep_fn, [(sv["qkvc"], 512, 0), (sv["qkvc"], 512, 1), (p, LANES, 44)], [],
        [prm["a_log"], prm["dt_b"]], [[dqf, dqb], [dkf, dkb], [dgbf, dgbb]], [True, True, True], tm, nct)
    dqkvc = jnp.concatenate([dqc, dkc, ddv], 1)
    dp_qkv, dconv_w = conv_bwd(p, 2048, 3 * 512, prm["conv_w"], dqkvc, lc, f"dn_conv_b_{i}")
    (dr_o, dp_g), _, _ = rowwise_bwd(
        f"ret_out_b_{i}", ret_out_fn, [sv["r_of"], sv["r_ob"], (p, 512, 3)], [], [], [dy_ret], [True, False, True], tm, nct)
    drqf, drkf, drvf, drqb, drkb, drvb, dlogit = scan_bwd(
        ret_chunk, RET_CHUNK, sv["rq"], sv["rk"], sv["rv"], prm["ret_logit"], sv["r_st"], dr_o, RET_SELS, lc, f"ret_scan_b_{i}")
    drv = ew_sum([drvf, drvb], name=f"ret_dv_{i}")
    dret_logit = dlogit[0, :2 * RET_HEADS].reshape(2, RET_HEADS)
    (dp_rq, dp_rk), _, _ = rowwise_bwd(
        f"ret_prep_b_{i}", ret_prep_fn, [(p, 512, 0), (p, 512, 1), cos_t, sin_t], [], [], [[drqf, drqb], [drkf, drkb]],
        [True, True, False, False], tm, nct)
    dp = jnp.concatenate([dp_rq, dp_rk, drv, dp_g, dp_qkv, dp_z, dp_aq, dp_ak, dav, dp_ab], 1)
    dh1 = mm(dp, wts["w_in"], "nt", name=f"proj_in_dx_{i}")
    dw_in = mm(sv["h1"], dp, "tn", name=f"proj_in_dw_{i}")
    (dxb,), (dshift1, dscale1), _ = rowwise_bwd(
        f"mod1_b_{i}", modulate_fn, [sv["xin"]], [seg(0), seg(1)], [], [dh1], [True], tm, nct)
    dxin = ew_sum([dxa, dxb], name=f"dxin_{i}")
    dmods = jnp.concatenate([dshift1, dscale1, dgate1, dshift2, dscale2, dgate2], 1)
    big = dict(w_in=dw_in, w_o=dw_o, w_ffn_in=dw_ffn_in, w_ffn_out=dw_ffn_out)
    small = dict(ln1_w=both(dln1w)[0], ln1_b=both(dln1b)[0], ln2_w=both(dln2w)[0], ln2_b=both(dln2b)[0],
                 dn_norm_w=both(ddn_norm_w)[0], att_qn_w=both(dqn_w)[0], att_kn_w=both(dkn_w)[0],
                 dn_conv_w=dconv_w[:DN_CONV_K], ret_decay_logit=dret_logit,
                 dn_a_log=both(da_log)[0, :2 * DN_HEADS].reshape(2, DN_HEADS),
                 dn_dt_bias=both(ddt_b)[0, :2 * DN_HEADS].reshape(2, DN_HEADS))
    return dxin, dmods, big, small


BIG = ("w_in", "w_o", "w_ffn_in", "w_ffn_out")
SMALL = ("c_ctx", "b_ada", "ret_decay_logit", "dn_conv_w", "dn_a_log", "dn_dt_bias", "dn_norm_w", "att_qn_w", "att_kn_w",
         "ln1_w", "ln1_b", "ln2_w", "ln2_b")
WEIGHTS = ("c_ctx", "w_ada", "b_ada", "w_in", "ret_decay_logit", "dn_conv_w", "dn_a_log", "dn_dt_bias", "dn_norm_w",
           "att_qn_w", "att_kn_w", "w_o", "ln1_w", "ln1_b", "w_ffn_in", "w_ffn_out", "ln2_w", "ln2_b")


def _chip_major(g, name):
    if name in ("w_in", "w_ffn_in"):
        r, cols = g.shape
        return g.reshape(r, 4, cols // 4).transpose(1, 0, 2)
    return g.reshape(4, g.shape[0] // 4, g.shape[1])


def kernel(x, c, ctx, c_ctx, w_ada, b_ada, w_in, ret_decay_logit, dn_conv_w, dn_a_log, dn_dt_bias, dn_norm_w, att_qn_w, att_kn_w, w_o, ln1_w, ln1_b, w_ffn_in, w_ffn_out, ln2_w, ln2_b, loss_target, m_c_ctx, m_w_ada, m_b_ada, m_w_in, m_ret_decay_logit, m_dn_conv_w, m_dn_a_log, m_dn_dt_bias, m_dn_norm_w, m_att_qn_w, m_att_kn_w, m_w_o, m_ln1_w, m_ln1_b, m_w_ffn_in, m_w_ffn_out, m_ln2_w, m_ln2_b, v_c_ctx, v_w_ada, v_b_ada, v_w_in, v_ret_decay_logit, v_dn_conv_w, v_dn_a_log, v_dn_dt_bias, v_dn_norm_w, v_att_qn_w, v_att_kn_w, v_w_o, v_ln1_w, v_ln1_b, v_w_ffn_in, v_w_ffn_out, v_ln2_w, v_ln2_b):
    wv = dict(c_ctx=c_ctx, w_ada=w_ada, b_ada=b_ada, w_in=w_in, ret_decay_logit=ret_decay_logit, dn_conv_w=dn_conv_w,
              dn_a_log=dn_a_log, dn_dt_bias=dn_dt_bias, dn_norm_w=dn_norm_w, att_qn_w=att_qn_w, att_kn_w=att_kn_w, w_o=w_o,
              ln1_w=ln1_w, ln1_b=ln1_b, w_ffn_in=w_ffn_in, w_ffn_out=w_ffn_out, ln2_w=ln2_w, ln2_b=ln2_b)
    mv = dict(c_ctx=m_c_ctx, w_ada=m_w_ada, b_ada=m_b_ada, w_in=m_w_in, ret_decay_logit=m_ret_decay_logit,
              dn_conv_w=m_dn_conv_w, dn_a_log=m_dn_a_log, dn_dt_bias=m_dn_dt_bias, dn_norm_w=m_dn_norm_w,
              att_qn_w=m_att_qn_w, att_kn_w=m_att_kn_w, w_o=m_w_o, ln1_w=m_ln1_w, ln1_b=m_ln1_b, w_ffn_in=m_w_ffn_in,
              w_ffn_out=m_w_ffn_out, ln2_w=m_ln2_w, ln2_b=m_ln2_b)
    vv = dict(c_ctx=v_c_ctx, w_ada=v_w_ada, b_ada=v_b_ada, w_in=v_w_in, ret_decay_logit=v_ret_decay_logit,
              dn_conv_w=v_dn_conv_w, dn_a_log=v_dn_a_log, dn_dt_bias=v_dn_dt_bias, dn_norm_w=v_dn_norm_w,
              att_qn_w=v_att_qn_w, att_kn_w=v_att_kn_w, w_o=v_w_o, ln1_w=v_ln1_w, ln1_b=v_ln1_b, w_ffn_in=v_w_ffn_in,
              w_ffn_out=v_w_ffn_out, ln2_w=v_ln2_w, ln2_b=v_ln2_b)
    depth = w_in.shape[0]
    n_lat, lc = x.shape[1], ctx.shape[1]
    t = lc + n_lat
    tm = _pick(lc, (256, 128))
    xi, yi, ci = _place()
    bidx = 4 * xi + 2 * yi + ci
    chip = 2 * xi + yi
    ada_w = w_ada.shape[2]
    conv_sh = dn_conv_w.shape[2]

    slab0, offs0 = _pack([c, dn_conv_w])
    g0 = allgather8(slab0, "gather_cond").reshape(8, -1, LANES)
    c_all = jnp.concatenate([_unpack(g0[d], offs0)[0] for d in range(8)], 0)
    conv_full = jnp.concatenate([_unpack(g0[2 * k], offs0)[1] for k in range(4)], 2)
    c_raw = jnp.concatenate([c_all, c_ctx[None], jnp.zeros((LANES - 9, D_MODEL), F32)], 0)
    (cond,) = rowwise("cond_silu", lambda a: (_silu(a),), [c_raw], [], [], [(D_MODEL, F32)], LANES, 0)
    b_sh = lax.dynamic_slice(b_ada, (0, chip * ada_w), (depth, ada_w))
    mods_sh = []
    for i in range(depth):
        mi = mm(cond, w_ada[i], "nn", name=f"ada_{i}")
        (mi,) = rowwise(f"ada_bias_{i}", lambda a, b: (a + b,), [mi], [], [b_sh[i:i + 1]], [(ada_w, F32)], LANES, 0)
        mods_sh.append(mi[:16])
    slab1, offs1 = _pack([jnp.stack(mods_sh)])
    g1 = allgather8(slab1, "gather_mods").reshape(8, -1, LANES)
    mods_all = jnp.concatenate([_unpack(g1[2 * k], offs1)[0] for k in range(4)], 2)
    mod_lat = lax.dynamic_index_in_dim(mods_all, bidx, 1, keepdims=False)
    mod_ctx = mods_all[:, 8]
    mods = jnp.stack([mod_ctx, mod_lat], 1).reshape(depth, 2, 6, D_MODEL)

    wbf = {n: ew_sum([wv[n].reshape(-1, wv[n].shape[2])], name=f"cast_{n}", out_dtype=MXU_DTYPE).reshape(wv[n].shape) for n in BIG}
    layers_w = []
    for i in range(depth):
        g = gather_chips([wbf[n][i] for n in BIG], f"gather_w_{i}")
        gw = dict(zip(BIG, g))
        layers_w.append(dict(
            w_in=_permute_w_in(gw["w_in"].transpose(1, 0, 2).reshape(D_MODEL, PROJ_W)),
            w_o=gw["w_o"].reshape(D_MODEL, D_MODEL),
            w_ffn_in=gw["w_ffn_in"].transpose(1, 0, 2).reshape(D_MODEL, 2 * D_FF),
            w_ffn_out=gw["w_ffn_out"].reshape(D_FF, D_MODEL)))

    tabs = _rope_tables(n_lat, lc)
    prms = []
    for i in range(depth):
        prms.append(dict(
            ret_logit=_slab(ret_decay_logit[i].reshape(-1)), conv_w=conv_full[i],
            a_log=_slab(dn_a_log[i].reshape(-1)), dt_b=_slab(dn_dt_bias[i].reshape(-1)), dn_norm_w=dn_norm_w[i:i + 1],
            qn_w=att_qn_w[i:i + 1], kn_w=att_kn_w[i:i + 1], ln1_w=ln1_w[i:i + 1], ln1_b=ln1_b[i:i + 1],
            ln2_w=ln2_w[i:i + 1], ln2_b=ln2_b[i:i + 1]))

    rows = jnp.concatenate([ctx[0], x[0]], 0)
    saved = []
    for i in range(depth):
        rows, sv = _layer_fwd(rows, mods[i], layers_w[i], prms[i], tabs, lc, tm, i)
        saved.append(sv)
    loss_local, dy = loss_and_grad(rows[lc:], loss_target[0], tm, "loss")
    loss = lax.psum(loss_local[0, 0], ("x", "y", "c"))

    drows = jnp.concatenate([jnp.zeros((lc, D_MODEL), F32), dy], 0)
    dmods, big_g, small_g = [None] * depth, [None] * depth, [None] * depth
    for i in reversed(range(depth)):
        drows, dmods[i], big, small_g[i] = _layer_bwd(drows, saved[i], mods[i], layers_w[i], prms[i], tabs, lc, tm, i)
        big["w_in"] = _unpermute_dw_in(big["w_in"])
        big_g[i] = dict(zip(BIG, reduce_scatter([_chip_major(big[n], n) for n in BIG], str(i))))
    grad_x = drows[lc:][None]

    names = ("ln1_w", "ln1_b", "ln2_w", "ln2_b", "dn_norm_w", "att_qn_w", "att_kn_w", "dn_conv_w", "ret_decay_logit",
             "dn_a_log", "dn_dt_bias")
    loc = [jnp.stack(dmods)] + [jnp.stack([small_g[i][n] for i in range(depth)]) for n in names]
    slab2, offs2 = _pack(loc)
    g2 = allgather8(slab2, "gather_small").reshape(8, -1, LANES)
    tot = _unpack(sum_entries(g2, tuple(range(8)), "sum_small"), offs2)
    dmods_sum = tot[0]
    gsm = dict(zip(names, tot[1:]))
    dmod_lat = jnp.stack([_unpack(g2[d], offs2)[0][:, 1].reshape(depth, 6 * D_MODEL) for d in range(8)], 1)
    dmod_ctx = dmods_sum[:, 0].reshape(depth, 1, 6 * D_MODEL)
    dm = jnp.concatenate([dmod_lat, dmod_ctx, jnp.zeros((depth, LANES - 9, 6 * D_MODEL), F32)], 1)
    dm_sh = lax.dynamic_slice(dm, (0, 0, chip * ada_w), (depth, LANES, ada_w))
    g_w_ada = jnp.stack([mm(cond, dm_sh[i], "tn", name=f"ada_dw_{i}") for i in range(depth)])
    gsm["b_ada"] = ew_sum([dmods_sum[:, 0].reshape(-1, LANES), dmods_sum[:, 1].reshape(-1, LANES)], name="b_ada_sum").reshape(b_ada.shape)
    dctx_rows = jnp.concatenate([dm_sh[:, 8:9], jnp.zeros((depth, 15, ada_w), F32)], 1)
    part = ew_sum([mm(dctx_rows[i], w_ada[i], "nt", name=f"ada_dcond_{i}") for i in range(depth)], name="ada_dcond_sum")[0]
    slab3, offs3 = _pack([part])
    g3 = allgather8(slab3, "gather_dcond").reshape(8, -1, LANES)
    dcond_ctx = _unpack(sum_entries(g3, (0, 2, 4, 6), "sum_dcond"), offs3)[0]
    (dc_ctx,), _, _ = rowwise_bwd("c_ctx_silu_b", lambda a: (_silu(a),), [c_ctx.reshape(16, LANES)], [], [],
                                  [dcond_ctx.reshape(16, LANES)], [True], 16, 0)
    gsm["c_ctx"] = dc_ctx.reshape(c_ctx.shape)
    gsm["dn_conv_w"] = lax.dynamic_slice(gsm["dn_conv_w"], (0, 0, chip * conv_sh), (depth, DN_CONV_K, conv_sh))

    grads, delta, new_m, new_v = {}, {}, {}, {}
    gs, offs = _pack([gsm[n] for n in SMALL])
    ws, _ = _pack([wv[n] for n in SMALL])
    ms, _ = _pack([mv[n] for n in SMALL])
    vs, _ = _pack([vv[n] for n in SMALL])
    res = [_unpack(o, offs) for o in adamw(gs, ws, ms, vs, "adamw_small")]
    for j, n in enumerate(SMALL):
        grads[n], delta[n], new_m[n], new_v[n] = gsm[n], res[0][j], res[1][j], res[2][j]
    bigs = {n: jnp.stack([big_g[i][n] for i in range(depth)]) for n in BIG}
    bigs["w_ada"] = g_w_ada
    for n, g in bigs.items():
        shp = wv[n].shape
        flat = lambda a: a.reshape(-1, shp[2])
        d_, m_, v_ = adamw(flat(g), flat(wv[n]), flat(mv[n]), flat(vv[n]), f"adamw_{n}")
        grads[n], delta[n], new_m[n], new_v[n] = g.reshape(shp), d_.reshape(shp), m_.reshape(shp), v_.reshape(shp)
    return (loss, grad_x, *[grads[n] for n in WEIGHTS], *[delta[n] for n in WEIGHTS], *[new_m[n] for n in WEIGHTS],
            *[new_v[n] for n in WEIGHTS])
```

```python
import functools

import jax
import jax.numpy as jnp
from jax import lax
from jax.experimental import pallas as pl
from jax.experimental.pallas import tpu as pltpu

F32 = jnp.float32
MXU_DTYPE = jnp.bfloat16

D_MODEL = 2048
DEPTH = 4
GRID_W = 64
HEAD_DIM = 128
RET_HEADS = 4
DN_HEADS = 4
ATT_HEADS = 8
ATT_KV_HEADS = 2
RET_CHUNK = 128
DN_CHUNK = 64
DN_CONV_K = 5
ROPE_THETA = 10000.0
D_FF = 5632
PROJ_W = 5648
PROJ_PAD = 5760
DEEPNORM_ALPHA = (2 * DEPTH) ** 0.25
EPS = 1e-6
QK_SCALE = HEAD_DIM ** -0.5
ADAM_LR = 0.001
ADAM_B1 = 0.9
ADAM_B2 = 0.999
ADAM_EPS = 1e-08
ADAM_WD = 0.01
ADAM_STEP = 10
LANES = 128
MM_VMEM_BUDGET = 44 * 1024 * 1024
MESH_ID = pl.DeviceIdType.MESH
NEG_BIG = -1e30


def _pick(n, prefs):
    for p in prefs:
        if n % p == 0:
            return p
    return n


def _contract(a, b, ca, cb, hp):
    dims = (((ca,), (cb,)), ((), ()))
    if hp and MXU_DTYPE == F32:
        return lax.dot_general(a, b, dims, precision=lax.Precision.HIGHEST, preferred_element_type=F32)
    if hp:
        ah, bh = a.astype(MXU_DTYPE), b.astype(MXU_DTYPE)
        al = (a - ah.astype(F32)).astype(MXU_DTYPE)
        bl = (b - bh.astype(F32)).astype(MXU_DTYPE)
        dot = lambda u, v: lax.dot_general(u, v, dims, preferred_element_type=F32)
        return dot(ah, bh) + (dot(ah, bl) + dot(al, bh))
    return lax.dot_general(a.astype(MXU_DTYPE), b.astype(MXU_DTYPE), dims, preferred_element_type=F32)


@functools.partial(jax.custom_vjp, nondiff_argnums=(2, 3, 4))
def mdot(a, b, ca, cb, hp=False):
    return _contract(a, b, ca, cb, hp)


def _mdot_fwd(a, b, ca, cb, hp):
    return _contract(a, b, ca, cb, hp), (a, b)


def _mdot_bwd(ca, cb, hp, res, g):
    a, b = res
    da = mdot(g, b, 1, 1 - cb, hp) if ca == 1 else mdot(b, g, 1 - cb, 1, hp)
    db = mdot(a, g, 1 - ca, 0, hp) if cb == 0 else mdot(g, a, 0, 1 - ca, hp)
    return da, db


mdot.defvjp(_mdot_fwd, _mdot_bwd)


@jax.custom_vjp
def swap_halves(x):
    return pltpu.roll(x, HEAD_DIM // 2, 1)


def _swap_fwd(x):
    return swap_halves(x), None


def _swap_bwd(_, g):
    return (swap_halves(g),)


swap_halves.defvjp(_swap_fwd, _swap_bwd)


@jax.custom_vjp
def tri_inv(a):
    n = a.shape[0]
    r = lax.broadcasted_iota(jnp.int32, (n, n), 0)
    c = lax.broadcasted_iota(jnp.int32, (n, n), 1)
    eye = (r == c).astype(F32)
    p = -a
    t = eye + p
    k = 2
    while k < n:
        p = _contract(p, p, 1, 0, True)
        t = t + _contract(t, p, 1, 0, True)
        k *= 2
    return t


def _tri_inv_fwd(a):
    t = tri_inv(a)
    return t, t


def _tri_inv_bwd(t, g):
    return (-_contract(_contract(t, g, 0, 0, True), t, 1, 1, True),)


tri_inv.defvjp(_tri_inv_fwd, _tri_inv_bwd)


def _sigmoid(x):
    return 1.0 / (1.0 + jnp.exp(-x))


def _silu(x):
    return x * _sigmoid(x)


def _softplus(x):
    return jnp.maximum(x, 0.0) + jnp.log(1.0 + jnp.exp(-jnp.abs(x)))


def _lane_pick(slab, idx):
    lane = lax.broadcasted_iota(jnp.int32, slab.shape, 1)
    return jnp.sum(jnp.where(lane == idx, slab, 0.0), axis=1, keepdims=True)


def _heads(x):
    return [x[:, h * HEAD_DIM:(h + 1) * HEAD_DIM] for h in range(x.shape[1] // HEAD_DIM)]


def _rope(x, cos, sin):
    return x * cos + swap_halves(x) * sin


def _rms(x):
    return x * lax.rsqrt(jnp.mean(x * x, -1, keepdims=True) + EPS)


def _l2n(x):
    return x * lax.rsqrt(jnp.sum(x * x, -1, keepdims=True) + EPS)


def mm(a, b, mode, out_dtype=F32, name="mm"):
    if mode == "nn":
        (m, k), (k2, n) = a.shape, b.shape
        ca, cb = 1, 0
    elif mode == "nt":
        (m, k), (n, k2) = a.shape, b.shape
        ca, cb = 1, 1
    else:
        (k, m), (k2, n) = a.shape, b.shape
        ca, cb = 0, 0
    assert k == k2, (a.shape, b.shape, mode)
    tm = _pick(m, (768, 512, 384, 256, 128))
    tn = _pick(n, (1024, 768, 640, 512, 384, 256, 128))
    sa, sb, so = a.dtype.itemsize, b.dtype.itemsize, jnp.dtype(out_dtype).itemsize
    for tk in [k] + [d for d in (4224, 2816, 2048, 1920, 1408, 1152, 1024, 768, 640, 512, 384, 256, 128) if d < k and k % d == 0]:
        if 2 * (tm * tk * sa + tk * tn * sb) + 2 * tm * tn * so + (tm * tn * 4 if tk < k else 0) <= MM_VMEM_BUDGET:
            break
    nk = k // tk
    a_spec = pl.BlockSpec((tm, tk), lambda i, j, q: (i, q)) if ca == 1 else pl.BlockSpec((tk, tm), lambda i, j, q: (q, i))
    b_spec = pl.BlockSpec((tk, tn), lambda i, j, q: (q, j)) if cb == 0 else pl.BlockSpec((tn, tk), lambda i, j, q: (j, q))

    def body(a_ref, b_ref, o_ref, *scratch):
        d = _contract(a_ref[...], b_ref[...], ca, cb, False)
        if nk == 1:
            o_ref[...] = d.astype(o_ref.dtype)
            return
        acc = scratch[0]
        q = pl.program_id(2)

        @pl.when(q == 0)
        def _():
            acc[...] = d

        @pl.when(jnp.logical_and(q > 0, q < nk - 1))
        def _():
            acc[...] += d

        @pl.when(q == nk - 1)
        def _():
            o_ref[...] = (acc[...] + d).astype(o_ref.dtype)

    return pl.pallas_call(
        body, name=name, grid=(m // tm, n // tn, nk),
        in_specs=[a_spec, b_spec], out_specs=pl.BlockSpec((tm, tn), lambda i, j, q: (i, j)),
        out_shape=jax.ShapeDtypeStruct((m, n), out_dtype),
        scratch_shapes=[pltpu.VMEM((tm, tn), F32)] if nk > 1 else [],
        compiler_params=pltpu.CompilerParams(dimension_semantics=("parallel", "parallel", "arbitrary")),
    )(a, b)


def _row_spec(r, tm):
    if isinstance(r, tuple):
        arr, width, blk = r
        return arr, pl.BlockSpec((tm, width), lambda i, blk=blk: (i, blk))
    return r, pl.BlockSpec((tm, r.shape[1]), lambda i: (i, 0))


def _seg_spec(s, nct):
    return pl.BlockSpec((1, 1, s.shape[2]), lambda i: (jnp.where(i < nct, 0, 1), 0, 0))


def _full_spec(s):
    return pl.BlockSpec(s.shape, lambda i: (0,) * s.ndim)


def rowwise(name, fn, rows, segs, shared, outs, tm, nct):
    arrs, specs = zip(*[_row_spec(r, tm) for r in rows])
    t = arrs[0].shape[0]
    nr, ns = len(rows), len(segs)

    def body(*refs):
        vals = [r[...] for r in refs[:nr]] + [r[0] for r in refs[nr:nr + ns]] + [r[...] for r in refs[nr + ns:nr + ns + len(shared)]]
        res = fn(*vals)
        for o_ref, v in zip(refs[nr + ns + len(shared):], res):
            o_ref[...] = v.astype(o_ref.dtype)

    res = pl.pallas_call(
        body, name=name, grid=(t // tm,),
        in_specs=list(specs) + [_seg_spec(s, nct) for s in segs] + [_full_spec(s) for s in shared],
        out_specs=[pl.BlockSpec((tm, w), lambda i: (i, 0)) for w, _ in outs],
        out_shape=[jax.ShapeDtypeStruct((t, w), d) for w, d in outs],
        compiler_params=pltpu.CompilerParams(dimension_semantics=("parallel",)),
    )(*arrs, *segs, *shared)
    return res


def rowwise_bwd(name, fn, rows, segs, shared, cts, want, tm, nct):
    arrs, specs = zip(*[_row_spec(r, tm) for r in rows])
    t = arrs[0].shape[0]
    groups = [list(c) if isinstance(c, (list, tuple)) else [c] for c in cts]
    cts = [c for g in groups for c in g]
    nr, ns, nsh, nc = len(rows), len(segs), len(shared), len(cts)
    widths = [sp.block_shape[1] for sp in specs]
    wanted = [i for i in range(nr) if want[i]]

    def body(*refs):
        i = pl.program_id(0)
        ins = refs[:nr + ns + nsh]
        ct_refs = refs[nr + ns + nsh:nr + ns + nsh + nc]
        o_refs = refs[nr + ns + nsh + nc:]
        vals = [r[...] for r in ins[:nr]] + [r[0] for r in ins[nr:nr + ns]] + [r[...] for r in ins[nr + ns:]]
        _, vjp = jax.vjp(fn, *vals)
        ct_vals, pos = [], 0
        for g in groups:
            tot = ct_refs[pos][...]
            for c_ref in ct_refs[pos + 1:pos + len(g)]:
                tot = tot + c_ref[...]
            ct_vals.append(tot)
            pos += len(g)
        grads = vjp(tuple(ct_vals))
        for o_ref, idx in zip(o_refs[:len(wanted)], wanted):
            o_ref[...] = grads[idx].astype(o_ref.dtype)
        first = jnp.logical_or(i == 0, i == nct)
        for o_ref, g in zip(o_refs[len(wanted):], grads[nr:]):
            @pl.when(first)
            def _(o_ref=o_ref):
                o_ref[...] = jnp.zeros_like(o_ref)

            o_ref[0] += g

    seg_idx = lambda i: jnp.where(i < nct, 0, 1)
    out_specs = [pl.BlockSpec((tm, widths[idx]), lambda i: (i, 0)) for idx in wanted]
    out_shape = [jax.ShapeDtypeStruct((t, widths[idx]), F32) for idx in wanted]
    for s in segs:
        out_specs.append(pl.BlockSpec((1, 1, s.shape[2]), lambda i: (seg_idx(i), 0, 0)))
        out_shape.append(jax.ShapeDtypeStruct((2, 1, s.shape[2]), F32))
    for s in shared:
        out_specs.append(pl.BlockSpec((1,) + s.shape, lambda i, nd=s.ndim: (seg_idx(i),) + (0,) * nd))
        out_shape.append(jax.ShapeDtypeStruct((2,) + s.shape, F32))
    res = pl.pallas_call(
        body, name=name, grid=(t // tm,),
        in_specs=list(specs) + [_seg_spec(s, nct) for s in segs] + [_full_spec(s) for s in shared]
        + [pl.BlockSpec((tm, c.shape[1]), lambda i: (i, 0)) for c in cts],
        out_specs=out_specs, out_shape=out_shape,
        compiler_params=pltpu.CompilerParams(dimension_semantics=("arbitrary",)),
    )(*arrs, *segs, *shared, *cts)
    nw = len(wanted)
    return res[:nw], res[nw:nw + ns], res[nw + ns:]


def modulate_fn(x, shift, scale):
    return (x * (1.0 + scale) + shift,)


def postnorm_fn(x, a, gate, w, b):
    z = DEEPNORM_ALPHA * x + gate * a
    mu = jnp.mean(z, -1, keepdims=True)
    zc = z - mu
    var = jnp.mean(zc * zc, -1, keepdims=True)
    return (zc * lax.rsqrt(var + EPS) * w + b,)


def ret_prep_fn(q, k, cos, sin):
    qs = [_rope(x, cos, sin) for x in _heads(q)]
    ks = [_rope(x, cos, sin) * QK_SCALE for x in _heads(k)]
    return jnp.concatenate(qs, 1), jnp.concatenate(ks, 1)


def ret_out_fn(of, ob, gate):
    ys = [_rms(o) * _silu(g) for o, g in zip(_heads(of + ob), _heads(gate))]
    return (jnp.concatenate(ys, 1),)


def dn_prep_fn(q, k, ab, a_log, dt_b):
    qs = [_l2n(x) * QK_SCALE for x in _heads(q)]
    ks = [_l2n(x) for x in _heads(k)]
    lane = lax.broadcasted_iota(jnp.int32, ab.shape, 1)
    g = -jnp.exp(a_log) * _softplus(ab + dt_b)
    beta = _sigmoid(ab)
    gb = jnp.where(lane < 2 * DN_HEADS, g, jnp.where(lane < 4 * DN_HEADS, beta, 0.0))
    return jnp.concatenate(qs, 1), jnp.concatenate(ks, 1), gb


def dn_out_fn(of, ob, z, w):
    ys = [_rms(o) * w * _silu(g) for o, g in zip(_heads(of + ob), _heads(z))]
    return (jnp.concatenate(ys, 1),)


def att_prep_fn(q, k, cos, sin, qw, kw):
    qs = [_rope(_rms(x) * qw, cos, sin) for x in _heads(q)]
    ks = [_rope(_rms(x) * kw, cos, sin) for x in _heads(k)]
    return jnp.concatenate(qs, 1), jnp.concatenate(ks, 1)


def swiglu_fwd(u, tm, name):
    t, w2 = u.shape
    w = w2 // 2
    cw = _pick(w, (2816, 1408, 512, 256, 128))
    ncb = w // cw

    def body(g_ref, u_ref, o_ref):
        o_ref[...] = (_silu(g_ref[...].astype(F32)) * u_ref[...].astype(F32)).astype(o_ref.dtype)

    return pl.pallas_call(
        body, name=name, grid=(t // tm, ncb),
        in_specs=[pl.BlockSpec((tm, cw), lambda i, j: (i, j)), pl.BlockSpec((tm, cw), lambda i, j: (i, j + ncb))],
        out_specs=pl.BlockSpec((tm, cw), lambda i, j: (i, j)),
        out_shape=jax.ShapeDtypeStruct((t, w), MXU_DTYPE),
        compiler_params=pltpu.CompilerParams(dimension_semantics=("parallel", "parallel")),
    )(u, u)


def swiglu_bwd(u, dact, tm, name):
    t, w2 = u.shape
    w = w2 // 2
    cw = _pick(w, (2816, 1408, 512, 256, 128))
    ncb = w // cw

    def body(g_ref, u_ref, d_ref, o_ref):
        j = pl.program_id(1)
        g = g_ref[...].astype(F32)
        s = _sigmoid(g)
        d = d_ref[...]
        dg = d * u_ref[...].astype(F32) * (s * (1.0 + g * (1.0 - s)))
        du = d * g * s
        o_ref[...] = jnp.where(j < ncb, dg, du).astype(o_ref.dtype)

    return pl.pallas_call(
        body, name=name, grid=(t // tm, 2 * ncb),
        in_specs=[pl.BlockSpec((tm, cw), lambda i, j: (i, j % ncb)), pl.BlockSpec((tm, cw), lambda i, j: (i, j % ncb + ncb)),
                  pl.BlockSpec((tm, cw), lambda i, j: (i, j % ncb))],
        out_specs=pl.BlockSpec((tm, cw), lambda i, j: (i, j)),
        out_shape=jax.ShapeDtypeStruct((t, w2), MXU_DTYPE),
        compiler_params=pltpu.CompilerParams(dimension_semantics=("parallel", "parallel")),
    )(u, u, dact)


CONV_HALO = 8


def _conv_mask(t, lc, s):
    r = lax.broadcasted_iota(jnp.int32, (t, 1), 0)
    src = r + s
    return jnp.logical_and(jnp.logical_and(src >= 0, src < t), (r < lc) == (src < lc))


def _conv_taps(pad_ref, w_ref, t, lc, flip):
    acc = None
    for k in range(DN_CONV_K):
        s = k - DN_CONV_K // 2
        off = -s if flip else s
        tap = pad_ref[pl.ds(CONV_HALO + off, t), :]
        term = jnp.where(_conv_mask(t, lc, off), tap, 0.0) * w_ref[k:k + 1, :]
        acc = term if acc is None else acc + term
    return acc


def _fill_pad(pad_ref, val, t):
    pad_ref[pl.ds(0, CONV_HALO), :] = jnp.zeros((CONV_HALO, LANES), F32)
    pad_ref[pl.ds(CONV_HALO + t, CONV_HALO), :] = jnp.zeros((CONV_HALO, LANES), F32)
    pad_ref[pl.ds(CONV_HALO, t), :] = val


def conv_fwd(p, col0, width, w, lc, name):
    t = p.shape[0]
    b0 = col0 // LANES

    def body(x_ref, w_ref, o_ref, pad):
        _fill_pad(pad, x_ref[...], t)
        o_ref[...] = _silu(_conv_taps(pad, w_ref, t, lc, False))

    return pl.pallas_call(
        body, name=name, grid=(width // LANES,),
        in_specs=[pl.BlockSpec((t, LANES), lambda j: (0, j + b0)), pl.BlockSpec((DN_CONV_K, LANES), lambda j: (0, j))],
        out_specs=pl.BlockSpec((t, LANES), lambda j: (0, j)),
        out_shape=jax.ShapeDtypeStruct((t, width), F32),
        scratch_shapes=[pltpu.VMEM((t + 2 * CONV_HALO, LANES), F32)],
        compiler_params=pltpu.CompilerParams(dimension_semantics=("parallel",)),
    )(p, w)


def conv_bwd(p, col0, width, w, dout, lc, name):
    t = p.shape[0]
    b0 = col0 // LANES

    def body(x_ref, w_ref, d_ref, dx_ref, dw_ref, pad):
        _fill_pad(pad, x_ref[...], t)
        y = _conv_taps(pad, w_ref, t, lc, False)
        sg = _sigmoid(y)
        dy = d_ref[...] * (sg * (1.0 + y * (1.0 - sg)))
        krow = lax.broadcasted_iota(jnp.int32, (8, LANES), 0)
        dw = jnp.zeros((8, LANES), F32)
        for k in range(DN_CONV_K):
            s = k - DN_CONV_K // 2
            tap = pad[pl.ds(CONV_HALO + s, t), :]
            dw_k = jnp.sum(jnp.where(_conv_mask(t, lc, s), tap, 0.0) * dy, axis=0, keepdims=True)
            dw = dw + jnp.where(krow == k, dw_k, 0.0)
        dw_ref[...] = dw
        _fill_pad(pad, dy, t)
        dx_ref[...] = _conv_taps(pad, w_ref, t, lc, True)

    return pl.pallas_call(
        body, name=name, grid=(width // LANES,),
        in_specs=[pl.BlockSpec((t, LANES), lambda j: (0, j + b0)), pl.BlockSpec((DN_CONV_K, LANES), lambda j: (0, j)),
                  pl.BlockSpec((t, LANES), lambda j: (0, j))],
        out_specs=[pl.BlockSpec((t, LANES), lambda j: (0, j)), pl.BlockSpec((8, LANES), lambda j: (0, j))],
        out_shape=[jax.ShapeDtypeStruct((t, width), F32), jax.ShapeDtypeStruct((8, width), F32)],
        scratch_shapes=[pltpu.VMEM((t + 2 * CONV_HALO, LANES), F32)],
        compiler_params=pltpu.CompilerParams(dimension_semantics=("parallel",)),
    )(p, w, dout)


def ret_chunk(q, k, v, s, logit_slab, h):
    c = q.shape[0]
    lg = -_softplus(-_lane_pick(logit_slab, h))
    i = lax.broadcasted_iota(jnp.int32, (c, c), 0)
    j = lax.broadcasted_iota(jnp.int32, (c, c), 1)
    rel = (i - j).astype(F32)
    decay = jnp.where(i >= j, jnp.exp(jnp.maximum(rel, 0.0) * lg), 0.0)
    pos = lax.broadcasted_iota(jnp.int32, (c, 1), 0).astype(F32)
    q_decay = jnp.exp((pos + 1.0) * lg)
    k_decay = jnp.exp((c - 1.0 - pos) * lg)
    intra = mdot(q, k, 1, 1) * decay
    o = mdot(intra, v, 1, 0) + mdot(q * q_decay, s, 1, 0)
    s_new = s * jnp.exp(c * lg) + mdot(k * k_decay, v, 0, 0)
    return o, s_new


def dn_chunk(q, k, v, s, gb, h):
    c = q.shape[0]
    g = _lane_pick(gb, h[0])
    beta = _lane_pick(gb, h[1])
    i = lax.broadcasted_iota(jnp.int32, (c, c), 0)
    j = lax.broadcasted_iota(jnp.int32, (c, c), 1)
    tri = i >= j
    gc = _lane_pick(mdot(tri.astype(F32), gb, 1, 0, True), h[0])
    gc_row = jnp.sum(jnp.where(i == j, gc, 0.0), axis=0, keepdims=True)
    decay = jnp.where(tri, jnp.exp(jnp.where(tri, gc - gc_row, 0.0)), 0.0)
    kb = k * beta
    vb = v * beta
    a = jnp.where(i > j, mdot(kb, k, 1, 1) * decay, 0.0)
    t = tri_inv(a)
    e = jnp.exp(gc)
    g_last = jnp.sum(g, axis=0, keepdims=True)
    w_val = mdot(t, vb, 1, 0)
    k_cum = mdot(t, kb * e, 1, 0)
    qk = mdot(q, k, 1, 1) * decay
    v_new = w_val - mdot(k_cum, s, 1, 0)
    o = mdot(q * e, s, 1, 0) + mdot(qk, v_new, 1, 0)
    s_new = s * jnp.exp(g_last) + mdot(k * jnp.exp(g_last - gc), v_new, 0, 0)
    return o, s_new


def scan_fwd(chunk_fn, c, q, k, v, slab, hsel, name):
    t, wid = q.shape
    n, nheads = t // c, wid // HEAD_DIM
    per_token = slab.shape[0] != 1
    row = pl.BlockSpec((c, wid), lambda i: (i, 0))
    slab_spec = pl.BlockSpec((c, LANES), lambda i: (i, 0)) if per_token else _full_spec(slab)
    st_spec = pl.BlockSpec((1, nheads, HEAD_DIM, HEAD_DIM), lambda i: (i, 0, 0, 0))

    def body(q_ref, k_ref, v_ref, slab_ref, o_ref, st_ref, s_scr):
        @pl.when(pl.program_id(0) == 0)
        def _():
            s_scr[...] = jnp.zeros_like(s_scr)

        sb = slab_ref[...]
        for h in range(nheads):
            sl = slice(h * HEAD_DIM, (h + 1) * HEAD_DIM)
            s_h = s_scr[h]
            st_ref[0, h] = s_h
            o, s_new = chunk_fn(q_ref[:, sl], k_ref[:, sl], v_ref[:, sl], s_h, sb, hsel(h))
            o_ref[:, sl] = o
            s_scr[h] = s_new

    return pl.pallas_call(
        body, name=name, grid=(n,),
        in_specs=[row, row, row, slab_spec], out_specs=[row, st_spec],
        out_shape=[jax.ShapeDtypeStruct((t, wid), F32), jax.ShapeDtypeStruct((n, nheads, HEAD_DIM, HEAD_DIM), F32)],
        scratch_shapes=[pltpu.VMEM((nheads, HEAD_DIM, HEAD_DIM), F32)],
        compiler_params=pltpu.CompilerParams(dimension_semantics=("arbitrary",)),
    )(q, k, v, slab)


def scan_bwd(chunk_fn, c, q, k, v, slab, states, do, hsel, name):
    t, wid = q.shape
    n, nheads = t // c, wid // HEAD_DIM
    per_token = slab.shape[0] != 1
    row = pl.BlockSpec((c, wid), lambda i: (n - 1 - i, 0))
    slab_spec = pl.BlockSpec((c, LANES), lambda i: (n - 1 - i, 0)) if per_token else _full_spec(slab)
    st_spec = pl.BlockSpec((1, nheads, HEAD_DIM, HEAD_DIM), lambda i: (n - 1 - i, 0, 0, 0))

    def body(q_ref, k_ref, v_ref, slab_ref, st_ref, do_ref, dq_ref, dk_ref, dv_ref, dslab_ref, ds_scr):
        @pl.when(pl.program_id(0) == 0)
        def _():
            ds_scr[...] = jnp.zeros_like(ds_scr)
            if not per_token:
                dslab_ref[...] = jnp.zeros_like(dslab_ref)

        sb = slab_ref[...]
        dslab = None
        for h in range(nheads):
            sl = slice(h * HEAD_DIM, (h + 1) * HEAD_DIM)
            _, vjp = jax.vjp(lambda a, b, cc, s, z, h=h: chunk_fn(a, b, cc, s, z, hsel(h)),
                             q_ref[:, sl], k_ref[:, sl], v_ref[:, sl], st_ref[0, h], sb)
            dq, dk, dv, ds, dz = vjp((do_ref[:, sl], ds_scr[h]))
            dq_ref[:, sl] = dq
            dk_ref[:, sl] = dk
            dv_ref[:, sl] = dv
            ds_scr[h] = ds
            dslab = dz if dslab is None else dslab + dz
        if per_token:
            dslab_ref[...] = dslab
        else:
            dslab_ref[...] += dslab

    return pl.pallas_call(
        body, name=name, grid=(n,),
        in_specs=[row, row, row, slab_spec, st_spec, row], out_specs=[row, row, row, slab_spec],
        out_shape=[jax.ShapeDtypeStruct((t, wid), F32)] * 3 + [jax.ShapeDtypeStruct(slab.shape, F32)],
        scratch_shapes=[pltpu.VMEM((nheads, HEAD_DIM, HEAD_DIM), F32)],
        compiler_params=pltpu.CompilerParams(dimension_semantics=("arbitrary",)),
    )(q, k, v, slab, states, do)


def ret_chunk(q, k, v, s, logit_slab, sel):
    lane, rev = sel
    c = q.shape[0]
    lg = -_softplus(-_lane_pick(logit_slab, lane))
    i = lax.broadcasted_iota(jnp.int32, (c, c), 0)
    j = lax.broadcasted_iota(jnp.int32, (c, c), 1)
    rel = ((j - i) if rev else (i - j)).astype(F32)
    decay = jnp.where(rel >= 0, jnp.exp(jnp.maximum(rel, 0.0) * lg), 0.0)
    pos = lax.broadcasted_iota(jnp.int32, (c, 1), 0).astype(F32)
    pos = (c - 1.0 - pos) if rev else pos
    q_decay = jnp.exp((pos + 1.0) * lg)
    k_decay = jnp.exp((c - 1.0 - pos) * lg)
    intra = mdot(q, k, 1, 1) * decay
    o = mdot(intra, v, 1, 0) + mdot(q * q_decay, s, 1, 0)
    s_new = s * jnp.exp(c * lg) + mdot(k * k_decay, v, 0, 0)
    return o, s_new


def dn_chunk(q, k, v, s, gb, sel):
    g_lane, b_lane, rev = sel
    c = q.shape[0]
    g = _lane_pick(gb, g_lane)
    beta = _lane_pick(gb, b_lane)
    i = lax.broadcasted_iota(jnp.int32, (c, c), 0)
    j = lax.broadcasted_iota(jnp.int32, (c, c), 1)
    tri = (i <= j) if rev else (i >= j)
    strict = (i < j) if rev else (i > j)
    gc = _lane_pick(mdot(tri.astype(F32), gb, 1, 0, True), g_lane)
    gc_row = jnp.sum(jnp.where(i == j, gc, 0.0), axis=0, keepdims=True)
    decay = jnp.where(tri, jnp.exp(jnp.where(tri, gc - gc_row, 0.0)), 0.0)
    kb = k * beta
    vb = v * beta
    a = jnp.where(strict, mdot(kb, k, 1, 1) * decay, 0.0)
    t = tri_inv(a)
    e = jnp.exp(gc)
    g_last = jnp.sum(g, axis=0, keepdims=True)
    w_val = mdot(t, vb, 1, 0)
    k_cum = mdot(t, kb * e, 1, 0)
    qk = mdot(q, k, 1, 1) * decay
    v_new = w_val - mdot(k_cum, s, 1, 0)
    o = mdot(q * e, s, 1, 0) + mdot(qk, v_new, 1, 0)
    s_new = s * jnp.exp(g_last) + mdot(k * jnp.exp(g_last - gc), v_new, 0, 0)
    return o, s_new


def _scan_maps(n, ncc):
    return (lambda s: s), (lambda s: jnp.where(s < ncc, ncc - 1 - s, n - 1 - (s - ncc)))


def scan_fwd(chunk_fn, c, q, k, v, slab, sels, lc, name):
    t, wid = q.shape
    n, nheads, ncc = t // c, wid // HEAD_DIM, lc // c
    per_token = slab.shape[0] != 1
    maps = _scan_maps(n, ncc)
    rows = [pl.BlockSpec((c, wid), lambda s, m=m: (m(s), 0)) for m in maps]
    slabs = [pl.BlockSpec((c, LANES), lambda s, m=m: (m(s), 0)) if per_token else _full_spec(slab) for m in maps]
    st_spec = pl.BlockSpec((1, 2 * nheads, HEAD_DIM, HEAD_DIM), lambda s: (s, 0, 0, 0))

    def body(qf, kf, vf, sbf, qb, kb, vb, sbb, of_ref, ob_ref, st_ref, s_scr):
        @pl.when(pl.program_id(0) == 0)
        def _():
            s_scr[...] = jnp.zeros_like(s_scr)

        work = []
        for d, (qr, kr, vr, sr) in enumerate(((qf, kf, vf, sbf), (qb, kb, vb, sbb))):
            sb = sr[...]
            for h in range(nheads):
                sl = slice(h * HEAD_DIM, (h + 1) * HEAD_DIM)
                work.append((d, h, sl, qr[:, sl], kr[:, sl], vr[:, sl], s_scr[d * nheads + h], sb))
        res = [chunk_fn(qq, kk, vv, ss, sb, sels[d][h]) for d, h, sl, qq, kk, vv, ss, sb in work]
        for (d, h, sl, _, _, _, ss, _), (o, s_new) in zip(work, res):
            st_ref[0, d * nheads + h] = ss
            (of_ref, ob_ref)[d][:, sl] = o
            s_scr[d * nheads + h] = s_new

    return pl.pallas_call(
        body, name=name, grid=(n,),
        in_specs=[rows[0]] * 3 + [slabs[0]] + [rows[1]] * 3 + [slabs[1]], out_specs=[rows[0], rows[1], st_spec],
        out_shape=[jax.ShapeDtypeStruct((t, wid), F32)] * 2 + [jax.ShapeDtypeStruct((n, 2 * nheads, HEAD_DIM, HEAD_DIM), F32)],
        scratch_shapes=[pltpu.VMEM((2 * nheads, HEAD_DIM, HEAD_DIM), F32)],
        compiler_params=pltpu.CompilerParams(dimension_semantics=("arbitrary",)),
    )(q, k, v, slab, q, k, v, slab)


def scan_bwd(chunk_fn, c, q, k, v, slab, states, do, sels, lc, name):
    t, wid = q.shape
    n, nheads, ncc = t // c, wid // HEAD_DIM, lc // c
    per_token = slab.shape[0] != 1
    maps = [lambda s, m=m: m(n - 1 - s) for m in _scan_maps(n, ncc)]
    rows = [pl.BlockSpec((c, wid), lambda s, m=m: (m(s), 0)) for m in maps]
    slabs = [pl.BlockSpec((c, LANES), lambda s, m=m: (m(s), 0)) if per_token else _full_spec(slab) for m in maps]
    st_spec = pl.BlockSpec((1, 2 * nheads, HEAD_DIM, HEAD_DIM), lambda s: (n - 1 - s, 0, 0, 0))
    n_slab_out = 2 if per_token else 1

    def body(*refs):
        ins = (refs[0:4] + (refs[9],), refs[4:8] + (refs[10],))
        st_ref = refs[8]
        outs = refs[11:11 + 6 + n_slab_out]
        ds_scr = refs[-1]

        @pl.when(pl.program_id(0) == 0)
        def _():
            ds_scr[...] = jnp.zeros_like(ds_scr)
            if not per_token:
                outs[6][...] = jnp.zeros_like(outs[6])

        work = []
        for d, (qr, kr, vr, sr, dor) in enumerate(ins):
            sb = sr[...]
            for h in range(nheads):
                sl = slice(h * HEAD_DIM, (h + 1) * HEAD_DIM)
                idx = d * nheads + h
                work.append((d, h, sl, idx, (qr[:, sl], kr[:, sl], vr[:, sl], st_ref[0, idx], sb), (dor[:, sl], ds_scr[idx])))
        res = []
        for d, h, sl, idx, prim, cot in work:
            _, vjp = jax.vjp(lambda a, b, cc, s, z, d=d, h=h: chunk_fn(a, b, cc, s, z, sels[d][h]), *prim)
            res.append(vjp(cot))
        dslab = [None, None]
        for (d, h, sl, idx, _, _), (dq, dk, dv, ds, dz) in zip(work, res):
            outs[3 * d][:, sl] = dq
            outs[3 * d + 1][:, sl] = dk
            outs[3 * d + 2][:, sl] = dv
            ds_scr[idx] = ds
            dslab[d] = dz if dslab[d] is None else dslab[d] + dz
        if per_token:
            outs[6][...] = dslab[0]
            outs[7][...] = dslab[1]
        else:
            outs[6][...] += dslab[0] + dslab[1]

    return pl.pallas_call(
        body, name=name, grid=(n,),
        in_specs=[rows[0]] * 3 + [slabs[0]] + [rows[1]] * 3 + [slabs[1]] + [st_spec, rows[0], rows[1]],
        out_specs=[rows[0]] * 3 + [rows[1]] * 3 + ([slabs[0], slabs[1]] if per_token else [slabs[0]]),
        out_shape=[jax.ShapeDtypeStruct((t, wid), F32)] * 6 + [jax.ShapeDtypeStruct(slab.shape, F32)] * n_slab_out,
        scratch_shapes=[pltpu.VMEM((2 * nheads, HEAD_DIM, HEAD_DIM), F32)],
        compiler_params=pltpu.CompilerParams(dimension_semantics=("arbitrary",)),
    )(q, k, v, slab, q, k, v, slab, states, do, do)


@jax.custom_vjp
def tri_inv_all(mats):
    n = mats[0].shape[0]
    r = lax.broadcasted_iota(jnp.int32, (n, n), 0)
    c = lax.broadcasted_iota(jnp.int32, (n, n), 1)
    eye = (r == c).astype(F32)
    ps = [-a for a in mats]
    ts = [eye + p for p in ps]
    k = 2
    while k < n:
        ps = [_contract(p, p, 1, 0, True) for p in ps]
        ts = [t + _contract(t, p, 1, 0, True) for t, p in zip(ts, ps)]
        k *= 2
    return tuple(ts)


def _tri_inv_all_fwd(mats):
    ts = tri_inv_all(mats)
    return ts, ts


def _tri_inv_all_bwd(ts, gs):
    left = [_contract(t, g, 0, 0, True) for t, g in zip(ts, gs)]
    return (tuple(-_contract(l, t, 1, 1, True) for l, t in zip(left, ts)),)


tri_inv_all.defvjp(_tri_inv_all_fwd, _tri_inv_all_bwd)


def ret_chunk(qs, ks, vs, ss, slabs, sels):
    c = qs[0].shape[0]
    i = lax.broadcasted_iota(jnp.int32, (c, c), 0)
    j = lax.broadcasted_iota(jnp.int32, (c, c), 1)
    pos0 = lax.broadcasted_iota(jnp.int32, (c, 1), 0).astype(F32)
    lgs = [-_softplus(-_lane_pick(slabs[d], lane)) for d, lane, _ in sels]
    rels = [((j - i) if rev else (i - j)).astype(F32) for _, _, rev in sels]
    decays = [jnp.where(rel >= 0, jnp.exp(jnp.maximum(rel, 0.0) * lg), 0.0) for rel, lg in zip(rels, lgs)]
    poss = [(c - 1.0 - pos0) if rev else pos0 for _, _, rev in sels]
    intra = [mdot(q, k, 1, 1) * dec for q, k, dec in zip(qs, ks, decays)]
    kv = [mdot(k * jnp.exp((c - 1.0 - pos) * lg), v, 0, 0) for k, v, pos, lg in zip(ks, vs, poss, lgs)]
    o1 = [mdot(a, v, 1, 0) for a, v in zip(intra, vs)]
    o2 = [mdot(q * jnp.exp((pos + 1.0) * lg), s, 1, 0) for q, s, pos, lg in zip(qs, ss, poss, lgs)]
    outs = [a + b for a, b in zip(o1, o2)]
    s_new = [s * jnp.exp(c * lg) + u for s, lg, u in zip(ss, lgs, kv)]
    return outs, s_new


def dn_chunk(qs, ks, vs, ss, slabs, sels):
    c = qs[0].shape[0]
    i = lax.broadcasted_iota(jnp.int32, (c, c), 0)
    j = lax.broadcasted_iota(jnp.int32, (c, c), 1)
    tris = [(i <= j) if rev else (i >= j) for _, _, _, rev in sels]
    stricts = [(i < j) if rev else (i > j) for _, _, _, rev in sels]
    gs = [_lane_pick(slabs[d], gl) for d, gl, _, _ in sels]
    betas = [_lane_pick(slabs[d], bl) for d, _, bl, _ in sels]
    sums = [mdot(tri.astype(F32), slabs[sel[0]], 1, 0, True) for tri, sel in zip(tris, sels)]
    gcs = [_lane_pick(cs, sel[1]) for cs, sel in zip(sums, sels)]
    gc_rows = [jnp.sum(jnp.where(i == j, gc, 0.0), axis=0, keepdims=True) for gc in gcs]
    decays = [jnp.where(tri, jnp.exp(jnp.where(tri, gc - gr, 0.0)), 0.0) for tri, gc, gr in zip(tris, gcs, gc_rows)]
    kbs = [k * b for k, b in zip(ks, betas)]
    vbs = [v * b for v, b in zip(vs, betas)]
    kk = [mdot(kb, k, 1, 1) for kb, k in zip(kbs, ks)]
    qk = [mdot(q, k, 1, 1) * dec for q, k, dec in zip(qs, ks, decays)]
    ts = tri_inv_all(tuple(jnp.where(st, a * dec, 0.0) for st, a, dec in zip(stricts, kk, decays)))
    es = [jnp.exp(gc) for gc in gcs]
    g_last = [jnp.sum(g, axis=0, keepdims=True) for g in gs]
    w_val = [mdot(t, vb, 1, 0) for t, vb in zip(ts, vbs)]
    k_cum = [mdot(t, kb * e, 1, 0) for t, kb, e in zip(ts, kbs, es)]
    ks_s = [mdot(kc, s, 1, 0) for kc, s in zip(k_cum, ss)]
    qs_s = [mdot(q * e, s, 1, 0) for q, e, s in zip(qs, es, ss)]
    v_new = [w - u for w, u in zip(w_val, ks_s)]
    o2 = [mdot(a, vn, 1, 0) for a, vn in zip(qk, v_new)]
    upd = [mdot(k * jnp.exp(gl - gc), vn, 0, 0) for k, gl, gc, vn in zip(ks, g_last, gcs, v_new)]
    outs = [a + b for a, b in zip(qs_s, o2)]
    s_new = [s * jnp.exp(gl) + u for s, gl, u in zip(ss, g_last, upd)]
    return outs, s_new


def scan_fwd(chunk_fn, c, q, k, v, slab, sels, lc, name):
    t, wid = q.shape
    n, nheads, ncc = t // c, wid // HEAD_DIM, lc // c
    per_token = slab.shape[0] != 1
    maps = _scan_maps(n, ncc)
    rows = [pl.BlockSpec((c, wid), lambda s, m=m: (m(s), 0)) for m in maps]
    slabs = [pl.BlockSpec((c, LANES), lambda s, m=m: (m(s), 0)) if per_token else _full_spec(slab) for m in maps]
    st_spec = pl.BlockSpec((1, 2 * nheads, HEAD_DIM, HEAD_DIM), lambda s: (s, 0, 0, 0))
    items = [(d, h) for d in range(2) for h in range(nheads)]
    flat_sels = [(d,) + tuple(sels[d][h]) for d, h in items]

    def body(qf, kf, vf, sbf, qb, kb, vb, sbb, of_ref, ob_ref, st_ref, s_scr):
        @pl.when(pl.program_id(0) == 0)
        def _():
            s_scr[...] = jnp.zeros_like(s_scr)

        qr, kr, vr = (qf, qb), (kf, kb), (vf, vb)
        sl = lambda h: slice(h * HEAD_DIM, (h + 1) * HEAD_DIM)
        ss = [s_scr[n_] for n_ in range(len(items))]
        outs, s_new = chunk_fn([qr[d][:, sl(h)] for d, h in items], [kr[d][:, sl(h)] for d, h in items],
                               [vr[d][:, sl(h)] for d, h in items], ss, [sbf[...], sbb[...]], flat_sels)
        for n_, (d, h) in enumerate(items):
            st_ref[0, n_] = ss[n_]
            (of_ref, ob_ref)[d][:, sl(h)] = outs[n_]
            s_scr[n_] = s_new[n_]

    return pl.pallas_call(
        body, name=name, grid=(n,),
        in_specs=[rows[0]] * 3 + [slabs[0]] + [rows[1]] * 3 + [slabs[1]], out_specs=[rows[0], rows[1], st_spec],
        out_shape=[jax.ShapeDtypeStruct((t, wid), F32)] * 2 + [jax.ShapeDtypeStruct((n, 2 * nheads, HEAD_DIM, HEAD_DIM), F32)],
        scratch_shapes=[pltpu.VMEM((2 * nheads, HEAD_DIM, HEAD_DIM), F32)],
        compiler_params=pltpu.CompilerParams(dimension_semantics=("arbitrary",)),
    )(q, k, v, slab, q, k, v, slab)


def scan_bwd(chunk_fn, c, q, k, v, slab, states, do, sels, lc, name):
    t, wid = q.shape
    n, nheads, ncc = t // c, wid // HEAD_DIM, lc // c
    per_token = slab.shape[0] != 1
    maps = [lambda s, m=m: m(n - 1 - s) for m in _scan_maps(n, ncc)]
    rows = [pl.BlockSpec((c, wid), lambda s, m=m: (m(s), 0)) for m in maps]
    slabs = [pl.BlockSpec((c, LANES), lambda s, m=m: (m(s), 0)) if per_token else _full_spec(slab) for m in maps]
    st_spec = pl.BlockSpec((1, 2 * nheads, HEAD_DIM, HEAD_DIM), lambda s: (n - 1 - s, 0, 0, 0))
    n_slab_out = 2 if per_token else 1
    items = [(d, h) for d in range(2) for h in range(nheads)]
    flat_sels = [(d,) + tuple(sels[d][h]) for d, h in items]

    def body(*refs):
        qr, kr, vr, sr = (refs[0], refs[4]), (refs[1], refs[5]), (refs[2], refs[6]), (refs[3], refs[7])
        st_ref, dor = refs[8], (refs[9], refs[10])
        outs = refs[11:11 + 6 + n_slab_out]
        ds_scr = refs[-1]

        @pl.when(pl.program_id(0) == 0)
        def _():
            ds_scr[...] = jnp.zeros_like(ds_scr)
            if not per_token:
                outs[6][...] = jnp.zeros_like(outs[6])

        sl = lambda h: slice(h * HEAD_DIM, (h + 1) * HEAD_DIM)
        prim = ([qr[d][:, sl(h)] for d, h in items], [kr[d][:, sl(h)] for d, h in items],
                [vr[d][:, sl(h)] for d, h in items], [st_ref[0, n_] for n_ in range(len(items))], [sr[0][...], sr[1][...]])
        cot = ([dor[d][:, sl(h)] for d, h in items], [ds_scr[n_] for n_ in range(len(items))])
        _, vjp = jax.vjp(lambda *a: chunk_fn(*a, flat_sels), *prim)
        dqs, dks, dvs, dss, dslabs = vjp(cot)
        for n_, (d, h) in enumerate(items):
            outs[3 * d][:, sl(h)] = dqs[n_]
            outs[3 * d + 1][:, sl(h)] = dks[n_]
            outs[3 * d + 2][:, sl(h)] = dvs[n_]
            ds_scr[n_] = dss[n_]
        if per_token:
            outs[6][...] = dslabs[0]
            outs[7][...] = dslabs[1]
        else:
            outs[6][...] += dslabs[0] + dslabs[1]

    return pl.pallas_call(
        body, name=name, grid=(n,),
        in_specs=[rows[0]] * 3 + [slabs[0]] + [rows[1]] * 3 + [slabs[1]] + [st_spec, rows[0], rows[1]],
        out_specs=[rows[0]] * 3 + [rows[1]] * 3 + ([slabs[0], slabs[1]] if per_token else [slabs[0]]),
        out_shape=[jax.ShapeDtypeStruct((t, wid), F32)] * 6 + [jax.ShapeDtypeStruct(slab.shape, F32)] * n_slab_out,
        scratch_shapes=[pltpu.VMEM((2 * nheads, HEAD_DIM, HEAD_DIM), F32)],
        compiler_params=pltpu.CompilerParams(dimension_semantics=("arbitrary",)),
    )(q, k, v, slab, q, k, v, slab, states, do, do)


def _att_tiles(t, lc):
    return _pick(lc, (256, 128)), _pick(t, (768, 384, 256, 128))


def _att_probs(q, k, lse, is_ctx, j, tk, lc):
    s = _contract(q, k, 1, 1, False) * QK_SCALE
    kidx = j * tk + lax.broadcasted_iota(jnp.int32, (1, tk), 1)
    masked = jnp.logical_and(is_ctx, kidx >= lc)
    return s, masked, (None if lse is None else jnp.where(masked, 0.0, jnp.exp(s - lse)))


def attn_fwd(qn, kn, v, lc, name):
    t = qn.shape[0]
    nh, nkv = qn.shape[1] // HEAD_DIM, kn.shape[1] // HEAD_DIM
    grp = nh // nkv
    tq, tk = _att_tiles(t, lc)
    nq, nk, nctq = t // tq, t // tk, lc // tq

    def body(q_ref, k_ref, v_ref, o_ref, lse_ref, m_scr, l_scr, acc):
        i, j = pl.program_id(1), pl.program_id(2)

        @pl.when(j == 0)
        def _():
            m_scr[...] = jnp.full_like(m_scr, NEG_BIG)
            l_scr[...] = jnp.zeros_like(l_scr)
            acc[...] = jnp.zeros_like(acc)

        is_ctx = i < nctq

        @pl.when(jnp.logical_or(jnp.logical_not(is_ctx), j * tk < lc))
        def _():
            s, masked, _ = _att_probs(q_ref[...], k_ref[...], None, is_ctx, j, tk, lc)
            s = jnp.where(masked, NEG_BIG, s)
            m_old = m_scr[...]
            m_new = jnp.maximum(m_old, jnp.max(s, axis=1, keepdims=True))
            alpha = jnp.exp(m_old - m_new)
            p = jnp.where(masked, 0.0, jnp.exp(s - m_new))
            l_scr[...] = alpha * l_scr[...] + jnp.sum(p, axis=1, keepdims=True)
            acc[...] = alpha * acc[...] + _contract(p, v_ref[...], 1, 0, False)
            m_scr[...] = m_new

        @pl.when(j == nk - 1)
        def _():
            o_ref[...] = acc[...] / l_scr[...]
            lse_ref[0] = m_scr[...] + jnp.log(l_scr[...])

    return pl.pallas_call(
        body, name=name, grid=(nh, nq, nk),
        in_specs=[pl.BlockSpec((tq, HEAD_DIM), lambda h, i, j: (i, h)),
                  pl.BlockSpec((tk, HEAD_DIM), lambda h, i, j: (j, h // grp)),
                  pl.BlockSpec((tk, HEAD_DIM), lambda h, i, j: (j, h // grp))],
        out_specs=[pl.BlockSpec((tq, HEAD_DIM), lambda h, i, j: (i, h)),
                   pl.BlockSpec((1, tq, 1), lambda h, i, j: (h, i, 0))],
        out_shape=[jax.ShapeDtypeStruct((t, nh * HEAD_DIM), F32), jax.ShapeDtypeStruct((nh, t, 1), F32)],
        scratch_shapes=[pltpu.VMEM((tq, 1), F32), pltpu.VMEM((tq, 1), F32), pltpu.VMEM((tq, HEAD_DIM), F32)],
        compiler_params=pltpu.CompilerParams(dimension_semantics=("parallel", "parallel", "arbitrary")),
    )(qn, kn, v)


def attn_bwd_dq(qn, kn, v, o, do, lse, lc, name):
    t = qn.shape[0]
    nh, nkv = qn.shape[1] // HEAD_DIM, kn.shape[1] // HEAD_DIM
    grp = nh // nkv
    tq, tk = _att_tiles(t, lc)
    nq, nk, nctq = t // tq, t // tk, lc // tq

    def body(q_ref, k_ref, v_ref, o_ref, do_ref, lse_ref, dq_ref, dl_ref, acc, dl_scr):
        i, j = pl.program_id(1), pl.program_id(2)

        @pl.when(j == 0)
        def _():
            acc[...] = jnp.zeros_like(acc)
            dl_scr[...] = jnp.sum(do_ref[...] * o_ref[...], axis=1, keepdims=True)

        is_ctx = i < nctq

        @pl.when(jnp.logical_or(jnp.logical_not(is_ctx), j * tk < lc))
        def _():
            _, _, p = _att_probs(q_ref[...], k_ref[...], lse_ref[0], is_ctx, j, tk, lc)
            dp = _contract(do_ref[...], v_ref[...], 1, 1, False)
            ds = p * (dp - dl_scr[...]) * QK_SCALE
            acc[...] += _contract(ds, k_ref[...], 1, 0, False)

        @pl.when(j == nk - 1)
        def _():
            dq_ref[...] = acc[...]
            dl_ref[0] = dl_scr[...]

    qspec = pl.BlockSpec((tq, HEAD_DIM), lambda h, i, j: (i, h))
    kspec = pl.BlockSpec((tk, HEAD_DIM), lambda h, i, j: (j, h // grp))
    rspec = pl.BlockSpec((1, tq, 1), lambda h, i, j: (h, i, 0))
    return pl.pallas_call(
        body, name=name, grid=(nh, nq, nk),
        in_specs=[qspec, kspec, kspec, qspec, qspec, rspec],
        out_specs=[qspec, rspec],
        out_shape=[jax.ShapeDtypeStruct((t, nh * HEAD_DIM), F32), jax.ShapeDtypeStruct((nh, t, 1), F32)],
        scratch_shapes=[pltpu.VMEM((tq, HEAD_DIM), F32), pltpu.VMEM((tq, 1), F32)],
        compiler_params=pltpu.CompilerParams(dimension_semantics=("parallel", "parallel", "arbitrary")),
    )(qn, kn, v, o, do, lse)


def attn_bwd_dkv(qn, kn, v, do, lse, delta, lc, name):
    t = qn.shape[0]
    nh, nkv = qn.shape[1] // HEAD_DIM, kn.shape[1] // HEAD_DIM
    grp = nh // nkv
    tq, tk = _att_tiles(t, lc)
    nq, nk, nctq = t // tq, t // tk, lc // tq
    nr = grp * nq

    def body(q_ref, k_ref, v_ref, do_ref, lse_ref, dl_ref, dk_ref, dv_ref, dk_acc, dv_acc):
        j, r = pl.program_id(1), pl.program_id(2)

        @pl.when(r == 0)
        def _():
            dk_acc[...] = jnp.zeros_like(dk_acc)
            dv_acc[...] = jnp.zeros_like(dv_acc)

        is_ctx = (r % nq) < nctq

        @pl.when(jnp.logical_or(jnp.logical_not(is_ctx), j * tk < lc))
        def _():
            _, _, p = _att_probs(q_ref[...], k_ref[...], lse_ref[0], is_ctx, j, tk, lc)
            dv_acc[...] += _contract(p, do_ref[...], 0, 0, False)
            dp = _contract(do_ref[...], v_ref[...], 1, 1, False)
            ds = p * (dp - dl_ref[0]) * QK_SCALE
            dk_acc[...] += _contract(ds, q_ref[...], 0, 0, False)

        @pl.when(r == nr - 1)
        def _():
            dk_ref[...] = dk_acc[...]
            dv_ref[...] = dv_acc[...]

    qspec = pl.BlockSpec((tq, HEAD_DIM), lambda g, j, r: (r % nq, g * grp + r // nq))
    kspec = pl.BlockSpec((tk, HEAD_DIM), lambda g, j, r: (j, g))
    rspec = pl.BlockSpec((1, tq, 1), lambda g, j, r: (g * grp + r // nq, r % nq, 0))
    return pl.pallas_call(
        body, name=name, grid=(nkv, nk, nr),
        in_specs=[qspec, kspec, kspec, qspec, rspec, rspec],
        out_specs=[kspec, kspec],
        out_shape=[jax.ShapeDtypeStruct((t, nkv * HEAD_DIM), F32), jax.ShapeDtypeStruct((t, nkv * HEAD_DIM), F32)],
        scratch_shapes=[pltpu.VMEM((tk, HEAD_DIM), F32), pltpu.VMEM((tk, HEAD_DIM), F32)],
        compiler_params=pltpu.CompilerParams(dimension_semantics=("parallel", "parallel", "arbitrary")),
    )(qn, kn, v, do, lse, delta)


ATT_FWD_TK = (2816, 1408, 768, 384, 256, 128)
ATT_DQ_TK = (1408, 768, 384, 256, 128)
LOG2E = 1.4426950408889634
LN2 = 0.6931471805599453


def _att_grid(qn, kn, lc):
    t = qn.shape[0]
    nkv = kn.shape[1] // HEAD_DIM
    grp = qn.shape[1] // HEAD_DIM // nkv
    tq, tk = _att_tiles(t, lc)
    return t, nkv, grp, tq, tk, t // tq, t // tk, lc // tq


def _att_paths(i, j, nctq, tk, lc, step):
    is_ctx = i < nctq

    @pl.when(jnp.logical_and(is_ctx, j * tk < lc))
    def _():
        kidx = j * tk + lax.broadcasted_iota(jnp.int32, (1, tk), 1)
        step(kidx >= lc)

    @pl.when(jnp.logical_not(is_ctx))
    def _():
        step(None)


def _att_scores2(q, k, hidden):
    s = _contract(q, k, 1, 1, False)
    return s if hidden is None else jnp.where(hidden, NEG_BIG, s)


def attn_fwd(qn, kn, v, lc, name):
    t, nkv, grp, tq, tk, nq, nk, nctq = _att_grid(qn, kn, lc)
    tk = _pick(t, ATT_FWD_TK)
    nk = t // tk

    def body(q_ref, k_ref, v_ref, o_ref, lse_ref, m_scr, acc):
        i, j = pl.program_id(1), pl.program_id(2)

        @pl.when(j == 0)
        def _():
            m_scr[...] = jnp.full_like(m_scr, NEG_BIG)
            acc[...] = jnp.zeros_like(acc)

        def step(hidden):
            k = k_ref[...]
            vext = jnp.concatenate([v_ref[...], jnp.ones((tk, HEAD_DIM), v_ref.dtype)], 1)
            qs = [q_ref[:, h * HEAD_DIM:(h + 1) * HEAD_DIM] for h in range(grp)]
            ms = [m_scr[h] for h in range(grp)]
            accs = [acc[h] for h in range(grp)]
            ss = [_att_scores2(q, k, hidden) for q in qs]
            m_new = [jnp.maximum(m, jnp.max(s, axis=1, keepdims=True)) for m, s in zip(ms, ss)]
            alpha = [jnp.exp2(m - mn) for m, mn in zip(ms, m_new)]
            ps = [jnp.exp2(s - mn) for s, mn in zip(ss, m_new)]
            pv = [_contract(p, vext, 1, 0, False) for p in ps]
            for h in range(grp):
                m_scr[h] = m_new[h]
                acc[h] = alpha[h] * accs[h] + pv[h]

        _att_paths(i, j, nctq, tk, lc, step)

        @pl.when(j == nk - 1)
        def _():
            for h in range(grp):
                a = acc[h]
                o_ref[:, h * HEAD_DIM:(h + 1) * HEAD_DIM] = a[:, :HEAD_DIM] / a[:, HEAD_DIM:]
                lse_ref[h] = m_scr[h] * LN2 + jnp.log(a[:, HEAD_DIM:HEAD_DIM + 1])

    wid = grp * HEAD_DIM
    return pl.pallas_call(
        body, name=name, grid=(nkv, nq, nk),
        in_specs=[pl.BlockSpec((tq, wid), lambda g, i, j: (i, g)),
                  pl.BlockSpec((tk, HEAD_DIM), lambda g, i, j: (j, g)),
                  pl.BlockSpec((tk, HEAD_DIM), lambda g, i, j: (j, g))],
        out_specs=[pl.BlockSpec((tq, wid), lambda g, i, j: (i, g)),
                   pl.BlockSpec((grp, tq, 1), lambda g, i, j: (g, i, 0))],
        out_shape=[jax.ShapeDtypeStruct((t, nkv * wid), F32), jax.ShapeDtypeStruct((nkv * grp, t, 1), F32)],
        scratch_shapes=[pltpu.VMEM((grp, tq, 1), F32), pltpu.VMEM((grp, tq, 2 * HEAD_DIM), F32)],
        compiler_params=pltpu.CompilerParams(dimension_semantics=("parallel", "parallel", "arbitrary")),
    )(qn, kn, v)


def attn_bwd_dq(qn, kn, v, o, do, lse, lc, name):
    t, nkv, grp, tq, tk, nq, nk, nctq = _att_grid(qn, kn, lc)
    tk = _pick(t, ATT_DQ_TK)
    nk = t // tk

    def body(q_ref, k_ref, v_ref, o_ref, do_ref, lse_ref, dq_ref, dl_ref, acc, dl_scr):
        i, j = pl.program_id(1), pl.program_id(2)

        @pl.when(j == 0)
        def _():
            acc[...] = jnp.zeros_like(acc)
            for h in range(grp):
                sl = slice(h * HEAD_DIM, (h + 1) * HEAD_DIM)
                dl_scr[h] = jnp.sum(do_ref[:, sl] * o_ref[:, sl], axis=1, keepdims=True)

        def step(hidden):
            k, vv = k_ref[...], v_ref[...]
            sls = [slice(h * HEAD_DIM, (h + 1) * HEAD_DIM) for h in range(grp)]
            ss = [_att_scores2(q_ref[:, sl], k, hidden) for sl in sls]
            dps = [_contract(do_ref[:, sl], vv, 1, 1, False) for sl in sls]
            ps = [jnp.exp2(s - lse_ref[h] * LOG2E) for h, s in enumerate(ss)]
            dss = [p * (dp - dl_scr[h]) for h, (p, dp) in enumerate(zip(ps, dps))]
            upd = [_contract(ds, k, 1, 0, False) for ds in dss]
            for h in range(grp):
                acc[h] += upd[h]

        _att_paths(i, j, nctq, tk, lc, step)

        @pl.when(j == nk - 1)
        def _():
            for h in range(grp):
                dq_ref[:, h * HEAD_DIM:(h + 1) * HEAD_DIM] = acc[h] * QK_SCALE
                dl_ref[h] = dl_scr[h]

    wid = grp * HEAD_DIM
    qspec = pl.BlockSpec((tq, wid), lambda g, i, j: (i, g))
    kspec = pl.BlockSpec((tk, HEAD_DIM), lambda g, i, j: (j, g))
    rspec = pl.BlockSpec((grp, tq, 1), lambda g, i, j: (g, i, 0))
    return pl.pallas_call(
        body, name=name, grid=(nkv, nq, nk),
        in_specs=[qspec, kspec, kspec, qspec, qspec, rspec],
        out_specs=[qspec, rspec],
        out_shape=[jax.ShapeDtypeStruct((t, nkv * wid), F32), jax.ShapeDtypeStruct((nkv * grp, t, 1), F32)],
        scratch_shapes=[pltpu.VMEM((grp, tq, HEAD_DIM), F32), pltpu.VMEM((grp, tq, 1), F32)],
        compiler_params=pltpu.CompilerParams(dimension_semantics=("parallel", "parallel", "arbitrary")),
    )(qn, kn, v, o, do, lse)


def attn_bwd_dkv(qn, kn, v, do, lse, delta, lc, name):
    t, nkv, grp, tq, tk, nq, nk, nctq = _att_grid(qn, kn, lc)

    def body(q_ref, k_ref, v_ref, do_ref, lse_ref, dl_ref, dk_ref, dv_ref, dk_acc, dv_acc):
        j, i = pl.program_id(1), pl.program_id(2)

        @pl.when(i == 0)
        def _():
            dk_acc[...] = jnp.zeros_like(dk_acc)
            dv_acc[...] = jnp.zeros_like(dv_acc)

        def step(hidden):
            k, vv = k_ref[...], v_ref[...]
            sls = [slice(h * HEAD_DIM, (h + 1) * HEAD_DIM) for h in range(grp)]
            qs = [q_ref[:, sl] for sl in sls]
            dos = [do_ref[:, sl] for sl in sls]
            ss = [_att_scores2(q, k, hidden) for q in qs]
            dps = [_contract(do, vv, 1, 1, False) for do in dos]
            ps = [jnp.exp2(s - lse_ref[h] * LOG2E) for h, s in enumerate(ss)]
            dss = [p * (dp - dl_ref[h]) for h, (p, dp) in enumerate(zip(ps, dps))]
            dv_new = [_contract(p, do, 0, 0, False) for p, do in zip(ps, dos)]
            dk_new = [_contract(ds, q, 0, 0, False) for ds, q in zip(dss, qs)]
            dv_acc[...] += (dv_new[0] + dv_new[1]) + (dv_new[2] + dv_new[3]) if grp == 4 else sum(dv_new)
            dk_acc[...] += (dk_new[0] + dk_new[1]) + (dk_new[2] + dk_new[3]) if grp == 4 else sum(dk_new)

        _att_paths(i, j, nctq, tk, lc, step)

        @pl.when(i == nq - 1)
        def _():
            dk_ref[...] = dk_acc[...] * LN2
            dv_ref[...] = dv_acc[...]

    wid = grp * HEAD_DIM
    qspec = pl.BlockSpec((tq, wid), lambda g, j, i: (i, g))
    kspec = pl.BlockSpec((tk, HEAD_DIM), lambda g, j, i: (j, g))
    rspec = pl.BlockSpec((grp, tq, 1), lambda g, j, i: (g, i, 0))
    return pl.pallas_call(
        body, name=name, grid=(nkv, nk, nq),
        in_specs=[qspec, kspec, kspec, qspec, rspec, rspec],
        out_specs=[kspec, kspec],
        out_shape=[jax.ShapeDtypeStruct((t, nkv * HEAD_DIM), F32), jax.ShapeDtypeStruct((t, nkv * HEAD_DIM), F32)],
        scratch_shapes=[pltpu.VMEM((tk, HEAD_DIM), F32), pltpu.VMEM((tk, HEAD_DIM), F32)],
        compiler_params=pltpu.CompilerParams(dimension_semantics=("parallel", "parallel", "arbitrary")),
    )(qn, kn, v, do, lse, delta)


def loss_and_grad(y, target, tm, name):
    t, d = y.shape

    def body(y_ref, t_ref, l_ref, g_ref):
        @pl.when(pl.program_id(0) == 0)
        def _():
            l_ref[...] = jnp.zeros_like(l_ref)

        e = y_ref[...] - t_ref[...]
        g_ref[...] = e * (1.0 / d)
        l_ref[...] += 0.5 * jnp.sum(jnp.mean(e * e, axis=1, keepdims=True), axis=0, keepdims=True)

    return pl.pallas_call(
        body, name=name, grid=(t // tm,),
        in_specs=[pl.BlockSpec((tm, d), lambda i: (i, 0))] * 2,
        out_specs=[pl.BlockSpec((1, 1), lambda i: (0, 0)), pl.BlockSpec((tm, d), lambda i: (i, 0))],
        out_shape=[jax.ShapeDtypeStruct((1, 1), F32), jax.ShapeDtypeStruct((t, d), F32)],
        compiler_params=pltpu.CompilerParams(dimension_semantics=("arbitrary",)),
    )(y, target)


def _row_tile(rows, width):
    budget = max(8, (2 * 1024 * 1024) // (4 * width))
    for cand in (1024, 512, 256, 128, 64, 32, 16, 8):
        if cand <= budget and rows % cand == 0:
            return cand
    return rows


def ew_sum(arrs, name, out_dtype=F32):
    rows, width = arrs[0].shape
    tr = _row_tile(rows, width)

    def body(*refs):
        acc = refs[0][...].astype(F32)
        for r in refs[1:-1]:
            acc = acc + r[...].astype(F32)
        refs[-1][...] = acc.astype(refs[-1].dtype)

    spec = pl.BlockSpec((tr, width), lambda i: (i, 0))
    return pl.pallas_call(
        body, name=name, grid=(rows // tr,), in_specs=[spec] * len(arrs), out_specs=spec,
        out_shape=jax.ShapeDtypeStruct((rows, width), out_dtype),
        compiler_params=pltpu.CompilerParams(dimension_semantics=("parallel",)),
    )(*arrs)


def adamw(g, w, m, v, name):
    rows, width = g.shape
    tr = _row_tile(rows, width)

    def body(g_ref, w_ref, m_ref, v_ref, d_ref, mo_ref, vo_ref):
        gg = g_ref[...]
        m_new = ADAM_B1 * m_ref[...] + (1.0 - ADAM_B1) * gg
        v_new = ADAM_B2 * v_ref[...] + (1.0 - ADAM_B2) * jnp.square(gg)
        m_hat = m_new / (1.0 - ADAM_B1 ** ADAM_STEP)
        v_hat = v_new / (1.0 - ADAM_B2 ** ADAM_STEP)
        d_ref[...] = -ADAM_LR * (m_hat / (jnp.sqrt(v_hat) + ADAM_EPS) + ADAM_WD * w_ref[...])
        mo_ref[...] = m_new
        vo_ref[...] = v_new

    spec = pl.BlockSpec((tr, width), lambda i: (i, 0))
    return pl.pallas_call(
        body, name=name, grid=(rows // tr,), in_specs=[spec] * 4, out_specs=[spec] * 3,
        out_shape=[jax.ShapeDtypeStruct((rows, width), F32)] * 3,
        compiler_params=pltpu.CompilerParams(dimension_semantics=("parallel",)),
    )(g, w, m, v)


def _place():
    return lax.axis_index("x"), lax.axis_index("y"), lax.axis_index("c")


def _other_chips(x, y):
    return [(1 - x, y), (x, 1 - y), (1 - x, 1 - y)]


def allgather8(x_shard, name):
    m_per, n = x_shard.shape

    def body(x_ref, out_ref, send_sems, recv_sems, local_sem):
        x, y, c = _place()
        me, sibling = (x, y, c), (x, y, 1 - c)
        chips = _other_chips(x, y)

        def rows(px, py, pc):
            return out_ref.at[pl.ds((4 * px + 2 * py + pc) * m_per, m_per), :]

        def copy(k, block, to, src=None):
            return pltpu.make_async_remote_copy(
                src_ref=rows(*block) if src is None else src, dst_ref=rows(*block),
                send_sem=send_sems.at[k], recv_sem=recv_sems.at[k], device_id=to, device_id_type=MESH_ID)

        mine = pltpu.make_async_copy(x_ref, rows(*me), local_sem)
        mine.start()
        first = [copy(0, me, sibling, src=x_ref)]
        first += [copy(1 + j, me, (*chip, c), src=x_ref) for j, chip in enumerate(chips)]
        for cp in first:
            cp.start()
        passed = [copy(4 + j, (*chip, c), sibling) for j, chip in enumerate(chips)]
        for j, chip in enumerate(chips):
            copy(1 + j, (*chip, c), me).wait_recv()
            passed[j].start()
        copy(0, sibling, me).wait_recv()
        for j, chip in enumerate(chips):
            copy(4 + j, (*chip, 1 - c), me).wait_recv()
        for cp in first + passed:
            cp.wait_send()
        mine.wait()

    return pl.pallas_call(
        body, name=name,
        out_shape=jax.ShapeDtypeStruct((8 * m_per, n), x_shard.dtype),
        in_specs=[pl.BlockSpec(memory_space=pltpu.VMEM)],
        out_specs=pl.BlockSpec(memory_space=pltpu.VMEM),
        scratch_shapes=[pltpu.SemaphoreType.DMA((7,)), pltpu.SemaphoreType.DMA((7,)), pltpu.SemaphoreType.DMA],
    )(x_shard)


_ANY = pl.BlockSpec(memory_space=pl.ANY)


def gather_chips(shards, name):
    n = len(shards)

    def body(*refs):
        ins, outs = refs[:n], refs[n:2 * n]
        send_sems, recv_sems, local_sems = refs[2 * n:]
        x, y, c = _place()
        chips = _other_chips(x, y)
        started = []
        for a in range(n):
            loc = pltpu.make_async_copy(ins[a], outs[a].at[2 * x + y], local_sems.at[a])
            loc.start()
            started.append(loc)
        sends = []
        for a in range(n):
            for j, chip in enumerate(chips):
                cp = pltpu.make_async_remote_copy(
                    src_ref=ins[a], dst_ref=outs[a].at[2 * x + y], send_sem=send_sems.at[3 * a + j],
                    recv_sem=recv_sems.at[3 * a + j], device_id=(*chip, c), device_id_type=MESH_ID)
                cp.start()
                sends.append(cp)
        for a in range(n):
            for j, chip in enumerate(chips):
                pltpu.make_async_remote_copy(
                    src_ref=ins[a], dst_ref=outs[a].at[2 * chip[0] + chip[1]], send_sem=send_sems.at[3 * a + j],
                    recv_sem=recv_sems.at[3 * a + j], device_id=(*chip, c), device_id_type=MESH_ID).wait_recv()
        for cp in sends:
            cp.wait_send()
        for loc in started:
            loc.wait()

    return pl.pallas_call(
        body, name=name,
        out_shape=[jax.ShapeDtypeStruct((4,) + s.shape, s.dtype) for s in shards],
        in_specs=[_ANY] * n, out_specs=[_ANY] * n,
        scratch_shapes=[pltpu.SemaphoreType.DMA((3 * n,)), pltpu.SemaphoreType.DMA((3 * n,)), pltpu.SemaphoreType.DMA((n,))],
    )(*shards)


def rs_sibling(grads, name):
    n = len(grads)

    def body(*refs):
        ins, mine, got = refs[:n], refs[n:2 * n], refs[2 * n:3 * n]
        send_sems, recv_sems, local_sems = refs[3 * n:]
        x, y, c = _place()
        pend = []
        for a in range(n):
            h = ins[a].shape[1] // 2
            loc = pltpu.make_async_copy(ins[a].at[:, pl.ds(pl.multiple_of(c * h, 8), h), :], mine[a], local_sems.at[a])
            cp = pltpu.make_async_remote_copy(
                src_ref=ins[a].at[:, pl.ds(pl.multiple_of((1 - c) * h, 8), h), :], dst_ref=got[a],
                send_sem=send_sems.at[a], recv_sem=recv_sems.at[a], device_id=(x, y, 1 - c), device_id_type=MESH_ID)
            loc.start()
            cp.start()
            pend.append((loc, cp))
        for loc, cp in pend:
            cp.wait()
            loc.wait()

    half = [jax.ShapeDtypeStruct((g.shape[0], g.shape[1] // 2, g.shape[2]), g.dtype) for g in grads]
    return pl.pallas_call(
        body, name=name, out_shape=half + half, in_specs=[_ANY] * n, out_specs=[_ANY] * (2 * n),
        scratch_shapes=[pltpu.SemaphoreType.DMA((n,)), pltpu.SemaphoreType.DMA((n,)), pltpu.SemaphoreType.DMA((n,))],
    )(*grads)


def rs_chips(parts, name):
    n = len(parts)

    def body(*refs):
        ins, mine, got = refs[:n], refs[n:2 * n], refs[2 * n:3 * n]
        send_sems, recv_sems, local_sems = refs[3 * n:]
        x, y, c = _place()
        chips = _other_chips(x, y)
        pend = []
        for a in range(n):
            loc = pltpu.make_async_copy(ins[a].at[2 * x + y], mine[a], local_sems.at[a])
            loc.start()
            pend.append(loc)
            for j, chip in enumerate(chips):
                cp = pltpu.make_async_remote_copy(
                    src_ref=ins[a].at[2 * chip[0] + chip[1]], dst_ref=got[a].at[j],
                    send_sem=send_sems.at[3 * a + j], recv_sem=recv_sems.at[3 * a + j],
                    device_id=(*chip, c), device_id_type=MESH_ID)
                cp.start()
                pend.append(cp)
        for p in pend:
            p.wait()

    return pl.pallas_call(
        body, name=name,
        out_shape=[jax.ShapeDtypeStruct(p.shape[1:], p.dtype) for p in parts]
        + [jax.ShapeDtypeStruct((3,) + p.shape[1:], p.dtype) for p in parts],
        in_specs=[_ANY] * n, out_specs=[_ANY] * (2 * n),
        scratch_shapes=[pltpu.SemaphoreType.DMA((3 * n,)), pltpu.SemaphoreType.DMA((3 * n,)), pltpu.SemaphoreType.DMA((n,))],
    )(*parts)


def share_sibling(halves, name):
    n = len(halves)

    def body(*refs):
        ins, outs = refs[:n], refs[n:2 * n]
        send_sems, recv_sems, local_sems = refs[2 * n:]
        x, y, c = _place()
        pend = []
        for a in range(n):
            loc = pltpu.make_async_copy(ins[a], outs[a].at[c], local_sems.at[a])
            cp = pltpu.make_async_remote_copy(
                src_ref=ins[a], dst_ref=outs[a].at[c], send_sem=send_sems.at[a], recv_sem=recv_sems.at[a],
                device_id=(x, y, 1 - c), device_id_type=MESH_ID)
            loc.start()
            cp.start()
            pend.append((loc, cp))
        for a, (loc, cp) in enumerate(pend):
            cp.wait_send()
            pltpu.make_async_remote_copy(
                src_ref=ins[a], dst_ref=outs[a].at[1 - c], send_sem=send_sems.at[a], recv_sem=recv_sems.at[a],
                device_id=(x, y, 1 - c), device_id_type=MESH_ID).wait_recv()
            loc.wait()

    return pl.pallas_call(
        body, name=name, out_shape=[jax.ShapeDtypeStruct((2,) + h.shape, h.dtype) for h in halves],
        in_specs=[_ANY] * n, out_specs=[_ANY] * n,
        scratch_shapes=[pltpu.SemaphoreType.DMA((n,)), pltpu.SemaphoreType.DMA((n,)), pltpu.SemaphoreType.DMA((n,))],
    )(*halves)


def reduce_scatter(grads, tag):
    mine, got = _split(rs_sibling(grads, name=f"rs_sibling_{tag}"))
    pair = [ew_sum([a.reshape(-1, a.shape[2]), b.reshape(-1, b.shape[2])], name=f"rs_pair_{tag}_{i}").reshape(a.shape)
            for i, (a, b) in enumerate(zip(mine, got))]
    own, recv = _split(rs_chips(pair, name=f"rs_chips_{tag}"))
    tot = [ew_sum([a, b[0], b[1], b[2]], name=f"rs_quad_{tag}_{i}") for i, (a, b) in enumerate(zip(own, recv))]
    both = share_sibling(tot, name=f"rs_share_{tag}")
    return [b.reshape(-1, b.shape[2]) for b in both]


def _split(lst):
    n = len(lst) // 2
    return lst[:n], lst[n:]


def _sibling():
    x, y, c = _place()
    return (x, y, 1 - c)


def send_rows(src, name):
    r, c = src.shape
    tr = _row_tile(r, c)
    n = r // tr

    def body(x_ref, out_ref, send_sem, recv_sem):
        i = pl.program_id(0)
        cp = pltpu.make_async_remote_copy(
            src_ref=x_ref, dst_ref=out_ref.at[pl.ds(pl.multiple_of(i * tr, 8), tr), :], send_sem=send_sem,
            recv_sem=recv_sem, device_id=_sibling(), device_id_type=MESH_ID)
        cp.start()
        cp.wait_send()

        @pl.when(i == n - 1)
        def _():
            pltpu.make_async_remote_copy(src_ref=out_ref, dst_ref=out_ref, send_sem=send_sem, recv_sem=recv_sem,
                                         device_id=_sibling(), device_id_type=MESH_ID).wait_recv()

    return pl.pallas_call(
        body, name=name, grid=(n,), in_specs=[pl.BlockSpec((tr, c), lambda i: (i, 0))], out_specs=_ANY,
        out_shape=jax.ShapeDtypeStruct((r, c), src.dtype),
        scratch_shapes=[pltpu.SemaphoreType.DMA, pltpu.SemaphoreType.DMA],
        compiler_params=pltpu.CompilerParams(dimension_semantics=("arbitrary",)),
    )(src)


def gather_chips(shards, name):
    n = len(shards)

    def body(*refs):
        ins, outs = refs[:n], refs[n:2 * n]
        send_sems, recv_sems = refs[2 * n:]
        x, y, c = _place()
        chips = _other_chips(x, y)
        sends = []
        for a in range(n):
            for j, chip in enumerate(chips):
                cp = pltpu.make_async_remote_copy(
                    src_ref=ins[a], dst_ref=outs[a].at[2 * x + y], send_sem=send_sems.at[3 * a + j],
                    recv_sem=recv_sems.at[3 * a + j], device_id=(*chip, c), device_id_type=MESH_ID)
                cp.start()
                sends.append(cp)
        for a in range(n):
            for j, chip in enumerate(chips):
                pltpu.make_async_remote_copy(
                    src_ref=ins[a], dst_ref=outs[a].at[2 * chip[0] + chip[1]], send_sem=send_sems.at[3 * a + j],
                    recv_sem=recv_sems.at[3 * a + j], device_id=(*chip, c), device_id_type=MESH_ID).wait_recv()
        for cp in sends:
            cp.wait_send()

    res = pl.pallas_call(
        body, name=name,
        out_shape=[jax.ShapeDtypeStruct((4,) + s.shape, s.dtype) for s in shards],
        in_specs=[_ANY] * n, out_specs=[_ANY] * n,
        scratch_shapes=[pltpu.SemaphoreType.DMA((3 * n,)), pltpu.SemaphoreType.DMA((3 * n,))],
    )(*shards)
    x, y, _ = _place()
    return [lax.dynamic_update_slice(g, s[None], (2 * x + y,) + (0,) * s.ndim) for g, s in zip(res, shards)]


def gather_chips(shards, name):
    n = len(shards)

    def body(*refs):
        ins, outs = refs[:n], refs[n:2 * n]
        send_sems, recv_sems = refs[2 * n:]
        x, y, c = _place()
        chips = _other_chips(x, y)
        sends = []
        for a in range(n):
            h = ins[a].shape[0] // 2
            src = ins[a].at[pl.ds(pl.multiple_of(c * h, 16), h), :]
            for j, chip in enumerate(chips):
                cp = pltpu.make_async_remote_copy(
                    src_ref=src, dst_ref=outs[a].at[2 * x + y], send_sem=send_sems.at[3 * a + j],
                    recv_sem=recv_sems.at[3 * a + j], device_id=(*chip, c), device_id_type=MESH_ID)
                cp.start()
                sends.append(cp)
        for a in range(n):
            h = ins[a].shape[0] // 2
            src = ins[a].at[pl.ds(pl.multiple_of(c * h, 16), h), :]
            for j, chip in enumerate(chips):
                pltpu.make_async_remote_copy(
                    src_ref=src, dst_ref=outs[a].at[2 * chip[0] + chip[1]], send_sem=send_sems.at[3 * a + j],
                    recv_sem=recv_sems.at[3 * a + j], device_id=(*chip, c), device_id_type=MESH_ID).wait_recv()
        for cp in sends:
            cp.wait_send()

    halves = pl.pallas_call(
        body, name=name,
        out_shape=[jax.ShapeDtypeStruct((4, s.shape[0] // 2, s.shape[1]), s.dtype) for s in shards],
        in_specs=[_ANY] * n, out_specs=[_ANY] * n,
        scratch_shapes=[pltpu.SemaphoreType.DMA((3 * n,)), pltpu.SemaphoreType.DMA((3 * n,))],
    )(*shards)
    x, y, ci = _place()
    res = []
    for a, (g, s) in enumerate(zip(halves, shards)):
        h = s.shape[0] // 2
        g = lax.dynamic_update_slice(g, lax.dynamic_slice_in_dim(s, ci * h, h, 0)[None], (2 * x + y, 0, 0))
        other = send_rows(g.reshape(4 * h, s.shape[1]), f"{name}_sib_{a}").reshape(g.shape)
        res.append(jnp.concatenate([jnp.where(ci == 0, g, other), jnp.where(ci == 0, other, g)], 1))
    return res


def rs_chips(parts, name):
    n = len(parts)

    def body(*refs):
        ins, got = refs[:n], refs[n:2 * n]
        send_sems, recv_sems = refs[2 * n:]
        x, y, c = _place()
        chips = _other_chips(x, y)
        pend = []
        for a in range(n):
            for j, chip in enumerate(chips):
                cp = pltpu.make_async_remote_copy(
                    src_ref=ins[a].at[2 * chip[0] + chip[1]], dst_ref=got[a].at[j],
                    send_sem=send_sems.at[3 * a + j], recv_sem=recv_sems.at[3 * a + j],
                    device_id=(*chip, c), device_id_type=MESH_ID)
                cp.start()
                pend.append(cp)
        for p in pend:
            p.wait()

    return pl.pallas_call(
        body, name=name,
        out_shape=[jax.ShapeDtypeStruct((3,) + p.shape[1:], p.dtype) for p in parts],
        in_specs=[_ANY] * n, out_specs=[_ANY] * n,
        scratch_shapes=[pltpu.SemaphoreType.DMA((3 * n,)), pltpu.SemaphoreType.DMA((3 * n,))],
    )(*parts)


def reduce_scatter(grads, tag):
    x, y, ci = _place()
    chip = 2 * x + y
    out = []
    pairs = []
    for i, g in enumerate(grads):
        _, r, c = g.shape
        h = r // 2
        keep = lax.dynamic_slice_in_dim(g, ci * h, h, 1).reshape(4 * h, c)
        give = lax.dynamic_slice_in_dim(g, (1 - ci) * h, h, 1).reshape(4 * h, c).astype(MXU_DTYPE)
        got = send_rows(give, f"rs_sibling_{tag}_{i}")
        pairs.append(ew_sum([keep, got], name=f"rs_pair_{tag}_{i}").reshape(4, h, c))
    recv = rs_chips([p.astype(MXU_DTYPE) for p in pairs], f"rs_chips_{tag}")
    for i, (p, b) in enumerate(zip(pairs, recv)):
        own = lax.dynamic_index_in_dim(p, chip, 0, keepdims=False)
        tot = ew_sum([own, b[0], b[1], b[2]], name=f"rs_quad_{tag}_{i}")
        other = send_rows(tot, f"rs_share_{tag}_{i}")
        h = tot.shape[0]
        out.append(lax.dynamic_update_slice(jnp.concatenate([other, other], 0), tot, (ci * h, 0)))
    return out


def sum_entries(g, idxs, name):
    _, rows, width = g.shape
    tr = _row_tile(rows, width)

    def body(g_ref, o_ref):
        acc = g_ref[idxs[0]]
        for d in idxs[1:]:
            acc = acc + g_ref[d]
        o_ref[...] = acc

    return pl.pallas_call(
        body, name=name, grid=(rows // tr,),
        in_specs=[pl.BlockSpec((8, tr, width), lambda i: (0, i, 0))], out_specs=pl.BlockSpec((tr, width), lambda i: (i, 0)),
        out_shape=jax.ShapeDtypeStruct((rows, width), F32),
        compiler_params=pltpu.CompilerParams(dimension_semantics=("parallel",)),
    )(g)


def _pack(arrs, rows_multiple=8):
    parts, offs, r = [], [], 0
    for a in arrs:
        flat = a.reshape(-1).astype(F32)
        nrow = -(-flat.shape[0] // LANES)
        parts.append(jnp.pad(flat, (0, nrow * LANES - flat.shape[0])).reshape(nrow, LANES))
        offs.append((r, nrow, a.shape))
        r += nrow
    pad = (-r) % rows_multiple
    if pad:
        parts.append(jnp.zeros((pad, LANES), F32))
    return jnp.concatenate(parts, 0), offs


def _unpack(slab, offs):
    outs = []
    for r, nrow, shape in offs:
        size = 1
        for s in shape:
            size *= s
        outs.append(slab[r:r + nrow].reshape(-1)[:size].reshape(shape))
    return outs


def _seqflip(a, lc):
    return jnp.concatenate([jnp.flip(a[:lc], 0), jnp.flip(a[lc:], 0)], 0)


def _slab(vec8):
    return jnp.pad(vec8.reshape(1, -1).astype(F32), ((0, 0), (0, LANES - vec8.size)))


def _rope_tables(n_lat, lc):
    rows = n_lat // GRID_W
    row = jnp.repeat(jnp.arange(rows, dtype=F32), GRID_W)
    col = jnp.tile(jnp.arange(GRID_W, dtype=F32), rows)
    n_freq = HEAD_DIM // 4
    inv = ROPE_THETA ** (-jnp.arange(n_freq, dtype=F32) / n_freq)
    ang = jnp.concatenate([row[:, None] * inv, col[:, None] * inv], -1)
    cos, sin = jnp.cos(ang), jnp.sin(ang)
    cos_t = jnp.concatenate([jnp.ones((lc, HEAD_DIM), F32), jnp.concatenate([cos, cos], -1)], 0)
    sin_t = jnp.concatenate([jnp.zeros((lc, HEAD_DIM), F32), jnp.concatenate([-sin, sin], -1)], 0)
    return cos_t, sin_t


def _permute_w_in(w):
    return jnp.concatenate([w[:, :4096], w[:, 4112:], w[:, 4096:4112], jnp.zeros((w.shape[0], PROJ_PAD - PROJ_W), w.dtype)], 1)


def _unpermute_dw_in(dw):
    return jnp.concatenate([dw[:, :4096], dw[:, 5632:5648], dw[:, 4096:5632]], 1)


RET_SELS = [[(d * RET_HEADS + h, d == 1) for h in range(RET_HEADS)] for d in range(2)]
DN_SELS = [[(d * DN_HEADS + h, (2 + d) * DN_HEADS + h, d == 1) for h in range(DN_HEADS)] for d in range(2)]


def _layer_fwd(xin, mods, wts, prm, tabs, lc, tm, i):
    nct = lc // tm
    seg = lambda j: mods[:, j:j + 1, :]
    cos_t, sin_t = tabs
    sv = {}
    (h1,) = rowwise(f"mod1_{i}", modulate_fn, [xin], [seg(0), seg(1)], [], [(D_MODEL, MXU_DTYPE)], tm, nct)
    p = mm(h1, wts["w_in"], "nn", name=f"proj_in_{i}")
    rq, rk = rowwise(f"ret_prep_{i}", ret_prep_fn, [(p, 512, 0), (p, 512, 1), cos_t, sin_t], [], [],
                     [(512, F32), (512, F32)], tm, nct)
    rv = p[:, 1024:1536]
    r_of, r_ob, r_st = scan_fwd(ret_chunk, RET_CHUNK, rq, rk, rv, prm["ret_logit"], RET_SELS, lc, f"ret_scan_{i}")
    (y_ret,) = rowwise(f"ret_out_{i}", ret_out_fn, [r_of, r_ob, (p, 512, 3)], [], [], [(512, F32)], tm, nct)
    qkvc = conv_fwd(p, 2048, 3 * 512, prm["conv_w"], lc, f"dn_conv_{i}")
    dq, dk, gb = rowwise(f"dn_prep_{i}", dn_prep_fn, [(qkvc, 512, 0), (qkvc, 512, 1), (p, LANES, 44)], [],
                         [prm["a_log"], prm["dt_b"]], [(512, F32), (512, F32), (LANES, F32)], tm, nct)
    dv = qkvc[:, 1024:1536]
    d_of, d_ob, d_st = scan_fwd(dn_chunk, DN_CHUNK, dq, dk, dv, gb, DN_SELS, lc, f"dn_scan_{i}")
    (y_dn,) = rowwise(f"dn_out_{i}", dn_out_fn, [d_of, d_ob, (p, 512, 7)], [], [prm["dn_norm_w"]], [(512, F32)], tm, nct)
    aq, ak, av = rowwise(f"att_prep_{i}", lambda q, k, v, *rest: (lambda qk: (qk[0] * (QK_SCALE * LOG2E), qk[1], v))(att_prep_fn(q, k, *rest)),
                         [(p, 1024, 4), (p, 256, 20), (p, 256, 21), cos_t, sin_t], [], [prm["qn_w"], prm["kn_w"]],
                         [(1024, MXU_DTYPE), (256, MXU_DTYPE), (256, MXU_DTYPE)], tm, nct)
    ao, lse = attn_fwd(aq, ak, av, lc, f"attn_fwd_{i}")
    y = jnp.concatenate([y_ret, y_dn, ao], 1)
    a1 = mm(y, wts["w_o"], "nn", name=f"proj_out_{i}")
    (x1,) = rowwise(f"postnorm1_{i}", postnorm_fn, [xin, a1], [seg(2)], [prm["ln1_w"], prm["ln1_b"]], [(D_MODEL, F32)], tm, nct)
    (h2,) = rowwise(f"mod2_{i}", modulate_fn, [x1], [seg(3), seg(4)], [], [(D_MODEL, MXU_DTYPE)], tm, nct)
    u = mm(h2, wts["w_ffn_in"], "nn", out_dtype=MXU_DTYPE, name=f"ffn_in_{i}")
    act = swiglu_fwd(u, tm, f"swiglu_{i}")
    a2 = mm(act, wts["w_ffn_out"], "nn", name=f"ffn_out_{i}")
    (x2,) = rowwise(f"postnorm2_{i}", postnorm_fn, [x1, a2], [seg(5)], [prm["ln2_w"], prm["ln2_b"]], [(D_MODEL, F32)], tm, nct)
    sv.update(xin=xin, h1=h1, p=p, rq=rq, rk=rk, rv=rv, r_st=r_st, r_of=r_of, r_ob=r_ob, qkvc=qkvc, dq=dq, dk=dk, dv=dv,
              gb=gb, d_st=d_st, d_of=d_of, d_ob=d_ob, aq=aq, ak=ak, av=av, ao=ao, lse=lse, y=y, a1=a1, x1=x1, h2=h2,
              u=u, act=act, a2=a2)
    return x2, sv


def _layer_bwd(dx2, sv, mods, wts, prm, tabs, lc, tm, i):
    nct = lc // tm
    seg = lambda j: mods[:, j:j + 1, :]
    cos_t, sin_t = tabs
    p = sv["p"]
    both = lambda g: g[0] + g[1]
    (dx1a, da2), (dgate2,), (dln2w, dln2b) = rowwise_bwd(
        f"postnorm2_b_{i}", postnorm_fn, [sv["x1"], sv["a2"]], [seg(5)], [prm["ln2_w"], prm["ln2_b"]], [dx2], [True, True], tm, nct)
    dact = mm(da2, wts["w_ffn_out"], "nt", name=f"ffn_out_dx_{i}")
    dw_ffn_out = mm(sv["act"], da2, "tn", name=f"ffn_out_dw_{i}")
    du = swiglu_bwd(sv["u"], dact, tm, f"swiglu_b_{i}")
    dh2 = mm(du, wts["w_ffn_in"], "nt", name=f"ffn_in_dx_{i}")
    dw_ffn_in = mm(sv["h2"], du, "tn", name=f"ffn_in_dw_{i}")
    (dx1b,), (dshift2, dscale2), _ = rowwise_bwd(
        f"mod2_b_{i}", modulate_fn, [sv["x1"]], [seg(3), seg(4)], [], [dh2], [True], tm, nct)
    dx1 = ew_sum([dx1a, dx1b], name=f"dx1_{i}")
    (dxa, da1), (dgate1,), (dln1w, dln1b) = rowwise_bwd(
        f"postnorm1_b_{i}", postnorm_fn, [sv["xin"], sv["a1"]], [seg(2)], [prm["ln1_w"], prm["ln1_b"]], [dx1], [True, True], tm, nct)
    dy = mm(da1, wts["w_o"], "nt", name=f"proj_out_dx_{i}")
    dw_o = mm(sv["y"], da1, "tn", name=f"proj_out_dw_{i}")
    dy_ret, dy_dn, dao = dy[:, :512], dy[:, 512:1024], dy[:, 1024:]
    daq, delta = attn_bwd_dq(sv["aq"], sv["ak"], sv["av"], sv["ao"], dao, sv["lse"], lc, f"attn_dq_{i}")
    dak, dav = attn_bwd_dkv(sv["aq"], sv["ak"], sv["av"], dao, sv["lse"], delta, lc, f"attn_dkv_{i}")
    (dp_aq, dp_ak), _, (dqn_w, dkn_w) = rowwise_bwd(
        f"att_prep_b_{i}", att_prep_fn, [(p, 1024, 4), (p, 256, 20), cos_t, sin_t], [], [prm["qn_w"], prm["kn_w"]],
        [daq, dak], [True, True, False, False], tm, nct)
    (dd_o, dp_z), _, (ddn_norm_w,) = rowwise_bwd(
        f"dn_out_b_{i}", dn_out_fn, [sv["d_of"], sv["d_ob"], (p, 512, 7)], [], [prm["dn_norm_w"]], [dy_dn], [True, False, True], tm, nct)
    dqf, dkf, dvf, dqb, dkb, dvb, dgbf, dgbb = scan_bwd(
        dn_chunk, DN_CHUNK, sv["dq"], sv["dk"], sv["dv"], sv["gb"], sv["d_st"], dd_o, DN_SELS, lc, f"dn_scan_b_{i}")
    ddv = ew_sum([dvf, dvb], name=f"dn_dv_{i}")
    (dqc, dkc, dp_ab), _, (da_log, ddt_b) = rowwise_bwd(
        f"dn_prep_b_{i}", dn_prep_fn, [(sv["qkvc"], 512, 0), (sv["qkvc"], 512, 1), (p, LANES, 44)], [],
        [prm["a_log"], prm["dt_b"]], [[dqf, dqb], [dkf, dkb], [dgbf, dgbb]], [True, True, True], tm, nct)
    dqkvc = jnp.concatenate([dqc, dkc, ddv], 1)
    dp_qkv, dconv_w = conv_bwd(p, 2048, 3 * 512, prm["conv_w"], dqkvc, lc, f"dn_conv_b_{i}")
    (dr_o, dp_g), _, _ = rowwise_bwd(
        f"ret_out_b_{i}", ret_out_fn, [sv["r_of"], sv["r_ob"], (p, 512, 3)], [], [], [dy_ret], [True, False, True], tm, nct)
    drqf, drkf, drvf, drqb, drkb, drvb, dlogit = scan_bwd(
        ret_chunk, RET_CHUNK, sv["rq"], sv["rk"], sv["rv"], prm["ret_logit"], sv["r_st"], dr_o, RET_SELS, lc, f"ret_scan_b_{i}")
    drv = ew_sum([drvf, drvb], name=f"ret_dv_{i}")
    dret_logit = dlogit[0, :2 * RET_HEADS].reshape(2, RET_HEADS)
    (dp_rq, dp_rk), _, _ = rowwise_bwd(
        f"ret_prep_b_{i}", ret_prep_fn, [(p, 512, 0), (p, 512, 1), cos_t, sin_t], [], [], [[drqf, drqb], [drkf, drkb]],
        [True, True, False, False], tm, nct)
    dp = jnp.concatenate([dp_rq, dp_rk, drv, dp_g, dp_qkv, dp_z, dp_aq, dp_ak, dav, dp_ab], 1)
    dh1 = mm(dp, wts["w_in"], "nt", name=f"proj_in_dx_{i}")
    dw_in = mm(sv["h1"], dp, "tn", name=f"proj_in_dw_{i}")
    (dxb,), (dshift1, dscale1), _ = rowwise_bwd(
        f"mod1_b_{i}", modulate_fn, [sv["xin"]], [seg(0), seg(1)], [], [dh1], [True], tm, nct)
    dxin = ew_sum([dxa, dxb], name=f"dxin_{i}")
    dmods = jnp.concatenate([dshift1, dscale1, dgate1, dshift2, dscale2, dgate2], 1)
    big = dict(w_in=dw_in, w_o=dw_o, w_ffn_in=dw_ffn_in, w_ffn_out=dw_ffn_out)
    small = dict(ln1_w=both(dln1w)[0], ln1_b=both(dln1b)[0], ln2_w=both(dln2w)[0], ln2_b=both(dln2b)[0],
                 dn_norm_w=both(ddn_norm_w)[0], att_qn_w=both(dqn_w)[0], att_kn_w=both(dkn_w)[0],
                 dn_conv_w=dconv_w[:DN_CONV_K], ret_decay_logit=dret_logit,
                 dn_a_log=both(da_log)[0, :2 * DN_HEADS].reshape(2, DN_HEADS),
                 dn_dt_bias=both(ddt_b)[0, :2 * DN_HEADS].reshape(2, DN_HEADS))
    return dxin, dmods, big, small


BIG = ("w_in", "w_o", "w_ffn_in", "w_ffn_out")
SMALL = ("c_ctx", "b_ada", "ret_decay_logit", "dn_conv_w", "dn_a_log", "dn_dt_bias", "dn_norm_w", "att_qn_w", "att_kn_w",
         "ln1_w", "ln1_b", "ln2_w", "ln2_b")
WEIGHTS = ("c_ctx", "w_ada", "b_ada", "w_in", "ret_decay_logit", "dn_conv_w", "dn_a_log", "dn_dt_bias", "dn_norm_w",
           "att_qn_w", "att_kn_w", "w_o", "ln1_w", "ln1_b", "w_ffn_in", "w_ffn_out", "ln2_w", "ln2_b")


def _chip_major(g, name):
    if name in ("w_in", "w_ffn_in"):
        r, cols = g.shape
        return g.reshape(r, 4, cols // 4).transpose(1, 0, 2)
    return g.reshape(4, g.shape[0] // 4, g.shape[1])


def kernel(x, c, ctx, c_ctx, w_ada, b_ada, w_in, ret_decay_logit, dn_conv_w, dn_a_log, dn_dt_bias, dn_norm_w, att_qn_w, att_kn_w, w_o, ln1_w, ln1_b, w_ffn_in, w_ffn_out, ln2_w, ln2_b, loss_target, m_c_ctx, m_w_ada, m_b_ada, m_w_in, m_ret_decay_logit, m_dn_conv_w, m_dn_a_log, m_dn_dt_bias, m_dn_norm_w, m_att_qn_w, m_att_kn_w, m_w_o, m_ln1_w, m_ln1_b, m_w_ffn_in, m_w_ffn_out, m_ln2_w, m_ln2_b, v_c_ctx, v_w_ada, v_b_ada, v_w_in, v_ret_decay_logit, v_dn_conv_w, v_dn_a_log, v_dn_dt_bias, v_dn_norm_w, v_att_qn_w, v_att_kn_w, v_w_o, v_ln1_w, v_ln1_b, v_w_ffn_in, v_w_ffn_out, v_ln2_w, v_ln2_b):
    wv = dict(c_ctx=c_ctx, w_ada=w_ada, b_ada=b_ada, w_in=w_in, ret_decay_logit=ret_decay_logit, dn_conv_w=dn_conv_w,
              dn_a_log=dn_a_log, dn_dt_bias=dn_dt_bias, dn_norm_w=dn_norm_w, att_qn_w=att_qn_w, att_kn_w=att_kn_w, w_o=w_o,
              ln1_w=ln1_w, ln1_b=ln1_b, w_ffn_in=w_ffn_in, w_ffn_out=w_ffn_out, ln2_w=ln2_w, ln2_b=ln2_b)
    mv = dict(c_ctx=m_c_ctx, w_ada=m_w_ada, b_ada=m_b_ada, w_in=m_w_in, ret_decay_logit=m_ret_decay_logit,
              dn_conv_w=m_dn_conv_w, dn_a_log=m_dn_a_log, dn_dt_bias=m_dn_dt_bias, dn_norm_w=m_dn_norm_w,
              att_qn_w=m_att_qn_w, att_kn_w=m_att_kn_w, w_o=m_w_o, ln1_w=m_ln1_w, ln1_b=m_ln1_b, w_ffn_in=m_w_ffn_in,
              w_ffn_out=m_w_ffn_out, ln2_w=m_ln2_w, ln2_b=m_ln2_b)
    vv = dict(c_ctx=v_c_ctx, w_ada=v_w_ada, b_ada=v_b_ada, w_in=v_w_in, ret_decay_logit=v_ret_decay_logit,
              dn_conv_w=v_dn_conv_w, dn_a_log=v_dn_a_log, dn_dt_bias=v_dn_dt_bias, dn_norm_w=v_dn_norm_w,
              att_qn_w=v_att_qn_w, att_kn_w=v_att_kn_w, w_o=v_w_o, ln1_w=v_ln1_w, ln1_b=v_ln1_b, w_ffn_in=v_w_ffn_in,
              w_ffn_out=v_w_ffn_out, ln2_w=v_ln2_w, ln2_b=v_ln2_b)
    depth = w_in.shape[0]
    n_lat, lc = x.shape[1], ctx.shape[1]
    t = lc + n_lat
    tm = _pick(lc, (256, 128))
    xi, yi, ci = _place()
    bidx = 4 * xi + 2 * yi + ci
    chip = 2 * xi + yi
    ada_w = w_ada.shape[2]
    conv_sh = dn_conv_w.shape[2]

    slab0, offs0 = _pack([c, dn_conv_w])
    g0 = allgather8(slab0, "gather_cond").reshape(8, -1, LANES)
    c_all = jnp.concatenate([_unpack(g0[d], offs0)[0] for d in range(8)], 0)
    conv_full = jnp.concatenate([_unpack(g0[2 * k], offs0)[1] for k in range(4)], 2)
    c_raw = jnp.concatenate([c_all, c_ctx[None], jnp.zeros((LANES - 9, D_MODEL), F32)], 0)
    (cond,) = rowwise("cond_silu", lambda a: (_silu(a),), [c_raw], [], [], [(D_MODEL, F32)], LANES, 0)
    b_sh = lax.dynamic_slice(b_ada, (0, chip * ada_w), (depth, ada_w))
    mods_sh = []
    for i in range(depth):
        mi = mm(cond, w_ada[i], "nn", name=f"ada_{i}")
        (mi,) = rowwise(f"ada_bias_{i}", lambda a, b: (a + b,), [mi], [], [b_sh[i:i + 1]], [(ada_w, F32)], LANES, 0)
        mods_sh.append(mi[:16])
    slab1, offs1 = _pack([jnp.stack(mods_sh)])
    g1 = allgather8(slab1, "gather_mods").reshape(8, -1, LANES)
    mods_all = jnp.concatenate([_unpack(g1[2 * k], offs1)[0] for k in range(4)], 2)
    mod_lat = lax.dynamic_index_in_dim(mods_all, bidx, 1, keepdims=False)
    mod_ctx = mods_all[:, 8]
    mods = jnp.stack([mod_ctx, mod_lat], 1).reshape(depth, 2, 6, D_MODEL)

    wbf = {n: ew_sum([wv[n].reshape(-1, wv[n].shape[2])], name=f"cast_{n}", out_dtype=MXU_DTYPE).reshape(wv[n].shape) for n in BIG}
    layers_w = []
    for i in range(depth):
        g = gather_chips([wbf[n][i] for n in BIG], f"gather_w_{i}")
        gw = dict(zip(BIG, g))
        layers_w.append(dict(
            w_in=_permute_w_in(gw["w_in"].transpose(1, 0, 2).reshape(D_MODEL, PROJ_W)),
            w_o=gw["w_o"].reshape(D_MODEL, D_MODEL),
            w_ffn_in=gw["w_ffn_in"].transpose(1, 0, 2).reshape(D_MODEL, 2 * D_FF),
            w_ffn_out=gw["w_ffn_out"].reshape(D_FF, D_MODEL)))

    tabs = _rope_tables(n_lat, lc)
    prms = []
    for i in range(depth):
        prms.append(dict(
            ret_logit=_slab(ret_decay_logit[i].reshape(-1)), conv_w=conv_full[i],
            a_log=_slab(dn_a_log[i].reshape(-1)), dt_b=_slab(dn_dt_bias[i].reshape(-1)), dn_norm_w=dn_norm_w[i:i + 1],
            qn_w=att_qn_w[i:i + 1], kn_w=att_kn_w[i:i + 1], ln1_w=ln1_w[i:i + 1], ln1_b=ln1_b[i:i + 1],
            ln2_w=ln2_w[i:i + 1], ln2_b=ln2_b[i:i + 1]))

    rows = jnp.concatenate([ctx[0], x[0]], 0)
    saved = []
    for i in range(depth):
        rows, sv = _layer_fwd(rows, mods[i], layers_w[i], prms[i], tabs, lc, tm, i)
        saved.append(sv)
    loss_local, dy = loss_and_grad(rows[lc:], loss_target[0], tm, "loss")
    loss = lax.psum(loss_local[0, 0], ("x", "y", "c"))

    drows = jnp.concatenate([jnp.zeros((lc, D_MODEL), F32), dy], 0)
    dmods, big_g, small_g = [None] * depth, [None] * depth, [None] * depth
    for i in reversed(range(depth)):
        drows, dmods[i], big, small_g[i] = _layer_bwd(drows, saved[i], mods[i], layers_w[i], prms[i], tabs, lc, tm, i)
        big["w_in"] = _unpermute_dw_in(big["w_in"])
        big_g[i] = dict(zip(BIG, reduce_scatter([_chip_major(big[n], n) for n in BIG], str(i))))
    grad_x = drows[lc:][None]

    names = ("ln1_w", "ln1_b", "ln2_w", "ln2_b", "dn_norm_w", "att_qn_w", "att_kn_w", "dn_conv_w", "ret_decay_logit",
             "dn_a_log", "dn_dt_bias")
    loc = [jnp.stack(dmods)] + [jnp.stack([small_g[i][n] for i in range(depth)]) for n in names]
    slab2, offs2 = _pack(loc)
    g2 = allgather8(slab2, "gather_small").reshape(8, -1, LANES)
    tot = _unpack(sum_entries(g2, tuple(range(8)), "sum_small"), offs2)
    dmods_sum = tot[0]
    gsm = dict(zip(names, tot[1:]))
    dmod_lat = jnp.stack([_unpack(g2[d], offs2)[0][:, 1].reshape(depth, 6 * D_MODEL) for d in range(8)], 1)
    dmod_ctx = dmods_sum[:, 0].reshape(depth, 1, 6 * D_MODEL)
    dm = jnp.concatenate([dmod_lat, dmod_ctx, jnp.zeros((depth, LANES - 9, 6 * D_MODEL), F32)], 1)
    dm_sh = lax.dynamic_slice(dm, (0, 0, chip * ada_w), (depth, LANES, ada_w))
    g_w_ada = jnp.stack([mm(cond, dm_sh[i], "tn", name=f"ada_dw_{i}") for i in range(depth)])
    gsm["b_ada"] = ew_sum([dmods_sum[:, 0].reshape(-1, LANES), dmods_sum[:, 1].reshape(-1, LANES)], name="b_ada_sum").reshape(b_ada.shape)
    dctx_rows = jnp.concatenate([dm_sh[:, 8:9], jnp.zeros((depth, 15, ada_w), F32)], 1)
    part = ew_sum([mm(dctx_rows[i], w_ada[i], "nt", name=f"ada_dcond_{i}") for i in range(depth)], name="ada_dcond_sum")[0]
    slab3, offs3 = _pack([part])
    g3 = allgather8(slab3, "gather_dcond").reshape(8, -1, LANES)
    dcond_ctx = _unpack(sum_entries(g3, (0, 2, 4, 6), "sum_dcond"), offs3)[0]
    (dc_ctx,), _, _ = rowwise_bwd("c_ctx_silu_b", lambda a: (_silu(a),), [c_ctx.reshape(16, LANES)], [], [],
                                  [dcond_ctx.reshape(16, LANES)], [True], 16, 0)
    gsm["c_ctx"] = dc_ctx.reshape(c_ctx.shape)
    gsm["dn_conv_w"] = lax.dynamic_slice(gsm["dn_conv_w"], (0, 0, chip * conv_sh), (depth, DN_CONV_K, conv_sh))

    grads, delta, new_m, new_v = {}, {}, {}, {}
    gs, offs = _pack([gsm[n] for n in SMALL])
    ws, _ = _pack([wv[n] for n in SMALL])
    ms, _ = _pack([mv[n] for n in SMALL])
    vs, _ = _pack([vv[n] for n in SMALL])
    res = [_unpack(o, offs) for o in adamw(gs, ws, ms, vs, "adamw_small")]
    for j, n in enumerate(SMALL):
        grads[n], delta[n], new_m[n], new_v[n] = gsm[n], res[0][j], res[1][j], res[2][j]
    bigs = {n: jnp.stack([big_g[i][n] for i in range(depth)]) for n in BIG}
    bigs["w_ada"] = g_w_ada
    for n, g in bigs.items():
        shp = wv[n].shape
        flat = lambda a: a.reshape(-1, shp[2])
        d_, m_, v_ = adamw(flat(g), flat(wv[n]), flat(mv[n]), flat(vv[n]), f"adamw_{n}")
        grads[n], delta[n], new_m[n], new_v[n] = g.reshape(shp), d_.reshape(shp), m_.reshape(shp), v_.reshape(shp)
    return (loss, grad_x, *[grads[n] for n in WEIGHTS], *[delta[n] for n in WEIGHTS], *[new_m[n] for n in WEIGHTS],
            *[new_v[n] for n in WEIGHTS])
```
